```python
import math
import jax, jax.numpy as jnp
from jax import lax
import numpy as np

D_MODEL = 2048
BATCH = 1
SEQ = 8192
DEPTH = 2

MIX_WIDTH = D_MODEL
ATT_WIDTH = MIX_WIDTH // 2
SSM_WIDTH = MIX_WIDTH - ATT_WIDTH
DIFF_HEAD_DIM = 64
V_HEAD_DIM = 2 * DIFF_HEAD_DIM
N_ATT_HEADS = ATT_WIDTH // V_HEAD_DIM
SSM_GROUP = 16
N_SSM_GROUPS = SSM_WIDTH // SSM_GROUP
SSM_STATE = 64
IN_WIDTH = 3 * ATT_WIDTH + SSM_WIDTH
D_FF = 256 * ((8 * D_MODEL // 3 + 255) // 256)
N_BUCKETS = 32
MAX_DISTANCE = 128
Q_BLOCK = 128
N_COND = 9
NORM_EPS = 1e-6
SUBLN_EPS = 1e-5
NEG_INF = -1e30

kernel_name = 'hybrid_diffattn_s5_macaron_block'


def rms_norm(x, g, eps=NORM_EPS):
    xf = x.astype(jnp.float32)
    y = xf * lax.rsqrt(jnp.mean(xf * xf, axis=-1, keepdims=True) + eps)
    return (y * g.astype(jnp.float32)).astype(x.dtype)


def modulate(h, shift, scale):
    return h * (1.0 + scale[:, None, :]) + shift[:, None, :]


def swiglu(h, w_gate, w_up, w_down):
    return (jax.nn.silu(h @ w_gate) * (h @ w_up)) @ w_down


def t5_causal_buckets(dist):
    max_exact = N_BUCKETS // 2
    d = jnp.maximum(dist, 1).astype(jnp.float32)
    large = max_exact + (jnp.log(d / max_exact) / math.log(MAX_DISTANCE / max_exact)
                         * (N_BUCKETS - max_exact)).astype(jnp.int32)
    large = jnp.minimum(large, N_BUCKETS - 1)
    return jnp.where(dist < max_exact, dist, large)


def diff_attention(q, k, v, bias_dist, lam):
    b, L = q.shape[0], q.shape[1]
    nb = L // Q_BLOCK
    scale = DIFF_HEAD_DIM ** -0.5
    k_pos = jnp.arange(L)
    q_blocks = q.reshape(b, nb, Q_BLOCK, N_ATT_HEADS, 2, DIFF_HEAD_DIM).transpose(1, 0, 2, 3, 4, 5)

    def one_block(args):
        q_blk, idx = args
        q_pos = idx * Q_BLOCK + jnp.arange(Q_BLOCK)
        dist = q_pos[:, None] - k_pos[None, :]
        s = jnp.einsum('bqhme,bkhme->bhmqk', q_blk, k).astype(jnp.float32) * scale
        bias = bias_dist[jnp.clip(dist, 0, L - 1)].transpose(2, 0, 1)
        s = jnp.where(dist >= 0, s + bias[None, :, None], NEG_INF)
        p = jax.nn.softmax(s, axis=-1)
        a = p[:, :, 0] - lam * p[:, :, 1]
        return jnp.einsum('bhqk,bkhd->bqhd', a.astype(v.dtype), v)

    o = lax.map(one_block, (q_blocks, jnp.arange(nb)))
    return o.transpose(1, 0, 2, 3, 4).reshape(b, L, N_ATT_HEADS, V_HEAD_DIM)


def ssm_combine(e1, e2):
    a1, b1 = e1
    a2, b2 = e2
    return a1 * a2, a2 * b1 + b2


def s5_ssm(u, lam_re, lam_im, log_step, b_re, b_im, c_re, c_im, d):
    f32 = jnp.float32
    uf = u.astype(f32)
    lam = lax.complex(jnp.minimum(lam_re.astype(f32), -1e-4), lam_im.astype(f32))
    step = jnp.exp(log_step.astype(f32))[:, None]
    lam_bar = jnp.exp(lam * step)
    b_bar = ((lam_bar - 1.0) / lam)[:, :, None] * lax.complex(b_re.astype(f32), b_im.astype(f32))
    bu = jnp.einsum('blgh,gph->blgp', uf.astype(jnp.complex64), b_bar)
    a = jnp.broadcast_to(lam_bar, bu.shape)
    _, states = lax.associative_scan(ssm_combine, (a, bu), axis=1)
    cm = lax.complex(c_re.astype(f32), c_im.astype(f32))
    y = jnp.einsum('blgp,ghp->blgh', states, cm).real + d.astype(f32) * uf
    return y.astype(u.dtype)


def hybrid_mixer(h, bias_dist, layer_idx, w_in, w_out, q_norm_g, k_norm_g,
                 lq1, lk1, lq2, lk2, subln_g,
                 lam_re, lam_im, log_step, b_re, b_im, c_re, c_im, d, glu_w, glu_b):
    b, L = h.shape[0], h.shape[1]
    proj = h @ w_in
    q, k, v, u = jnp.split(proj, [ATT_WIDTH, 2 * ATT_WIDTH, 3 * ATT_WIDTH], axis=-1)
    q = rms_norm(q.reshape(b, L, N_ATT_HEADS, 2, DIFF_HEAD_DIM), q_norm_g)
    k = rms_norm(k.reshape(b, L, N_ATT_HEADS, 2, DIFF_HEAD_DIM), k_norm_g)
    v = v.reshape(b, L, N_ATT_HEADS, V_HEAD_DIM)
    lam_init = 0.8 - 0.6 * math.exp(-0.3 * layer_idx)
    f32 = jnp.float32
    lam = (jnp.exp(jnp.sum(lq1.astype(f32) * lk1.astype(f32)))
           - jnp.exp(jnp.sum(lq2.astype(f32) * lk2.astype(f32))) + lam_init)
    att = diff_attention(q, k, v, bias_dist, lam)
    att = (rms_norm(att, subln_g, SUBLN_EPS) * (1.0 - lam_init)).reshape(b, L, ATT_WIDTH)
    y = s5_ssm(u.reshape(b, L, N_SSM_GROUPS, SSM_GROUP), lam_re, lam_im, log_step,
               b_re, b_im, c_re, c_im, d)
    y = jax.nn.gelu(y.reshape(b, L, SSM_WIDTH))
    y = y * jax.nn.sigmoid(y @ glu_w + glu_b)
    return jnp.concatenate([att, y], axis=-1) @ w_out


def setup_inputs(seed: int = 0) -> dict:
    key = jax.random.key(seed)
    ks = jax.random.split(key, 32)
    f32 = jnp.float32

    def nrm(k, shape, scale):
        return jax.random.normal(k, shape, f32) * scale

    P, G, Hc = SSM_STATE, N_SSM_GROUPS, SSM_GROUP
    lam_im0 = math.pi * jnp.arange(P, dtype=f32)
    return {
        'x': nrm(ks[0], (BATCH, SEQ, D_MODEL), 1.0),
        'c': nrm(ks[1], (BATCH, D_MODEL), 1.0),
        'rel_bias': nrm(ks[2], (N_BUCKETS, N_ATT_HEADS), 0.5),
        'ada_w': nrm(ks[3], (DEPTH, D_MODEL, N_COND * D_MODEL), D_MODEL ** -0.5),
        'ada_b': nrm(ks[4], (DEPTH, N_COND * D_MODEL), 0.02),
        'norm_g': 1.0 + nrm(ks[5], (DEPTH, 3, D_MODEL), 0.02),
        'ffn1_w_gate': nrm(ks[6], (DEPTH, D_MODEL, D_FF), D_MODEL ** -0.5),
        'ffn1_w_up': nrm(ks[7], (DEPTH, D_MODEL, D_FF), D_MODEL ** -0.5),
        'ffn1_w_down': nrm(ks[8], (DEPTH, D_FF, D_MODEL), D_FF ** -0.5),
        'ffn2_w_gate': nrm(ks[9], (DEPTH, D_MODEL, D_FF), D_MODEL ** -0.5),
        'ffn2_w_up': nrm(ks[10], (DEPTH, D_MODEL, D_FF), D_MODEL ** -0.5),
        'ffn2_w_down': nrm(ks[11], (DEPTH, D_FF, D_MODEL), D_FF ** -0.5),
        'w_in': nrm(ks[12], (DEPTH, D_MODEL, IN_WIDTH), D_MODEL ** -0.5),
        'w_out': nrm(ks[13], (DEPTH, MIX_WIDTH, D_MODEL), MIX_WIDTH ** -0.5),
        'q_norm_g': 1.0 + nrm(ks[14], (DEPTH, DIFF_HEAD_DIM), 0.02),
        'k_norm_g': 1.0 + nrm(ks[15], (DEPTH, DIFF_HEAD_DIM), 0.02),
        'lambda_q1': nrm(ks[16], (DEPTH, DIFF_HEAD_DIM), 0.1),
        'lambda_k1': nrm(ks[17], (DEPTH, DIFF_HEAD_DIM), 0.1),
        'lambda_q2': nrm(ks[18], (DEPTH, DIFF_HEAD_DIM), 0.1),
        'lambda_k2': nrm(ks[19], (DEPTH, DIFF_HEAD_DIM), 0.1),
        'subln_g': 1.0 + nrm(ks[20], (DEPTH, V_HEAD_DIM), 0.02),
        'ssm_lambda_re': -0.5 + nrm(ks[21], (DEPTH, G, P), 0.01),
        'ssm_lambda_im': lam_im0 + nrm(ks[22], (DEPTH, G, P), 0.01),
        'ssm_log_step': jax.random.uniform(ks[23], (DEPTH, G), f32,
                                           math.log(1e-3), math.log(1e-1)),
        'ssm_b_re': nrm(ks[24], (DEPTH, G, P, Hc), (2.0 * Hc) ** -0.5),
        'ssm_b_im': nrm(ks[25], (DEPTH, G, P, Hc), (2.0 * Hc) ** -0.5),
        'ssm_c_re': nrm(ks[26], (DEPTH, G, Hc, P), (2.0 * P) ** -0.5),
        'ssm_c_im': nrm(ks[27], (DEPTH, G, Hc, P), (2.0 * P) ** -0.5),
        'ssm_d': nrm(ks[28], (DEPTH, G, Hc), 1.0),
        'ssm_glu_w': nrm(ks[29], (DEPTH, SSM_WIDTH, SSM_WIDTH), SSM_WIDTH ** -0.5),
        'ssm_glu_b': nrm(ks[30], (DEPTH, SSM_WIDTH), 0.02),
    }


def reference(x, c, rel_bias, ada_w, ada_b, norm_g,
              ffn1_w_gate, ffn1_w_up, ffn1_w_down, ffn2_w_gate, ffn2_w_up, ffn2_w_down,
              w_in, w_out, q_norm_g, k_norm_g, lambda_q1, lambda_k1, lambda_q2, lambda_k2,
              subln_g, ssm_lambda_re, ssm_lambda_im, ssm_log_step,
              ssm_b_re, ssm_b_im, ssm_c_re, ssm_c_im, ssm_d, ssm_glu_w, ssm_glu_b):
    L = x.shape[1]
    bias_dist = rel_bias.astype(jnp.float32)[t5_causal_buckets(jnp.arange(L, dtype=jnp.int32))]
    cond = jax.nn.silu(c)
    for i in range(DEPTH):
        mod = cond @ ada_w[i] + ada_b[i]
        sh1, sc1, g1, sh2, sc2, g2, sh3, sc3, g3 = jnp.split(mod, N_COND, axis=-1)
        h = modulate(rms_norm(x, norm_g[i, 0]), sh1, sc1)
        x = x + 0.5 * g1[:, None, :] * swiglu(h, ffn1_w_gate[i], ffn1_w_up[i], ffn1_w_down[i])
        h = modulate(rms_norm(x, norm_g[i, 1]), sh2, sc2)
        m = hybrid_mixer(h, bias_dist, i, w_in[i], w_out[i], q_norm_g[i], k_norm_g[i],
                         lambda_q1[i], lambda_k1[i], lambda_q2[i], lambda_k2[i], subln_g[i],
                         ssm_lambda_re[i], ssm_lambda_im[i], ssm_log_step[i],
                         ssm_b_re[i], ssm_b_im[i], ssm_c_re[i], ssm_c_im[i], ssm_d[i],
                         ssm_glu_w[i], ssm_glu_b[i])
        x = x + g2[:, None, :] * m
        h = modulate(rms_norm(x, norm_g[i, 2]), sh3, sc3)
        x = x + 0.5 * g3[:, None, :] * swiglu(h, ffn2_w_gate[i], ffn2_w_up[i], ffn2_w_down[i])
    return x
```

```python
import functools
import math

import jax
import jax.numpy as jnp
from jax import lax
from jax.experimental import pallas as pl
from jax.experimental.pallas import tpu as pltpu

D_MODEL = 2048
SEQ = 8192
DEPTH = 2
ATT_WIDTH = 1024
SSM_WIDTH = 1024
DIFF_HEAD_DIM = 64
V_HEAD_DIM = 128
N_ATT_HEADS = 8
SSM_GROUP = 16
N_SSM_GROUPS = 64
SSM_STATE = 64
IN_WIDTH = 4096
D_FF = 5632
N_BUCKETS = 32
MAX_DISTANCE = 128
N_COND = 9
NORM_EPS = 1e-6
SUBLN_EPS = 1e-5
NEG_INF = -1e30

F32 = jnp.float32
BF16 = jnp.bfloat16

VMEM_LIMIT_BYTES = 56 * 1024 * 1024

ADA_TN = 1024
FFN_TM = 1024
FFN_TF = 256
PROJ_TM = 512
ATT_T = 256
SSM_TT = 1024
SSM_SLAB_GROUPS = 8
OUT_TM = 512


def _cparams(sem):
    return pltpu.CompilerParams(dimension_semantics=sem, vmem_limit_bytes=VMEM_LIMIT_BYTES)


def _ada_kernel(c_ref, w_ref, b_ref, o_ref):
    c = c_ref[...]
    cs = c * jax.nn.sigmoid(c)
    o_ref[0] = jnp.sum(w_ref[0] * cs, axis=0, keepdims=True) + b_ref[0]


def _ada_call(c_col, ada_w, ada_b3):
    depth, d, n = ada_w.shape
    return pl.pallas_call(
        _ada_kernel,
        grid=(depth, n // ADA_TN),
        in_specs=[
            pl.BlockSpec((d, 1), lambda l, j: (0, 0)),
            pl.BlockSpec((1, d, ADA_TN), lambda l, j: (l, 0, j)),
            pl.BlockSpec((1, 1, ADA_TN), lambda l, j: (l, 0, j)),
        ],
        out_specs=pl.BlockSpec((1, 1, ADA_TN), lambda l, j: (l, 0, j)),
        out_shape=jax.ShapeDtypeStruct((depth, 1, n), F32),
        compiler_params=_cparams(("arbitrary", "arbitrary")),
        name="adaln",
    )(c_col, ada_w, ada_b3)


def _norm_mod(x, g, sc, sh):
    ms = jnp.mean(x * x, axis=-1, keepdims=True)
    return (x * lax.rsqrt(ms + NORM_EPS) * g) * (1.0 + sc) + sh


def _ffn_kernel(x_ref, ng_ref, sh_ref, sc_ref, gt_ref, wg_ref, wu_ref, wd_ref, o_ref, h_ref):
    j = pl.program_id(1)

    @pl.when(j == 0)
    def _():
        h_ref[...] = _norm_mod(x_ref[...], ng_ref[...], sc_ref[...], sh_ref[...]).astype(BF16)
        o_ref[...] = jnp.zeros_like(o_ref)

    h = h_ref[...]
    g = jnp.dot(h, wg_ref[...].astype(BF16), preferred_element_type=F32)
    u = jnp.dot(h, wu_ref[...].astype(BF16), preferred_element_type=F32)
    a = (g * jax.nn.sigmoid(g) * u).astype(BF16)
    o_ref[...] += jnp.dot(a, wd_ref[...].astype(BF16), preferred_element_type=F32)

    @pl.when(j == pl.num_programs(1) - 1)
    def _():
        o_ref[...] = x_ref[...] + (0.5 * gt_ref[...]) * o_ref[...]


def _ffn_call(layer, x, ng, sh, sc, gt, wg, wu, wd):
    L, d = x.shape
    dff = wg.shape[-1]
    vec = pl.BlockSpec((1, d), lambda i, j: (0, 0))
    return pl.pallas_call(
        _ffn_kernel,
        grid=(L // FFN_TM, dff // FFN_TF),
        in_specs=[
            pl.BlockSpec((FFN_TM, d), lambda i, j: (i, 0), pipeline_mode=pl.Buffered(1)),
            vec, vec, vec, vec,
            pl.BlockSpec((None, d, FFN_TF), lambda i, j: (layer, 0, j)),
            pl.BlockSpec((None, d, FFN_TF), lambda i, j: (layer, 0, j)),
            pl.BlockSpec((None, FFN_TF, d), lambda i, j: (layer, j, 0)),
        ],
        out_specs=pl.BlockSpec((FFN_TM, d), lambda i, j: (i, 0)),
        out_shape=jax.ShapeDtypeStruct((L, d), F32),
        scratch_shapes=[pltpu.VMEM((FFN_TM, d), BF16)],
        compiler_params=_cparams(("arbitrary", "arbitrary")),
        name="ffn",
    )(x, ng, sh, sc, gt, wg, wu, wd)


def _proj_kernel(x_ref, ng_ref, sh_ref, sc_ref, w_ref, gm_ref, qg_ref, kg_ref,
                 q_ref, k_ref, vt_ref, u_ref):
    h = _norm_mod(x_ref[...], ng_ref[...], sc_ref[...], sh_ref[...]).astype(BF16)
    aw = ATT_WIDTH

    def head_norm(z, g):
        ms = jnp.dot((z * z).astype(BF16), gm_ref[...], preferred_element_type=F32)
        return z * lax.rsqrt(ms + NORM_EPS) * g

    q = jnp.dot(h, w_ref[:, 0:aw], preferred_element_type=F32)
    q_ref[...] = head_norm(q, qg_ref[...]).astype(BF16)
    k = jnp.dot(h, w_ref[:, aw:2 * aw], preferred_element_type=F32)
    k_ref[...] = head_norm(k, kg_ref[...]).astype(BF16)
    v = jnp.dot(h, w_ref[:, 2 * aw:3 * aw], preferred_element_type=F32)
    for b in range(PROJ_TM // ATT_T):
        vt_ref[b] = v[b * ATT_T:(b + 1) * ATT_T, :].T.astype(BF16)
    u_ref[...] = jnp.dot(h, w_ref[:, 3 * aw:4 * aw], preferred_element_type=F32)


def _proj_call(layer, x, ng, sh, sc, w_in_bf, gmat, qg, kg):
    L, d = x.shape
    aw = ATT_WIDTH
    nb = PROJ_TM // ATT_T
    vec = pl.BlockSpec((1, d), lambda i: (0, 0))
    vec_a = pl.BlockSpec((1, aw), lambda i: (0, 0))
    return pl.pallas_call(
        _proj_kernel,
        grid=(L // PROJ_TM,),
        in_specs=[
            pl.BlockSpec((PROJ_TM, d), lambda i: (i, 0)),
            vec, vec, vec,
            pl.BlockSpec((None, d, IN_WIDTH), lambda i: (layer, 0, 0),
                         pipeline_mode=pl.Buffered(1)),
            pl.BlockSpec((aw, aw), lambda i: (0, 0), pipeline_mode=pl.Buffered(1)),
            vec_a, vec_a,
        ],
        out_specs=[
            pl.BlockSpec((PROJ_TM, aw), lambda i: (i, 0)),
            pl.BlockSpec((PROJ_TM, aw), lambda i: (i, 0)),
            pl.BlockSpec((nb, aw, ATT_T), lambda i: (i, 0, 0)),
            pl.BlockSpec((PROJ_TM, SSM_WIDTH), lambda i: (i, 0)),
        ],
        out_shape=[
            jax.ShapeDtypeStruct((L, aw), BF16),
            jax.ShapeDtypeStruct((L, aw), BF16),
            jax.ShapeDtypeStruct((L // ATT_T, aw, ATT_T), BF16),
            jax.ShapeDtypeStruct((L, SSM_WIDTH), F32),
        ],
        compiler_params=_cparams(("arbitrary",)),
        name="in_proj",
    )(x, ng, sh, sc, w_in_bf, gmat, qg, kg)


def _attn_kernel(q_ref, k_ref, vt_ref, bias_ref, lq1_ref, lk1_ref, lq2_ref, lk2_ref, sg_ref,
                 o_ref, acc_ref, m_ref, l_ref, *, lam_init):
    t = ATT_T
    i = pl.program_id(1)
    q = q_ref[...]
    lane = lax.broadcasted_iota(jnp.int32, q.shape, 1)
    zero = jnp.zeros_like(q)
    qs = (jnp.where(lane < DIFF_HEAD_DIM, q, zero), jnp.where(lane >= DIFF_HEAD_DIM, q, zero))
    m_ref[...] = jnp.full(m_ref.shape, NEG_INF, F32)
    l_ref[...] = jnp.zeros(l_ref.shape, F32)
    acc_ref[...] = jnp.zeros(acc_ref.shape, F32)

    def body(j, carry):
        off = pl.multiple_of(j * t, t)
        kb = k_ref[pl.ds(off, t), :]
        vtb = vt_ref[j]
        bias = bias_ref[0, jnp.minimum(i - j, 2)]
        for mi in range(2):
            s = lax.dot_general(kb, qs[mi], (((1,), (1,)), ((), ())),
                                preferred_element_type=F32) + bias
            m_old = m_ref[mi]
            m_new = jnp.maximum(m_old, jnp.max(s, axis=0, keepdims=True))
            alpha = jnp.exp(m_old - m_new)
            p = jnp.exp(s - m_new)
            l_ref[mi] = alpha * l_ref[mi] + jnp.sum(p, axis=0, keepdims=True)
            acc_ref[mi] = alpha * acc_ref[mi] + jnp.dot(vtb, p.astype(BF16),
                                                        preferred_element_type=F32)
            m_ref[mi] = m_new
        return carry

    lax.fori_loop(0, i + 1, body, 0)

    lam = (jnp.exp(jnp.sum(lq1_ref[...] * lk1_ref[...], keepdims=True))
           - jnp.exp(jnp.sum(lq2_ref[...] * lk2_ref[...], keepdims=True)) + lam_init)
    o = acc_ref[0] / l_ref[0] - lam * (acc_ref[1] / l_ref[1])
    ms = jnp.mean(o * o, axis=0, keepdims=True)
    on = o * lax.rsqrt(ms + SUBLN_EPS) * (sg_ref[...] * (1.0 - lam_init))
    o_ref[...] = on.T.astype(BF16)


def _attn_call(layer_idx, q, k, vt, bias_t, lq1, lk1, lq2, lk2, sg_col):
    L = q.shape[0]
    t = ATT_T
    lam_init = 0.8 - 0.6 * math.exp(-0.3 * layer_idx)
    lvec = pl.BlockSpec((1, DIFF_HEAD_DIM), lambda h, i: (0, 0))
    return pl.pallas_call(
        functools.partial(_attn_kernel, lam_init=lam_init),
        grid=(N_ATT_HEADS, L // t),
        in_specs=[
            pl.BlockSpec((t, V_HEAD_DIM), lambda h, i: (i, h)),
            pl.BlockSpec((L, V_HEAD_DIM), lambda h, i: (0, h)),
            pl.BlockSpec((L // t, V_HEAD_DIM, t), lambda h, i: (0, h, 0)),
            pl.BlockSpec((1, 3, t, t), lambda h, i: (h, 0, 0, 0)),
            lvec, lvec, lvec, lvec,
            pl.BlockSpec((V_HEAD_DIM, 1), lambda h, i: (0, 0)),
        ],
        out_specs=pl.BlockSpec((t, V_HEAD_DIM), lambda h, i: (i, h)),
        out_shape=jax.ShapeDtypeStruct((L, ATT_WIDTH), BF16),
        scratch_shapes=[
            pltpu.VMEM((2, V_HEAD_DIM, t), F32),
            pltpu.VMEM((2, 1, t), F32),
            pltpu.VMEM((2, 1, t), F32),
        ],
        compiler_params=_cparams(("arbitrary", "arbitrary")),
        name="diff_attn",
    )(q, k, vt, bias_t, lq1, lk1, lq2, lk2, sg_col)


def _ssm_kernel(u_ref, bm_ref, cm_ref, lam_ref, lamr_ref, d_ref, y_ref, bu_ref, carry_ref):
    ns = SSM_SLAB_GROUPS * SSM_STATE
    nr = SSM_TT // 8
    tt = pl.program_id(1)

    @pl.when(tt == 0)
    def _():
        carry_ref[...] = jnp.zeros_like(carry_ref)

    u = u_ref[...]
    bu_ref[...] = jnp.dot(u.astype(BF16), bm_ref[0], preferred_element_type=F32)
    lr = jnp.broadcast_to(lam_ref[0, 0:1, :], (8, ns))
    li = jnp.broadcast_to(lam_ref[0, 1:2, :], (8, ns))

    def advance(r, sr, si):
        row = pl.multiple_of(r * 8, 8)
        br = bu_ref[pl.ds(row, 8), 0:ns]
        bi = bu_ref[pl.ds(row, 8), ns:2 * ns]
        return row, lr * sr - li * si + br, lr * si + li * sr + bi

    def local_step(r, s):
        _, nsr, nsi = advance(r, s[0], s[1])
        return nsr, nsi

    z = jnp.zeros((8, ns), F32)
    er, ei = lax.fori_loop(0, nr, local_step, (z, z), unroll=4)

    cr = carry_ref[0:1, 0:ns]
    ci = carry_ref[0:1, ns:2 * ns]
    pr = lamr_ref[0, 0:1, :]
    pi = lamr_ref[0, 1:2, :]
    rows = lax.broadcasted_iota(jnp.int32, (8, ns), 0)
    init_r, init_i = z, z
    for c in range(8):
        init_r = jnp.where(rows == c, cr, init_r)
        init_i = jnp.where(rows == c, ci, init_i)
        cr, ci = (pr * cr - pi * ci + er[c:c + 1, :], pr * ci + pi * cr + ei[c:c + 1, :])
    carry_ref[0:1, 0:ns] = cr
    carry_ref[0:1, ns:2 * ns] = ci

    def true_step(r, s):
        row, nsr, nsi = advance(r, s[0], s[1])
        bu_ref[pl.ds(row, 8), 0:ns] = nsr
        bu_ref[pl.ds(row, 8), ns:2 * ns] = nsi
        return nsr, nsi

    lax.fori_loop(0, nr, true_step, (init_r, init_i), unroll=4)

    y = jnp.dot(bu_ref[...].astype(BF16), cm_ref[0], preferred_element_type=F32) + d_ref[0] * u
    y_ref[...] = jax.nn.gelu(y)


def _ssm_call(u_perm, bmat, cmat, lam2, lamr2, d3):
    L = u_perm.shape[0]
    n_slab = N_SSM_GROUPS // SSM_SLAB_GROUPS
    ns = SSM_SLAB_GROUPS * SSM_STATE
    cw = SSM_SLAB_GROUPS * SSM_GROUP
    return pl.pallas_call(
        _ssm_kernel,
        grid=(n_slab, L // SSM_TT),
        in_specs=[
            pl.BlockSpec((SSM_TT, cw), lambda s, t: (t, s)),
            pl.BlockSpec((1, cw, 2 * ns), lambda s, t: (s, 0, 0)),
            pl.BlockSpec((1, 2 * ns, cw), lambda s, t: (s, 0, 0)),
            pl.BlockSpec((1, 2, ns), lambda s, t: (s, 0, 0)),
            pl.BlockSpec((1, 2, ns), lambda s, t: (s, 0, 0)),
            pl.BlockSpec((1, 1, cw), lambda s, t: (s, 0, 0)),
        ],
        out_specs=pl.BlockSpec((SSM_TT, cw), lambda s, t: (t, s)),
        out_shape=jax.ShapeDtypeStruct((L, SSM_WIDTH), F32),
        scratch_shapes=[
            pltpu.VMEM((SSM_TT, 2 * ns), F32),
            pltpu.VMEM((1, 2 * ns), F32),
        ],
        compiler_params=_cparams(("arbitrary", "arbitrary")),
        name="s5_scan",
    )(u_perm, bmat, cmat, lam2, lamr2, d3)


def _ssm_params(lam_re, lam_im, log_step, b_re, b_im, c_re, c_im, d):
    g, p, hc = N_SSM_GROUPS, SSM_STATE, SSM_GROUP
    sg = SSM_SLAB_GROUPS
    n_slab = g // sg
    lam = lax.complex(jnp.minimum(lam_re.astype(F32), -1e-4), lam_im.astype(F32))
    step = jnp.exp(log_step.astype(F32))[:, None]
    lam_bar = jnp.exp(lam * step)
    lam_bar_r = jnp.exp(lam * (step * (SSM_TT // 8)))
    b_bar = ((lam_bar - 1.0) / lam)[:, :, None] * lax.complex(b_re.astype(F32), b_im.astype(F32))
    eye = jnp.eye(sg, dtype=F32)

    def b_block(part):
        z = part.reshape(n_slab, sg, p, hc).transpose(0, 1, 3, 2)
        return (z[:, :, :, None, :] * eye[None, :, None, :, None]).reshape(n_slab, sg * hc, sg * p)

    def c_block(part):
        z = part.reshape(n_slab, sg, hc, p).transpose(0, 1, 3, 2)
        return (z[:, :, :, None, :] * eye[None, :, None, :, None]).reshape(n_slab, sg * p, sg * hc)

    bmat = jnp.concatenate([b_block(jnp.real(b_bar)), b_block(jnp.imag(b_bar))], axis=-1)
    cmat = jnp.concatenate([c_block(c_re.astype(F32)), c_block(-c_im.astype(F32))], axis=1)

    def rows(zc):
        return jnp.stack([jnp.real(zc).reshape(n_slab, sg * p),
                          jnp.imag(zc).reshape(n_slab, sg * p)], axis=1)

    d3 = d.astype(F32).reshape(n_slab, 1, sg * hc)
    return bmat.astype(BF16), cmat.astype(BF16), rows(lam_bar), rows(lam_bar_r), d3


def _out_kernel(att_ref, y_ref, gw_ref, gb_ref, woa_ref, wob_ref, x_ref, gt_ref, o_ref):
    y = y_ref[...]
    z = jnp.dot(y.astype(BF16), gw_ref[...], preferred_element_type=F32) + gb_ref[...]
    yg = (y * jax.nn.sigmoid(z)).astype(BF16)
    m = (jnp.dot(att_ref[...], woa_ref[...], preferred_element_type=F32)
         + jnp.dot(yg, wob_ref[...], preferred_element_type=F32))
    o_ref[...] = x_ref[...] + gt_ref[...] * m


def _out_call(layer, att, y, glu_w_bf, glu_b, w_out_bf, x, gt):
    L, d = x.shape
    aw, sw = ATT_WIDTH, SSM_WIDTH
    one = pl.Buffered(1)
    return pl.pallas_call(
        _out_kernel,
        grid=(L // OUT_TM,),
        in_specs=[
            pl.BlockSpec((OUT_TM, aw), lambda i: (i, 0)),
            pl.BlockSpec((OUT_TM, sw), lambda i: (i, 0)),
            pl.BlockSpec((None, sw, sw), lambda i: (layer, 0, 0), pipeline_mode=one),
            pl.BlockSpec((1, sw), lambda i: (0, 0)),
            pl.BlockSpec((None, aw, d), lambda i: (layer, 0, 0), pipeline_mode=one),
            pl.BlockSpec((None, sw, d), lambda i: (layer, 1, 0), pipeline_mode=one),
            pl.BlockSpec((OUT_TM, d), lambda i: (i, 0)),
            pl.BlockSpec((1, d), lambda i: (0, 0)),
        ],
        out_specs=pl.BlockSpec((OUT_TM, d), lambda i: (i, 0)),
        out_shape=jax.ShapeDtypeStruct((L, d), F32),
        compiler_params=_cparams(("arbitrary",)),
        name="out_proj",
    )(att, y, glu_w_bf, glu_b, w_out_bf, w_out_bf, x, gt)


def _t5_causal_buckets(dist):
    max_exact = N_BUCKETS // 2
    d = jnp.maximum(dist, 1).astype(F32)
    large = max_exact + (jnp.log(d / max_exact) / math.log(MAX_DISTANCE / max_exact)
                         * (N_BUCKETS - max_exact)).astype(jnp.int32)
    large = jnp.minimum(large, N_BUCKETS - 1)
    return jnp.where(dist < max_exact, dist, large)


def _bias_tiles(rel_bias, L):
    t = ATT_T
    assert t >= MAX_DISTANCE
    bias_dist = rel_bias.astype(F32)[_t5_causal_buckets(jnp.arange(L, dtype=jnp.int32))]
    kk = jnp.arange(t)[:, None]
    qq = jnp.arange(t)[None, :]
    tiles = []
    for b in range(3):
        dist = b * t + qq - kk
        val = bias_dist[jnp.clip(dist, 0, L - 1)]
        tiles.append(jnp.where((dist >= 0)[:, :, None], val, NEG_INF))
    return jnp.stack(tiles, axis=0).transpose(3, 0, 1, 2)


def _chunk_interleave(a):
    L, w = a.shape
    return a.reshape(L // SSM_TT, 8, SSM_TT // 8, w).transpose(0, 2, 1, 3).reshape(L, w)


def _chunk_deinterleave(a):
    L, w = a.shape
    return a.reshape(L // SSM_TT, SSM_TT // 8, 8, w).transpose(0, 2, 1, 3).reshape(L, w)


def kernel(x, c, rel_bias, ada_w, ada_b, norm_g, ffn1_w_gate, ffn1_w_up, ffn1_w_down, ffn2_w_gate, ffn2_w_up, ffn2_w_down, w_in, w_out, q_norm_g, k_norm_g, lambda_q1, lambda_k1, lambda_q2, lambda_k2, subln_g, ssm_lambda_re, ssm_lambda_im, ssm_log_step, ssm_b_re, ssm_b_im, ssm_c_re, ssm_c_im, ssm_d, ssm_glu_w, ssm_glu_b):
    b, L, d = x.shape
    assert b == 1 and c.shape == (1, d)
    x2 = x.reshape(L, d)

    mod = _ada_call(c.reshape(d, 1), ada_w, ada_b.reshape(DEPTH, 1, N_COND * d))
    bias_t = _bias_tiles(rel_bias, L)
    w_in_bf = w_in.astype(BF16)
    w_out_bf = w_out.astype(BF16)
    glu_w_bf = ssm_glu_w.astype(BF16)
    hd = DIFF_HEAD_DIM
    group_of = jnp.arange(ATT_WIDTH) // hd
    gmat = jnp.where(group_of[:, None] == group_of[None, :], 1.0 / hd, 0.0).astype(BF16)
    n_rep = ATT_WIDTH // hd

    for i in range(DEPTH):
        m = mod[i]
        sh1, sc1, g1, sh2, sc2, g2, sh3, sc3, g3 = [m[:, n * d:(n + 1) * d] for n in range(N_COND)]
        x2 = _ffn_call(i, x2, norm_g[i, 0][None], sh1, sc1, g1,
                       ffn1_w_gate, ffn1_w_up, ffn1_w_down)

        qg = jnp.tile(q_norm_g[i].astype(F32), n_rep)[None] * (hd ** -0.5)
        kg = jnp.tile(k_norm_g[i].astype(F32), n_rep)[None]
        q, k, vt, u = _proj_call(i, x2, norm_g[i, 1][None], sh2, sc2, w_in_bf, gmat, qg, kg)
        att = _attn_call(i, q, k, vt, bias_t, lambda_q1[i][None], lambda_k1[i][None],
                         lambda_q2[i][None], lambda_k2[i][None], subln_g[i].reshape(V_HEAD_DIM, 1))
        bmat, cmat, lam2, lamr2, d3 = _ssm_params(
            ssm_lambda_re[i], ssm_lambda_im[i], ssm_log_step[i], ssm_b_re[i], ssm_b_im[i],
            ssm_c_re[i], ssm_c_im[i], ssm_d[i])
        y = _chunk_deinterleave(_ssm_call(_chunk_interleave(u), bmat, cmat, lam2, lamr2, d3))
        x2 = _out_call(i, att, y, glu_w_bf, ssm_glu_b[i][None], w_out_bf, x2, g2)

        x2 = _ffn_call(i, x2, norm_g[i, 2][None], sh3, sc3, g3,
                       ffn2_w_gate, ffn2_w_up, ffn2_w_down)
    return x2.reshape(b, L, d)
```

```python
import functools
import math

import jax
import jax.numpy as jnp
from jax import lax
from jax.experimental import pallas as pl
from jax.experimental.pallas import tpu as pltpu

D_MODEL = 2048
SEQ = 8192
DEPTH = 2
ATT_WIDTH = 1024
SSM_WIDTH = 1024
DIFF_HEAD_DIM = 64
V_HEAD_DIM = 128
N_ATT_HEADS = 8
SSM_GROUP = 16
N_SSM_GROUPS = 64
SSM_STATE = 64
IN_WIDTH = 4096
D_FF = 5632
N_BUCKETS = 32
MAX_DISTANCE = 128
N_COND = 9
NORM_EPS = 1e-6
SUBLN_EPS = 1e-5
NEG_INF = -1e30
LOG2_E = math.log2(math.e)

F32 = jnp.float32
BF16 = jnp.bfloat16

VMEM_LIMIT_BYTES = 56 * 1024 * 1024

ADA_TN = 1024
FFN_TM = 1024
FFN_TF = 256
PROJ_TM = 512
ATT_T = 256
ATT_UNIT_BLOCKS = 4
ATT_MIN_DENOM = 2.0 ** -40
SSM_TT = 1024
SSM_SLAB_GROUPS = 8
OUT_TM = 512


def _cparams(sem):
    return pltpu.CompilerParams(dimension_semantics=sem, vmem_limit_bytes=VMEM_LIMIT_BYTES)


def _ada_kernel(c_ref, w_ref, b_ref, o_ref):
    c = c_ref[...]
    cs = c * jax.nn.sigmoid(c)
    o_ref[0] = jnp.sum(w_ref[0] * cs, axis=0, keepdims=True) + b_ref[0]


def _ada_call(c_col, ada_w, ada_b3):
    depth, d, n = ada_w.shape
    return pl.pallas_call(
        _ada_kernel,
        grid=(depth, n // ADA_TN),
        in_specs=[
            pl.BlockSpec((d, 1), lambda l, j: (0, 0)),
            pl.BlockSpec((1, d, ADA_TN), lambda l, j: (l, 0, j)),
            pl.BlockSpec((1, 1, ADA_TN), lambda l, j: (l, 0, j)),
        ],
        out_specs=pl.BlockSpec((1, 1, ADA_TN), lambda l, j: (l, 0, j)),
        out_shape=jax.ShapeDtypeStruct((depth, 1, n), F32),
        compiler_params=_cparams(("arbitrary", "arbitrary")),
        name="adaln",
    )(c_col, ada_w, ada_b3)


def _norm_mod(x, g, sc, sh):
    ms = jnp.mean(x * x, axis=-1, keepdims=True)
    return (x * lax.rsqrt(ms + NORM_EPS) * g) * (1.0 + sc) + sh


def _ffn_kernel(x_ref, ng_ref, sh_ref, sc_ref, gt_ref, wg_ref, wu_ref, wd_ref, o_ref, h_ref):
    j = pl.program_id(1)

    @pl.when(j == 0)
    def _():
        h_ref[...] = _norm_mod(x_ref[...], ng_ref[...], sc_ref[...], sh_ref[...]).astype(BF16)
        o_ref[...] = jnp.zeros_like(o_ref)

    h = h_ref[...]
    g = jnp.dot(h, wg_ref[...].astype(BF16), preferred_element_type=F32)
    u = jnp.dot(h, wu_ref[...].astype(BF16), preferred_element_type=F32)
    a = (g * jax.nn.sigmoid(g) * u).astype(BF16)
    o_ref[...] += jnp.dot(a, wd_ref[...].astype(BF16), preferred_element_type=F32)

    @pl.when(j == pl.num_programs(1) - 1)
    def _():
        o_ref[...] = x_ref[...] + (0.5 * gt_ref[...]) * o_ref[...]


def _ffn_call(layer, x, ng, sh, sc, gt, wg, wu, wd):
    L, d = x.shape
    dff = wg.shape[-1]
    vec = pl.BlockSpec((1, d), lambda i, j: (0, 0))
    return pl.pallas_call(
        _ffn_kernel,
        grid=(L // FFN_TM, dff // FFN_TF),
        in_specs=[
            pl.BlockSpec((FFN_TM, d), lambda i, j: (i, 0), pipeline_mode=pl.Buffered(1)),
            vec, vec, vec, vec,
            pl.BlockSpec((None, d, FFN_TF), lambda i, j: (layer, 0, j)),
            pl.BlockSpec((None, d, FFN_TF), lambda i, j: (layer, 0, j)),
            pl.BlockSpec((None, FFN_TF, d), lambda i, j: (layer, j, 0)),
        ],
        out_specs=pl.BlockSpec((FFN_TM, d), lambda i, j: (i, 0)),
        out_shape=jax.ShapeDtypeStruct((L, d), F32),
        scratch_shapes=[pltpu.VMEM((FFN_TM, d), BF16)],
        compiler_params=_cparams(("arbitrary", "arbitrary")),
        name="ffn",
    )(x, ng, sh, sc, gt, wg, wu, wd)


def _proj_kernel(x_ref, ng_ref, sh_ref, sc_ref, w_ref, gm_ref, qg_ref, kg_ref,
                 q_ref, k_ref, vt_ref, u_ref):
    h = _norm_mod(x_ref[...], ng_ref[...], sc_ref[...], sh_ref[...]).astype(BF16)
    aw = ATT_WIDTH

    def head_norm(z, g):
        ms = jnp.dot((z * z).astype(BF16), gm_ref[...], preferred_element_type=F32)
        return z * lax.rsqrt(ms + NORM_EPS) * g

    q = jnp.dot(h, w_ref[:, 0:aw], preferred_element_type=F32)
    q_ref[...] = head_norm(q, qg_ref[...]).astype(BF16)
    k = jnp.dot(h, w_ref[:, aw:2 * aw], preferred_element_type=F32)
    k_ref[...] = head_norm(k, kg_ref[...]).astype(BF16)
    v = jnp.dot(h, w_ref[:, 2 * aw:3 * aw], preferred_element_type=F32)
    vt_ref[0] = v.T.astype(BF16)
    u_ref[...] = jnp.dot(h, w_ref[:, 3 * aw:4 * aw], preferred_element_type=F32)


def _proj_call(layer, x, ng, sh, sc, w_in_bf, gmat, qg, kg):
    L, d = x.shape
    aw = ATT_WIDTH
    unit = ATT_UNIT_BLOCKS * ATT_T
    per_unit = unit // PROJ_TM
    vec = pl.BlockSpec((1, d), lambda i: (0, 0))
    vec_a = pl.BlockSpec((1, aw), lambda i: (0, 0))
    return pl.pallas_call(
        _proj_kernel,
        grid=(L // PROJ_TM,),
        in_specs=[
            pl.BlockSpec((PROJ_TM, d), lambda i: (i, 0)),
            vec, vec, vec,
            pl.BlockSpec((None, d, IN_WIDTH), lambda i: (layer, 0, 0),
                         pipeline_mode=pl.Buffered(1)),
            pl.BlockSpec((aw, aw), lambda i: (0, 0), pipeline_mode=pl.Buffered(1)),
            vec_a, vec_a,
        ],
        out_specs=[
            pl.BlockSpec((PROJ_TM, aw), lambda i: (i, 0)),
            pl.BlockSpec((PROJ_TM, aw), lambda i: (i, 0)),
            pl.BlockSpec((1, aw, PROJ_TM), lambda i: (i // per_unit, 0, i % per_unit)),
            pl.BlockSpec((PROJ_TM, SSM_WIDTH), lambda i: (i, 0)),
        ],
        out_shape=[
            jax.ShapeDtypeStruct((L, aw), BF16),
            jax.ShapeDtypeStruct((L, aw), BF16),
            jax.ShapeDtypeStruct((L // unit, aw, unit), BF16),
            jax.ShapeDtypeStruct((L, SSM_WIDTH), F32),
        ],
        compiler_params=_cparams(("arbitrary",)),
        name="in_proj",
    )(x, ng, sh, sc, w_in_bf, gmat, qg, kg)


def _attn_kernel(q_ref, k_ref, vt_ref, nbias_ref, cfar_ref, bound_ref, lq1_ref, lk1_ref, lq2_ref,
                 lk2_ref, sg_ref, o_ref, acc_ref, m_ref, l_ref, bias_ref, *, lam_init):
    t = ATT_T
    nb = ATT_UNIT_BLOCKS
    i = pl.program_id(1)

    @pl.when(i == 0)
    def _():
        bound = bound_ref[0]
        far = jnp.broadcast_to(cfar_ref[0] - bound, (t, t))
        masked = jnp.full((t, t), NEG_INF, F32)
        diag = nbias_ref[0, 0] - bound
        prev = nbias_ref[0, 1] - bound
        for b in range(nb):
            rows = pl.ds(b * t, t)
            bias_ref[0, rows, :] = far
            for p in range(nb):
                bias_ref[1 + p, rows, :] = far if b < p - 1 else prev if b == p - 1 else \
                    diag if b == p else masked
            bias_ref[nb + 1, rows, :] = far if b < nb - 1 else prev

    q = q_ref[...]
    lane = lax.broadcasted_iota(jnp.int32, q.shape, 1)
    zero = jnp.zeros_like(q)
    qcat = jnp.concatenate([jnp.where(lane < DIFF_HEAD_DIM, q, zero),
                            jnp.where(lane >= DIFF_HEAD_DIM, q, zero)], axis=0)

    n_units = i // nb + 1
    place = i % nb

    def variant_of(u):
        return jnp.where(u == n_units - 1, 1 + place,
                         jnp.where((u == n_units - 2) & (place == 0), nb + 1, 0))

    def block_scores(u, b):
        off = pl.multiple_of(u * (nb * t) + b * t, t)
        return lax.dot_general(k_ref[pl.ds(off, t), :], qcat, (((1,), (1,)), ((), ())),
                               preferred_element_type=F32)

    acc_ref[...] = jnp.zeros(acc_ref.shape, F32)
    l_ref[...] = jnp.zeros(l_ref.shape, F32)

    def fast_unit(u, carry):
        variant = variant_of(u)
        lsum = jnp.zeros((1, 2 * t), F32)
        pv = None
        s_next = block_scores(u, 0)
        for b in range(nb):
            s = s_next
            if b + 1 < nb:
                s_next = block_scores(u, b + 1)
            bias = bias_ref[variant, pl.ds(b * t, t), :]
            p = jnp.concatenate([jnp.exp2(s[:, :t] + bias), jnp.exp2(s[:, t:] + bias)], axis=1)
            lsum = lsum + jnp.sum(p, axis=0, keepdims=True)
            d = jnp.dot(vt_ref[u, :, b * t:(b + 1) * t], p.astype(BF16),
                        preferred_element_type=F32)
            pv = d if pv is None else pv + d
        acc_ref[...] += pv
        l_ref[...] += lsum
        return carry

    lax.fori_loop(0, n_units, fast_unit, 0)

    l_min = jnp.min(l_ref[...], keepdims=True)
    underflow = jnp.logical_not(l_min[0, 0] > ATT_MIN_DENOM)

    @pl.when(underflow)
    def _():
        m_ref[...] = jnp.full(m_ref.shape, NEG_INF, F32)
        l_ref[...] = jnp.zeros(l_ref.shape, F32)
        acc_ref[...] = jnp.zeros(acc_ref.shape, F32)

        def exact_unit(u, carry):
            variant = variant_of(u)
            for b in range(nb):
                s = block_scores(u, b)
                bias = bias_ref[variant, pl.ds(b * t, t), :]
                s = jnp.concatenate([s[:, :t] + bias, s[:, t:] + bias], axis=1)
                m_old = m_ref[...]
                m_new = jnp.maximum(m_old, jnp.max(s, axis=0, keepdims=True))
                alpha = jnp.exp2(m_old - m_new)
                p = jnp.exp2(s - m_new)
                l_ref[...] = alpha * l_ref[...] + jnp.sum(p, axis=0, keepdims=True)
                acc_ref[...] = alpha * acc_ref[...] + jnp.dot(
                    vt_ref[u, :, b * t:(b + 1) * t], p.astype(BF16), preferred_element_type=F32)
                m_ref[...] = m_new
            return carry

        lax.fori_loop(0, n_units, exact_unit, 0)

    lam = (jnp.exp(jnp.sum(lq1_ref[...] * lk1_ref[...], keepdims=True))
           - jnp.exp(jnp.sum(lq2_ref[...] * lk2_ref[...], keepdims=True)) + lam_init)
    on = acc_ref[...] / l_ref[...]
    o = on[:, :t] - lam * on[:, t:]
    ms = jnp.mean(o * o, axis=0, keepdims=True)
    o = o * lax.rsqrt(ms + SUBLN_EPS) * (sg_ref[...] * (1.0 - lam_init))
    o_ref[...] = o.T.astype(BF16)


def _attn_call(layer_idx, q, k, vt, near_bias, c_far, bound, lq1, lk1, lq2, lk2, sg_col):
    L = q.shape[0]
    t = ATT_T
    unit = ATT_UNIT_BLOCKS * t
    lam_init = 0.8 - 0.6 * math.exp(-0.3 * layer_idx)
    lvec = pl.BlockSpec((1, DIFF_HEAD_DIM), lambda h, i: (0, 0))
    scalar = pl.BlockSpec((1, 1, 1), lambda h, i: (h, 0, 0))
    return pl.pallas_call(
        functools.partial(_attn_kernel, lam_init=lam_init),
        grid=(N_ATT_HEADS, L // t),
        in_specs=[
            pl.BlockSpec((t, V_HEAD_DIM), lambda h, i: (i, h)),
            pl.BlockSpec((L, V_HEAD_DIM), lambda h, i: (0, h)),
            pl.BlockSpec((L // unit, V_HEAD_DIM, unit), lambda h, i: (0, h, 0)),
            pl.BlockSpec((1, 2, t, t), lambda h, i: (h, 0, 0, 0)),
            scalar, scalar,
            lvec, lvec, lvec, lvec,
            pl.BlockSpec((V_HEAD_DIM, 1), lambda h, i: (0, 0)),
        ],
        out_specs=pl.BlockSpec((t, V_HEAD_DIM), lambda h, i: (i, h)),
        out_shape=jax.ShapeDtypeStruct((L, ATT_WIDTH), BF16),
        scratch_shapes=[
            pltpu.VMEM((V_HEAD_DIM, 2 * t), F32),
            pltpu.VMEM((1, 2 * t), F32),
            pltpu.VMEM((1, 2 * t), F32),
            pltpu.VMEM((ATT_UNIT_BLOCKS + 2, unit, t), F32),
        ],
        compiler_params=_cparams(("arbitrary", "arbitrary")),
        name="diff_attn",
    )(q, k, vt, near_bias, c_far, bound, lq1, lk1, lq2, lk2, sg_col)


def _ssm_kernel(u_ref, bm_ref, cm_ref, lam_ref, lamr_ref, d_ref, y_ref, bu_ref, carry_ref):
    ns = SSM_SLAB_GROUPS * SSM_STATE
    nr = SSM_TT // 8
    tt = pl.program_id(1)

    @pl.when(tt == 0)
    def _():
        carry_ref[...] = jnp.zeros_like(carry_ref)

    u = u_ref[...]
    bu_ref[...] = jnp.dot(u.astype(BF16), bm_ref[0], preferred_element_type=F32)
    lr = jnp.broadcast_to(lam_ref[0, 0:1, :], (8, ns))
    li = jnp.broadcast_to(lam_ref[0, 1:2, :], (8, ns))

    def advance(r, sr, si):
        row = pl.multiple_of(r * 8, 8)
        br = bu_ref[pl.ds(row, 8), 0:ns]
        bi = bu_ref[pl.ds(row, 8), ns:2 * ns]
        return row, lr * sr - li * si + br, lr * si + li * sr + bi

    def local_step(r, s):
        _, nsr, nsi = advance(r, s[0], s[1])
        return nsr, nsi

    z = jnp.zeros((8, ns), F32)
    er, ei = lax.fori_loop(0, nr, local_step, (z, z), unroll=4)

    cr = carry_ref[0:1, 0:ns]
    ci = carry_ref[0:1, ns:2 * ns]
    pr = lamr_ref[0, 0:1, :]
    pi = lamr_ref[0, 1:2, :]
    rows = lax.broadcasted_iota(jnp.int32, (8, ns), 0)
    init_r, init_i = z, z
    for c in range(8):
        init_r = jnp.where(rows == c, cr, init_r)
        init_i = jnp.where(rows == c, ci, init_i)
        cr, ci = (pr * cr - pi * ci + er[c:c + 1, :], pr * ci + pi * cr + ei[c:c + 1, :])
    carry_ref[0:1, 0:ns] = cr
    carry_ref[0:1, ns:2 * ns] = ci

    def true_step(r, s):
        row, nsr, nsi = advance(r, s[0], s[1])
        bu_ref[pl.ds(row, 8), 0:ns] = nsr
        bu_ref[pl.ds(row, 8), ns:2 * ns] = nsi
        return nsr, nsi

    lax.fori_loop(0, nr, true_step, (init_r, init_i), unroll=4)

    y = jnp.dot(bu_ref[...].astype(BF16), cm_ref[0], preferred_element_type=F32) + d_ref[0] * u
    y_ref[...] = jax.nn.gelu(y)


def _ssm_call(u_perm, bmat, cmat, lam2, lamr2, d3):
    L = u_perm.shape[0]
    n_slab = N_SSM_GROUPS // SSM_SLAB_GROUPS
    ns = SSM_SLAB_GROUPS * SSM_STATE
    cw = SSM_SLAB_GROUPS * SSM_GROUP
    return pl.pallas_call(
        _ssm_kernel,
        grid=(n_slab, L // SSM_TT),
        in_specs=[
            pl.BlockSpec((SSM_TT, cw), lambda s, t: (t, s)),
            pl.BlockSpec((1, cw, 2 * ns), lambda s, t: (s, 0, 0)),
            pl.BlockSpec((1, 2 * ns, cw), lambda s, t: (s, 0, 0)),
            pl.BlockSpec((1, 2, ns), lambda s, t: (s, 0, 0)),
            pl.BlockSpec((1, 2, ns), lambda s, t: (s, 0, 0)),
            pl.BlockSpec((1, 1, cw), lambda s, t: (s, 0, 0)),
        ],
        out_specs=pl.BlockSpec((SSM_TT, cw), lambda s, t: (t, s)),
        out_shape=jax.ShapeDtypeStruct((L, SSM_WIDTH), F32),
        scratch_shapes=[
            pltpu.VMEM((SSM_TT, 2 * ns), F32),
            pltpu.VMEM((1, 2 * ns), F32),
        ],
        compiler_params=_cparams(("arbitrary", "arbitrary")),
        name="s5_scan",
    )(u_perm, bmat, cmat, lam2, lamr2, d3)


def _ssm_params(lam_re, lam_im, log_step, b_re, b_im, c_re, c_im, d):
    g, p, hc = N_SSM_GROUPS, SSM_STATE, SSM_GROUP
    sg = SSM_SLAB_GROUPS
    n_slab = g // sg
    lam = lax.complex(jnp.minimum(lam_re.astype(F32), -1e-4), lam_im.astype(F32))
    step = jnp.exp(log_step.astype(F32))[:, None]
    lam_bar = jnp.exp(lam * step)
    lam_bar_r = jnp.exp(lam * (step * (SSM_TT // 8)))
    b_bar = ((lam_bar - 1.0) / lam)[:, :, None] * lax.complex(b_re.astype(F32), b_im.astype(F32))
    eye = jnp.eye(sg, dtype=F32)

    def b_block(part):
        z = part.reshape(n_slab, sg, p, hc).transpose(0, 1, 3, 2)
        return (z[:, :, :, None, :] * eye[None, :, None, :, None]).reshape(n_slab, sg * hc, sg * p)

    def c_block(part):
        z = part.reshape(n_slab, sg, hc, p).transpose(0, 1, 3, 2)
        return (z[:, :, :, None, :] * eye[None, :, None, :, None]).reshape(n_slab, sg * p, sg * hc)

    bmat = jnp.concatenate([b_block(jnp.real(b_bar)), b_block(jnp.imag(b_bar))], axis=-1)
    cmat = jnp.concatenate([c_block(c_re.astype(F32)), c_block(-c_im.astype(F32))], axis=1)

    def rows(zc):
        return jnp.stack([jnp.real(zc).reshape(n_slab, sg * p),
                          jnp.imag(zc).reshape(n_slab, sg * p)], axis=1)

    d3 = d.astype(F32).reshape(n_slab, 1, sg * hc)
    return bmat.astype(BF16), cmat.astype(BF16), rows(lam_bar), rows(lam_bar_r), d3


def _out_kernel(att_ref, y_ref, gw_ref, gb_ref, woa_ref, wob_ref, x_ref, gt_ref, o_ref):
    y = y_ref[...]
    z = jnp.dot(y.astype(BF16), gw_ref[...], preferred_element_type=F32) + gb_ref[...]
    yg = (y * jax.nn.sigmoid(z)).astype(BF16)
    m = (jnp.dot(att_ref[...], woa_ref[...], preferred_element_type=F32)
         + jnp.dot(yg, wob_ref[...], preferred_element_type=F32))
    o_ref[...] = x_ref[...] + gt_ref[...] * m


def _out_call(layer, att, y, glu_w_bf, glu_b, w_out_bf, x, gt):
    L, d = x.shape
    aw, sw = ATT_WIDTH, SSM_WIDTH
    one = pl.Buffered(1)
    return pl.pallas_call(
        _out_kernel,
        grid=(L // OUT_TM,),
        in_specs=[
            pl.BlockSpec((OUT_TM, aw), lambda i: (i, 0)),
            pl.BlockSpec((OUT_TM, sw), lambda i: (i, 0)),
            pl.BlockSpec((None, sw, sw), lambda i: (layer, 0, 0), pipeline_mode=one),
            pl.BlockSpec((1, sw), lambda i: (0, 0)),
            pl.BlockSpec((None, aw, d), lambda i: (layer, 0, 0), pipeline_mode=one),
            pl.BlockSpec((None, sw, d), lambda i: (layer, 1, 0), pipeline_mode=one),
            pl.BlockSpec((OUT_TM, d), lambda i: (i, 0)),
            pl.BlockSpec((1, d), lambda i: (0, 0)),
        ],
        out_specs=pl.BlockSpec((OUT_TM, d), lambda i: (i, 0)),
        out_shape=jax.ShapeDtypeStruct((L, d), F32),
        compiler_params=_cparams(("arbitrary",)),
        name="out_proj",
    )(att, y, glu_w_bf, glu_b, w_out_bf, w_out_bf, x, gt)


def _t5_causal_buckets(dist):
    max_exact = N_BUCKETS // 2
    d = jnp.maximum(dist, 1).astype(F32)
    large = max_exact + (jnp.log(d / max_exact) / math.log(MAX_DISTANCE / max_exact)
                         * (N_BUCKETS - max_exact)).astype(jnp.int32)
    large = jnp.minimum(large, N_BUCKETS - 1)
    return jnp.where(dist < max_exact, dist, large)


def _near_bias(rel_bias):
    t = ATT_T
    assert t >= MAX_DISTANCE
    kk = jnp.arange(t, dtype=jnp.int32)[:, None]
    qq = jnp.arange(t, dtype=jnp.int32)[None, :]
    dist = jnp.stack([qq - kk, qq - kk + t], axis=0)
    bucket = _t5_causal_buckets(jnp.maximum(dist, 0))
    rb = rel_bias.astype(F32) * LOG2_E
    val = jnp.zeros((N_ATT_HEADS,) + dist.shape, F32)
    for b in range(N_BUCKETS):
        val = jnp.where((bucket == b)[None], rb[b][:, None, None, None], val)
    return jnp.where((dist >= 0)[None], val, NEG_INF)


def _chunk_interleave(a):
    L, w = a.shape
    return a.reshape(L // SSM_TT, 8, SSM_TT // 8, w).transpose(0, 2, 1, 3).reshape(L, w)


def _chunk_deinterleave(a):
    L, w = a.shape
    return a.reshape(L // SSM_TT, SSM_TT // 8, 8, w).transpose(0, 2, 1, 3).reshape(L, w)


def kernel(x, c, rel_bias, ada_w, ada_b, norm_g, ffn1_w_gate, ffn1_w_up, ffn1_w_down, ffn2_w_gate, ffn2_w_up, ffn2_w_down, w_in, w_out, q_norm_g, k_norm_g, lambda_q1, lambda_k1, lambda_q2, lambda_k2, subln_g, ssm_lambda_re, ssm_lambda_im, ssm_log_step, ssm_b_re, ssm_b_im, ssm_c_re, ssm_c_im, ssm_d, ssm_glu_w, ssm_glu_b):
    b, L, d = x.shape
    assert b == 1 and c.shape == (1, d)
    x2 = x.reshape(L, d)

    mod = _ada_call(c.reshape(d, 1), ada_w, ada_b.reshape(DEPTH, 1, N_COND * d))
    near_bias = _near_bias(rel_bias)
    c_far = (rel_bias.astype(F32)[N_BUCKETS - 1] * LOG2_E).reshape(N_ATT_HEADS, 1, 1)
    w_in_bf = w_in.astype(BF16)
    w_out_bf = w_out.astype(BF16)
    glu_w_bf = ssm_glu_w.astype(BF16)
    hd = DIFF_HEAD_DIM
    group_of = jnp.arange(ATT_WIDTH) // hd
    gmat = jnp.where(group_of[:, None] == group_of[None, :], 1.0 / hd, 0.0).astype(BF16)
    n_rep = ATT_WIDTH // hd

    for i in range(DEPTH):
        m = mod[i]
        sh1, sc1, g1, sh2, sc2, g2, sh3, sc3, g3 = [m[:, n * d:(n + 1) * d] for n in range(N_COND)]
        x2 = _ffn_call(i, x2, norm_g[i, 0][None], sh1, sc1, g1,
                       ffn1_w_gate, ffn1_w_up, ffn1_w_down)

        qg = jnp.tile(q_norm_g[i].astype(F32), n_rep)[None] * (hd ** -0.5 * LOG2_E)
        kg = jnp.tile(k_norm_g[i].astype(F32), n_rep)[None]
        q, k, vt, u = _proj_call(i, x2, norm_g[i, 1][None], sh2, sc2, w_in_bf, gmat, qg, kg)
        bound = (DIFF_HEAD_DIM * jnp.max(jnp.abs(qg[0, :hd] * kg[0, :hd]))
                 + jnp.max(rel_bias.astype(F32), axis=0) * LOG2_E).reshape(N_ATT_HEADS, 1, 1)
        att = _attn_call(i, q, k, vt, near_bias, c_far, bound, lambda_q1[i][None], lambda_k1[i][None],
                         lambda_q2[i][None], lambda_k2[i][None], subln_g[i].reshape(V_HEAD_DIM, 1))
        bmat, cmat, lam2, lamr2, d3 = _ssm_params(
            ssm_lambda_re[i], ssm_lambda_im[i], ssm_log_step[i], ssm_b_re[i], ssm_b_im[i],
            ssm_c_re[i], ssm_c_im[i], ssm_d[i])
        y = _chunk_deinterleave(_ssm_call(_chunk_interleave(u), bmat, cmat, lam2, lamr2, d3))
        x2 = _out_call(i, att, y, glu_w_bf, ssm_glu_b[i][None], w_out_bf, x2, g2)

        x2 = _ffn_call(i, x2, norm_g[i, 2][None], sh3, sc3, g3,
                       ffn2_w_gate, ffn2_w_up, ffn2_w_down)
    return x2.reshape(b, L, d)
```

```python
import functools
import math

import jax
import jax.numpy as jnp
from jax import lax
from jax.experimental import pallas as pl
from jax.experimental.pallas import tpu as pltpu

D_MODEL = 2048
SEQ = 8192
DEPTH = 2
ATT_WIDTH = 1024
SSM_WIDTH = 1024
DIFF_HEAD_DIM = 64
V_HEAD_DIM = 128
N_ATT_HEADS = 8
SSM_GROUP = 16
N_SSM_GROUPS = 64
SSM_STATE = 64
IN_WIDTH = 4096
D_FF = 5632
N_BUCKETS = 32
MAX_DISTANCE = 128
N_COND = 9
NORM_EPS = 1e-6
SUBLN_EPS = 1e-5
NEG_INF = -1e30
LOG2_E = math.log2(math.e)

F32 = jnp.float32
BF16 = jnp.bfloat16

VMEM_LIMIT_BYTES = 60 * 1024 * 1024

ADA_TN = 1024
FFN_TM = 1024
FFN_TF = 512
FFN_NORM_ROWS = 256
PROJ_TM = 512
ATT_T = 256
ATT_TQ = 512
ATT_UNIT_BLOCKS = 4
ATT_MIN_DENOM = 2.0 ** -40
SSM_TT = 1024
SSM_SLAB_GROUPS = 8
OUT_TM = 512


def _cparams(sem):
    return pltpu.CompilerParams(dimension_semantics=sem, vmem_limit_bytes=VMEM_LIMIT_BYTES)


def _ada_kernel(c_ref, w_ref, b_ref, o_ref):
    c = c_ref[...]
    cs = c * jax.nn.sigmoid(c)
    o_ref[0] = jnp.sum(w_ref[0] * cs, axis=0, keepdims=True) + b_ref[0]


def _ada_call(c_col, ada_w, ada_b3):
    depth, d, n = ada_w.shape
    return pl.pallas_call(
        _ada_kernel,
        grid=(depth, n // ADA_TN),
        in_specs=[
            pl.BlockSpec((d, 1), lambda l, j: (0, 0)),
            pl.BlockSpec((1, d, ADA_TN), lambda l, j: (l, 0, j)),
            pl.BlockSpec((1, 1, ADA_TN), lambda l, j: (l, 0, j)),
        ],
        out_specs=pl.BlockSpec((1, 1, ADA_TN), lambda l, j: (l, 0, j)),
        out_shape=jax.ShapeDtypeStruct((depth, 1, n), F32),
        compiler_params=_cparams(("arbitrary", "arbitrary")),
        name="adaln",
    )(c_col, ada_w, ada_b3)


def _norm_mod(x, g, sc, sh):
    ms = jnp.mean(x * x, axis=-1, keepdims=True)
    return (x * lax.rsqrt(ms + NORM_EPS) * g) * (1.0 + sc) + sh


def _ffn_kernel(x_ref, ng_ref, sh_ref, sc_ref, gt_ref, wg_ref, wu_ref, wd_ref, o_ref, h_ref):
    j = pl.program_id(1)

    def weights():
        return (wg_ref[...].astype(BF16), wu_ref[...].astype(BF16), wd_ref[...].astype(BF16))

    def hidden_update(h, wg, wu, wd):
        g = jnp.dot(h, wg, preferred_element_type=F32)
        u = jnp.dot(h, wu, preferred_element_type=F32)
        a = (g * jax.nn.sigmoid(g) * u).astype(BF16)
        return jnp.dot(a, wd, preferred_element_type=F32)

    @pl.when(j == 0)
    def _():
        w = weights()
        for r in range(0, FFN_TM, FFN_NORM_ROWS):
            rows = pl.ds(r, FFN_NORM_ROWS)
            h = _norm_mod(x_ref[rows, :], ng_ref[...], sc_ref[...], sh_ref[...]).astype(BF16)
            h_ref[rows, :] = h
            o_ref[rows, :] = hidden_update(h, *w)

    @pl.when(j > 0)
    def _():
        o_ref[...] += hidden_update(h_ref[...], *weights())

    @pl.when(j == pl.num_programs(1) - 1)
    def _():
        o_ref[...] = x_ref[...] + (0.5 * gt_ref[...]) * o_ref[...]


def _ffn_call(layer, x, ng, sh, sc, gt, wg, wu, wd):
    L, d = x.shape
    dff = wg.shape[-1]
    vec = pl.BlockSpec((1, d), lambda i, j: (0, 0))
    return pl.pallas_call(
        _ffn_kernel,
        grid=(L // FFN_TM, dff // FFN_TF),
        in_specs=[
            pl.BlockSpec((FFN_TM, d), lambda i, j: (i, 0), pipeline_mode=pl.Buffered(1)),
            vec, vec, vec, vec,
            pl.BlockSpec((None, d, FFN_TF), lambda i, j: (layer, 0, j)),
            pl.BlockSpec((None, d, FFN_TF), lambda i, j: (layer, 0, j)),
            pl.BlockSpec((None, FFN_TF, d), lambda i, j: (layer, j, 0)),
        ],
        out_specs=pl.BlockSpec((FFN_TM, d), lambda i, j: (i, 0)),
        out_shape=jax.ShapeDtypeStruct((L, d), F32),
        scratch_shapes=[pltpu.VMEM((FFN_TM, d), BF16)],
        compiler_params=_cparams(("arbitrary", "arbitrary")),
        name="ffn",
    )(x, ng, sh, sc, gt, wg, wu, wd)


def _proj_kernel(x_ref, ng_ref, sh_ref, sc_ref, w_ref, gm_ref, qg_ref, kg_ref,
                 q_ref, k_ref, vt_ref, u_ref):
    h = _norm_mod(x_ref[...], ng_ref[...], sc_ref[...], sh_ref[...]).astype(BF16)
    aw = ATT_WIDTH

    def head_norm(z, g):
        ms = jnp.dot((z * z).astype(BF16), gm_ref[...], preferred_element_type=F32)
        return z * lax.rsqrt(ms + NORM_EPS) * g

    q = jnp.dot(h, w_ref[:, 0:aw], preferred_element_type=F32)
    q_ref[...] = head_norm(q, qg_ref[...]).astype(BF16)
    k = jnp.dot(h, w_ref[:, aw:2 * aw], preferred_element_type=F32)
    k_ref[...] = head_norm(k, kg_ref[...]).astype(BF16)
    v = jnp.dot(h, w_ref[:, 2 * aw:3 * aw], preferred_element_type=F32)
    vt_ref[0] = v.T.astype(BF16)
    u_ref[...] = jnp.dot(h, w_ref[:, 3 * aw:4 * aw], preferred_element_type=F32)


def _proj_call(layer, x, ng, sh, sc, w_in_bf, gmat, qg, kg):
    L, d = x.shape
    aw = ATT_WIDTH
    unit = ATT_UNIT_BLOCKS * ATT_T
    per_unit = unit // PROJ_TM
    vec = pl.BlockSpec((1, d), lambda i: (0, 0))
    vec_a = pl.BlockSpec((1, aw), lambda i: (0, 0))
    return pl.pallas_call(
        _proj_kernel,
        grid=(L // PROJ_TM,),
        in_specs=[
            pl.BlockSpec((PROJ_TM, d), lambda i: (i, 0)),
            vec, vec, vec,
            pl.BlockSpec((None, d, IN_WIDTH), lambda i: (layer, 0, 0),
                         pipeline_mode=pl.Buffered(1)),
            pl.BlockSpec((aw, aw), lambda i: (0, 0), pipeline_mode=pl.Buffered(1)),
            vec_a, vec_a,
        ],
        out_specs=[
            pl.BlockSpec((PROJ_TM, aw), lambda i: (i, 0)),
            pl.BlockSpec((PROJ_TM, aw), lambda i: (i, 0)),
            pl.BlockSpec((1, aw, PROJ_TM), lambda i: (i // per_unit, 0, i % per_unit)),
            pl.BlockSpec((PROJ_TM, SSM_WIDTH), lambda i: (i, 0)),
        ],
        out_shape=[
            jax.ShapeDtypeStruct((L, aw), BF16),
            jax.ShapeDtypeStruct((L, aw), BF16),
            jax.ShapeDtypeStruct((L // unit, aw, unit), BF16),
            jax.ShapeDtypeStruct((L, SSM_WIDTH), F32),
        ],
        compiler_params=_cparams(("arbitrary",)),
        name="in_proj",
    )(x, ng, sh, sc, w_in_bf, gmat, qg, kg)


def _attn_kernel(q_ref, k_ref, vt_ref, nbias_ref, cfar_ref, bound_ref, lq1_ref, lk1_ref, lq2_ref,
                 lk2_ref, sg_ref, o_ref, acc_ref, m_ref, l_ref, bias_ref, s_ref, p_ref, *, lam_init):
    t = ATT_T
    tq = ATT_TQ
    qb = tq // t
    nb = ATT_UNIT_BLOCKS
    qi = pl.program_id(1)
    n_blk = k_ref.shape[0] // t

    @pl.when(qi == 0)
    def _():
        bound = bound_ref[0]
        tiles = {2: jnp.broadcast_to(cfar_ref[0] - bound, (t, t)), 1: nbias_ref[0, 1] - bound,
                 0: nbias_ref[0, 0] - bound, -1: jnp.full((t, t), NEG_INF, F32)}
        for d in range(-2, 3):
            for a in range(qb):
                bias_ref[d + 2, :, a * t:(a + 1) * t] = tiles[max(min(d + a, 2), -1)]

    q = q_ref[...]
    lane = lax.broadcasted_iota(jnp.int32, q.shape, 1)
    zero = jnp.zeros_like(q)
    qcat = jnp.concatenate([jnp.where(lane < DIFF_HEAD_DIM, q, zero),
                            jnp.where(lane >= DIFF_HEAD_DIM, q, zero)], axis=0)

    n_units = (qb * qi + qb + nb - 1) // nb

    def scores(blk):
        off = pl.multiple_of(blk * t, t)
        return lax.dot_general(k_ref[pl.ds(off, t), :], qcat, (((1,), (1,)), ((), ())),
                               preferred_element_type=F32)

    def biased(s, blk):
        bias = bias_ref[jnp.clip(qb * qi - blk, -2, 2) + 2]
        return s[:, :tq] + bias, s[:, tq:] + bias

    acc_ref[...] = jnp.zeros(acc_ref.shape, F32)
    l_ref[...] = jnp.zeros(l_ref.shape, F32)
    s_ref[...] = scores(0)
    p_ref[...] = jnp.zeros(p_ref.shape, BF16)

    def fast_unit(u, carry):
        s_cur = s_ref[...]
        pv = jnp.dot(vt_ref[jnp.maximum(u - 1, 0), :, (nb - 1) * t:nb * t], p_ref[...],
                     preferred_element_type=F32)
        lsum = jnp.zeros((1, 2 * tq), F32)
        for b in range(nb):
            blk = u * nb + b
            s_next = scores(jnp.minimum(blk + 1, n_blk - 1))
            s0, s1 = biased(s_cur, blk)
            p = jnp.concatenate([jnp.exp2(s0), jnp.exp2(s1)], axis=1)
            lsum = lsum + jnp.sum(p, axis=0, keepdims=True)
            if b < nb - 1:
                pv = pv + jnp.dot(vt_ref[u, :, b * t:(b + 1) * t], p.astype(BF16),
                                  preferred_element_type=F32)
            else:
                p_ref[...] = p.astype(BF16)
            s_cur = s_next
        s_ref[...] = s_cur
        acc_ref[...] += pv
        l_ref[...] += lsum
        return carry

    lax.fori_loop(0, n_units, fast_unit, 0)
    acc_ref[...] += jnp.dot(vt_ref[n_units - 1, :, (nb - 1) * t:nb * t], p_ref[...],
                            preferred_element_type=F32)

    l_min = jnp.min(l_ref[...], keepdims=True)
    underflow = jnp.logical_not(l_min[0, 0] > ATT_MIN_DENOM)

    @pl.when(underflow)
    def _():
        m_ref[...] = jnp.full(m_ref.shape, NEG_INF, F32)
        l_ref[...] = jnp.zeros(l_ref.shape, F32)
        acc_ref[...] = jnp.zeros(acc_ref.shape, F32)

        def exact_unit(u, carry):
            for b in range(nb):
                blk = u * nb + b
                s = jnp.concatenate(biased(scores(blk), blk), axis=1)
                m_old = m_ref[...]
                m_new = jnp.maximum(m_old, jnp.max(s, axis=0, keepdims=True))
                alpha = jnp.exp2(m_old - m_new)
                p = jnp.exp2(s - m_new)
                l_ref[...] = alpha * l_ref[...] + jnp.sum(p, axis=0, keepdims=True)
                acc_ref[...] = alpha * acc_ref[...] + jnp.dot(
                    vt_ref[u, :, b * t:(b + 1) * t], p.astype(BF16), preferred_element_type=F32)
                m_ref[...] = m_new
            return carry

        lax.fori_loop(0, n_units, exact_unit, 0)

    lam = (jnp.exp(jnp.sum(lq1_ref[...] * lk1_ref[...], keepdims=True))
           - jnp.exp(jnp.sum(lq2_ref[...] * lk2_ref[...], keepdims=True)) + lam_init)
    on = acc_ref[...] / l_ref[...]
    o = on[:, :tq] - lam * on[:, tq:]
    ms = jnp.mean(o * o, axis=0, keepdims=True)
    o = o * lax.rsqrt(ms + SUBLN_EPS) * (sg_ref[...] * (1.0 - lam_init))
    o_ref[...] = o.T.astype(BF16)


def _attn_call(layer_idx, q, k, vt, near_bias, c_far, bound, lq1, lk1, lq2, lk2, sg_col):
    L = q.shape[0]
    t, tq = ATT_T, ATT_TQ
    unit = ATT_UNIT_BLOCKS * t
    assert tq % t == 0 and tq // t <= 2 and L % unit == 0 and L % tq == 0
    lam_init = 0.8 - 0.6 * math.exp(-0.3 * layer_idx)
    lvec = pl.BlockSpec((1, DIFF_HEAD_DIM), lambda h, i: (0, 0))
    scalar = pl.BlockSpec((1, 1, 1), lambda h, i: (h, 0, 0))
    return pl.pallas_call(
        functools.partial(_attn_kernel, lam_init=lam_init),
        grid=(N_ATT_HEADS, L // tq),
        in_specs=[
            pl.BlockSpec((tq, V_HEAD_DIM), lambda h, i: (i, h)),
            pl.BlockSpec((L, V_HEAD_DIM), lambda h, i: (0, h)),
            pl.BlockSpec((L // unit, V_HEAD_DIM, unit), lambda h, i: (0, h, 0)),
            pl.BlockSpec((1, 2, t, t), lambda h, i: (h, 0, 0, 0)),
            scalar, scalar,
            lvec, lvec, lvec, lvec,
            pl.BlockSpec((V_HEAD_DIM, 1), lambda h, i: (0, 0)),
        ],
        out_specs=pl.BlockSpec((tq, V_HEAD_DIM), lambda h, i: (i, h)),
        out_shape=jax.ShapeDtypeStruct((L, ATT_WIDTH), BF16),
        scratch_shapes=[
            pltpu.VMEM((V_HEAD_DIM, 2 * tq), F32),
            pltpu.VMEM((1, 2 * tq), F32),
            pltpu.VMEM((1, 2 * tq), F32),
            pltpu.VMEM((5, t, tq), F32),
            pltpu.VMEM((t, 2 * tq), F32),
            pltpu.VMEM((t, 2 * tq), BF16),
        ],
        compiler_params=_cparams(("arbitrary", "arbitrary")),
        name="diff_attn",
    )(q, k, vt, near_bias, c_far, bound, lq1, lk1, lq2, lk2, sg_col)


def _ssm_kernel(u_ref, bm_ref, cm_ref, lam_ref, lamr_ref, d_ref, y_ref, bu_ref, carry_ref):
    ns = SSM_SLAB_GROUPS * SSM_STATE
    nr = SSM_TT // 8
    tt = pl.program_id(1)

    @pl.when(tt == 0)
    def _():
        carry_ref[...] = jnp.zeros_like(carry_ref)

    u = u_ref[...]
    bu_ref[...] = jnp.dot(u.astype(BF16), bm_ref[0], preferred_element_type=F32)
    lr = jnp.broadcast_to(lam_ref[0, 0:1, :], (8, ns))
    li = jnp.broadcast_to(lam_ref[0, 1:2, :], (8, ns))

    def advance(r, sr, si):
        row = pl.multiple_of(r * 8, 8)
        br = bu_ref[pl.ds(row, 8), 0:ns]
        bi = bu_ref[pl.ds(row, 8), ns:2 * ns]
        return row, lr * sr - li * si + br, lr * si + li * sr + bi

    def local_step(r, s):
        _, nsr, nsi = advance(r, s[0], s[1])
        return nsr, nsi

    z = jnp.zeros((8, ns), F32)
    er, ei = lax.fori_loop(0, nr, local_step, (z, z), unroll=4)

    cr = carry_ref[0:1, 0:ns]
    ci = carry_ref[0:1, ns:2 * ns]
    pr = lamr_ref[0, 0:1, :]
    pi = lamr_ref[0, 1:2, :]
    rows = lax.broadcasted_iota(jnp.int32, (8, ns), 0)
    init_r, init_i = z, z
    for c in range(8):
        init_r = jnp.where(rows == c, cr, init_r)
        init_i = jnp.where(rows == c, ci, init_i)
        cr, ci = (pr * cr - pi * ci + er[c:c + 1, :], pr * ci + pi * cr + ei[c:c + 1, :])
    carry_ref[0:1, 0:ns] = cr
    carry_ref[0:1, ns:2 * ns] = ci

    def true_step(r, s):
        row, nsr, nsi = advance(r, s[0], s[1])
        bu_ref[pl.ds(row, 8), 0:ns] = nsr
        bu_ref[pl.ds(row, 8), ns:2 * ns] = nsi
        return nsr, nsi

    lax.fori_loop(0, nr, true_step, (init_r, init_i), unroll=4)

    y = jnp.dot(bu_ref[...].astype(BF16), cm_ref[0], preferred_element_type=F32) + d_ref[0] * u
    y_ref[...] = jax.nn.gelu(y)


def _ssm_call(u_perm, bmat, cmat, lam2, lamr2, d3):
    L = u_perm.shape[0]
    n_slab = N_SSM_GROUPS // SSM_SLAB_GROUPS
    ns = SSM_SLAB_GROUPS * SSM_STATE
    cw = SSM_SLAB_GROUPS * SSM_GROUP
    return pl.pallas_call(
        _ssm_kernel,
        grid=(n_slab, L // SSM_TT),
        in_specs=[
            pl.BlockSpec((SSM_TT, cw), lambda s, t: (t, s)),
            pl.BlockSpec((1, cw, 2 * ns), lambda s, t: (s, 0, 0)),
            pl.BlockSpec((1, 2 * ns, cw), lambda s, t: (s, 0, 0)),
            pl.BlockSpec((1, 2, ns), lambda s, t: (s, 0, 0)),
            pl.BlockSpec((1, 2, ns), lambda s, t: (s, 0, 0)),
            pl.BlockSpec((1, 1, cw), lambda s, t: (s, 0, 0)),
        ],
        out_specs=pl.BlockSpec((SSM_TT, cw), lambda s, t: (t, s)),
        out_shape=jax.ShapeDtypeStruct((L, SSM_WIDTH), F32),
        scratch_shapes=[
            pltpu.VMEM((SSM_TT, 2 * ns), F32),
            pltpu.VMEM((1, 2 * ns), F32),
        ],
        compiler_params=_cparams(("arbitrary", "arbitrary")),
        name="s5_scan",
    )(u_perm, bmat, cmat, lam2, lamr2, d3)


def _ssm_params(lam_re, lam_im, log_step, b_re, b_im, c_re, c_im, d):
    g, p, hc = N_SSM_GROUPS, SSM_STATE, SSM_GROUP
    sg = SSM_SLAB_GROUPS
    n_slab = g // sg
    lam = lax.complex(jnp.minimum(lam_re.astype(F32), -1e-4), lam_im.astype(F32))
    step = jnp.exp(log_step.astype(F32))[:, None]
    lam_bar = jnp.exp(lam * step)
    lam_bar_r = jnp.exp(lam * (step * (SSM_TT // 8)))
    b_bar = ((lam_bar - 1.0) / lam)[:, :, None] * lax.complex(b_re.astype(F32), b_im.astype(F32))
    eye = jnp.eye(sg, dtype=F32)

    def b_block(part):
        z = part.reshape(n_slab, sg, p, hc).transpose(0, 1, 3, 2)
        return (z[:, :, :, None, :] * eye[None, :, None, :, None]).reshape(n_slab, sg * hc, sg * p)

    def c_block(part):
        z = part.reshape(n_slab, sg, hc, p).transpose(0, 1, 3, 2)
        return (z[:, :, :, None, :] * eye[None, :, None, :, None]).reshape(n_slab, sg * p, sg * hc)

    bmat = jnp.concatenate([b_block(jnp.real(b_bar)), b_block(jnp.imag(b_bar))], axis=-1)
    cmat = jnp.concatenate([c_block(c_re.astype(F32)), c_block(-c_im.astype(F32))], axis=1)

    def rows(zc):
        return jnp.stack([jnp.real(zc).reshape(n_slab, sg * p),
                          jnp.imag(zc).reshape(n_slab, sg * p)], axis=1)

    d3 = d.astype(F32).reshape(n_slab, 1, sg * hc)
    return bmat.astype(BF16), cmat.astype(BF16), rows(lam_bar), rows(lam_bar_r), d3


def _out_kernel(att_ref, y_ref, gw_ref, gb_ref, woa_ref, wob_ref, x_ref, gt_ref, o_ref):
    y = y_ref[...]
    z = jnp.dot(y.astype(BF16), gw_ref[...], preferred_element_type=F32) + gb_ref[...]
    yg = (y * jax.nn.sigmoid(z)).astype(BF16)
    m = (jnp.dot(att_ref[...], woa_ref[...], preferred_element_type=F32)
         + jnp.dot(yg, wob_ref[...], preferred_element_type=F32))
    o_ref[...] = x_ref[...] + gt_ref[...] * m


def _out_call(layer, att, y, glu_w_bf, glu_b, w_out_bf, x, gt):
    L, d = x.shape
    aw, sw = ATT_WIDTH, SSM_WIDTH
    one = pl.Buffered(1)
    return pl.pallas_call(
        _out_kernel,
        grid=(L // OUT_TM,),
        in_specs=[
            pl.BlockSpec((OUT_TM, aw), lambda i: (i, 0)),
            pl.BlockSpec((OUT_TM, sw), lambda i: (i, 0)),
            pl.BlockSpec((None, sw, sw), lambda i: (layer, 0, 0), pipeline_mode=one),
            pl.BlockSpec((1, sw), lambda i: (0, 0)),
            pl.BlockSpec((None, aw, d), lambda i: (layer, 0, 0), pipeline_mode=one),
            pl.BlockSpec((None, sw, d), lambda i: (layer, 1, 0), pipeline_mode=one),
            pl.BlockSpec((OUT_TM, d), lambda i: (i, 0)),
            pl.BlockSpec((1, d), lambda i: (0, 0)),
        ],
        out_specs=pl.BlockSpec((OUT_TM, d), lambda i: (i, 0)),
        out_shape=jax.ShapeDtypeStruct((L, d), F32),
        compiler_params=_cparams(("arbitrary",)),
        name="out_proj",
    )(att, y, glu_w_bf, glu_b, w_out_bf, w_out_bf, x, gt)


def _t5_causal_buckets(dist):
    max_exact = N_BUCKETS // 2
    d = jnp.maximum(dist, 1).astype(F32)
    large = max_exact + (jnp.log(d / max_exact) / math.log(MAX_DISTANCE / max_exact)
                         * (N_BUCKETS - max_exact)).astype(jnp.int32)
    large = jnp.minimum(large, N_BUCKETS - 1)
    return jnp.where(dist < max_exact, dist, large)


def _near_bias(rel_bias):
    t = ATT_T
    assert t >= MAX_DISTANCE
    kk = jnp.arange(t, dtype=jnp.int32)[:, None]
    qq = jnp.arange(t, dtype=jnp.int32)[None, :]
    dist = jnp.stack([qq - kk, qq - kk + t], axis=0)
    bucket = _t5_causal_buckets(jnp.maximum(dist, 0))
    rb = rel_bias.astype(F32) * LOG2_E
    val = jnp.zeros((N_ATT_HEADS,) + dist.shape, F32)
    for b in range(N_BUCKETS):
        val = jnp.where((bucket == b)[None], rb[b][:, None, None, None], val)
    return jnp.where((dist >= 0)[None], val, NEG_INF)


def _chunk_interleave(a):
    L, w = a.shape
    return a.reshape(L // SSM_TT, 8, SSM_TT // 8, w).transpose(0, 2, 1, 3).reshape(L, w)


def _chunk_deinterleave(a):
    L, w = a.shape
    return a.reshape(L // SSM_TT, SSM_TT // 8, 8, w).transpose(0, 2, 1, 3).reshape(L, w)


def kernel(x, c, rel_bias, ada_w, ada_b, norm_g, ffn1_w_gate, ffn1_w_up, ffn1_w_down, ffn2_w_gate, ffn2_w_up, ffn2_w_down, w_in, w_out, q_norm_g, k_norm_g, lambda_q1, lambda_k1, lambda_q2, lambda_k2, subln_g, ssm_lambda_re, ssm_lambda_im, ssm_log_step, ssm_b_re, ssm_b_im, ssm_c_re, ssm_c_im, ssm_d, ssm_glu_w, ssm_glu_b):
    b, L, d = x.shape
    assert b == 1 and c.shape == (1, d)
    x2 = x.reshape(L, d)

    mod = _ada_call(c.reshape(d, 1), ada_w, ada_b.reshape(DEPTH, 1, N_COND * d))
    near_bias = _near_bias(rel_bias)
    c_far = (rel_bias.astype(F32)[N_BUCKETS - 1] * LOG2_E).reshape(N_ATT_HEADS, 1, 1)
    w_in_bf = w_in.astype(BF16)
    w_out_bf = w_out.astype(BF16)
    glu_w_bf = ssm_glu_w.astype(BF16)
    hd = DIFF_HEAD_DIM
    group_of = jnp.arange(ATT_WIDTH) // hd
    gmat = jnp.where(group_of[:, None] == group_of[None, :], 1.0 / hd, 0.0).astype(BF16)
    n_rep = ATT_WIDTH // hd

    for i in range(DEPTH):
        m = mod[i]
        sh1, sc1, g1, sh2, sc2, g2, sh3, sc3, g3 = [m[:, n * d:(n + 1) * d] for n in range(N_COND)]
        x2 = _ffn_call(i, x2, norm_g[i, 0][None], sh1, sc1, g1,
                       ffn1_w_gate, ffn1_w_up, ffn1_w_down)

        qg = jnp.tile(q_norm_g[i].astype(F32), n_rep)[None] * (hd ** -0.5 * LOG2_E)
        kg = jnp.tile(k_norm_g[i].astype(F32), n_rep)[None]
        q, k, vt, u = _proj_call(i, x2, norm_g[i, 1][None], sh2, sc2, w_in_bf, gmat, qg, kg)
        bound = (DIFF_HEAD_DIM * jnp.max(jnp.abs(qg[0, :hd] * kg[0, :hd]))
                 + jnp.max(rel_bias.astype(F32), axis=0) * LOG2_E).reshape(N_ATT_HEADS, 1, 1)
        att = _attn_call(i, q, k, vt, near_bias, c_far, bound, lambda_q1[i][None], lambda_k1[i][None],
                         lambda_q2[i][None], lambda_k2[i][None], subln_g[i].reshape(V_HEAD_DIM, 1))
        bmat, cmat, lam2, lamr2, d3 = _ssm_params(
            ssm_lambda_re[i], ssm_lambda_im[i], ssm_log_step[i], ssm_b_re[i], ssm_b_im[i],
            ssm_c_re[i], ssm_c_im[i], ssm_d[i])
        y = _chunk_deinterleave(_ssm_call(_chunk_interleave(u), bmat, cmat, lam2, lamr2, d3))
        x2 = _out_call(i, att, y, glu_w_bf, ssm_glu_b[i][None], w_out_bf, x2, g2)

        x2 = _ffn_call(i, x2, norm_g[i, 2][None], sh3, sc3, g3,
                       ffn2_w_gate, ffn2_w_up, ffn2_w_down)
    return x2.reshape(b, L, d)
```

```python
import functools
import math

import jax
import jax.numpy as jnp
from jax import lax
from jax.experimental import pallas as pl
from jax.experimental.pallas import tpu as pltpu

D_MODEL = 2048
SEQ = 8192
DEPTH = 2
ATT_WIDTH = 1024
SSM_WIDTH = 1024
DIFF_HEAD_DIM = 64
V_HEAD_DIM = 128
N_ATT_HEADS = 8
SSM_GROUP = 16
N_SSM_GROUPS = 64
SSM_STATE = 64
IN_WIDTH = 4096
D_FF = 5632
N_BUCKETS = 32
MAX_DISTANCE = 128
N_COND = 9
NORM_EPS = 1e-6
SUBLN_EPS = 1e-5
NEG_INF = -1e30
LOG2_E = math.log2(math.e)

F32 = jnp.float32
BF16 = jnp.bfloat16

VMEM_LIMIT_BYTES = 60 * 1024 * 1024

ADA_TN = 1024
FFN_TM = 1024
FFN_TF = 512
FFN_NORM_ROWS = 256
PROJ_TM = 512
ATT_T = 256
ATT_TQ = 1024
ATT_UNIT_BLOCKS = 4
ATT_MIN_DENOM = 2.0 ** -40
SSM_TT = 1024
SSM_SLAB_GROUPS = 8
SSM_STRIP_ROWS = 128
OUT_TM = 512


def _cparams(sem):
    return pltpu.CompilerParams(dimension_semantics=sem, vmem_limit_bytes=VMEM_LIMIT_BYTES)


def _ada_kernel(c_ref, w_ref, b_ref, o_ref):
    c = c_ref[...]
    cs = c * jax.nn.sigmoid(c)
    o_ref[0] = jnp.sum(w_ref[0] * cs, axis=0, keepdims=True) + b_ref[0]


def _ada_call(c_col, ada_w, ada_b3):
    depth, d, n = ada_w.shape
    return pl.pallas_call(
        _ada_kernel,
        grid=(depth, n // ADA_TN),
        in_specs=[
            pl.BlockSpec((d, 1), lambda l, j: (0, 0)),
            pl.BlockSpec((1, d, ADA_TN), lambda l, j: (l, 0, j)),
            pl.BlockSpec((1, 1, ADA_TN), lambda l, j: (l, 0, j)),
        ],
        out_specs=pl.BlockSpec((1, 1, ADA_TN), lambda l, j: (l, 0, j)),
        out_shape=jax.ShapeDtypeStruct((depth, 1, n), F32),
        compiler_params=_cparams(("arbitrary", "arbitrary")),
        name="adaln",
    )(c_col, ada_w, ada_b3)


def _norm_mod(x, g, sc, sh):
    ms = jnp.mean(x * x, axis=-1, keepdims=True)
    return (x * lax.rsqrt(ms + NORM_EPS) * g) * (1.0 + sc) + sh


def _ffn_kernel(x_ref, ng_ref, sh_ref, sc_ref, gt_ref, wg_ref, wu_ref, wd_ref, o_ref, h_ref):
    j = pl.program_id(1)

    def weights():
        return (wg_ref[...].astype(BF16), wu_ref[...].astype(BF16), wd_ref[...].astype(BF16))

    def hidden_update(h, wg, wu, wd):
        g = jnp.dot(h, wg, preferred_element_type=F32)
        u = jnp.dot(h, wu, preferred_element_type=F32)
        a = (g * jax.nn.sigmoid(g) * u).astype(BF16)
        return jnp.dot(a, wd, preferred_element_type=F32)

    @pl.when(j == 0)
    def _():
        w = weights()
        for r in range(0, FFN_TM, FFN_NORM_ROWS):
            rows = pl.ds(r, FFN_NORM_ROWS)
            h = _norm_mod(x_ref[rows, :], ng_ref[...], sc_ref[...], sh_ref[...]).astype(BF16)
            h_ref[rows, :] = h
            o_ref[rows, :] = hidden_update(h, *w)

    @pl.when(j > 0)
    def _():
        o_ref[...] += hidden_update(h_ref[...], *weights())

    @pl.when(j == pl.num_programs(1) - 1)
    def _():
        o_ref[...] = x_ref[...] + (0.5 * gt_ref[...]) * o_ref[...]


def _ffn_call(layer, x, ng, sh, sc, gt, wg, wu, wd):
    L, d = x.shape
    dff = wg.shape[-1]
    vec = pl.BlockSpec((1, d), lambda i, j: (0, 0))
    return pl.pallas_call(
        _ffn_kernel,
        grid=(L // FFN_TM, dff // FFN_TF),
        in_specs=[
            pl.BlockSpec((FFN_TM, d), lambda i, j: (i, 0), pipeline_mode=pl.Buffered(1)),
            vec, vec, vec, vec,
            pl.BlockSpec((None, d, FFN_TF), lambda i, j: (layer, 0, j)),
            pl.BlockSpec((None, d, FFN_TF), lambda i, j: (layer, 0, j)),
            pl.BlockSpec((None, FFN_TF, d), lambda i, j: (layer, j, 0)),
        ],
        out_specs=pl.BlockSpec((FFN_TM, d), lambda i, j: (i, 0)),
        out_shape=jax.ShapeDtypeStruct((L, d), F32),
        scratch_shapes=[pltpu.VMEM((FFN_TM, d), BF16)],
        compiler_params=_cparams(("arbitrary", "arbitrary")),
        name="ffn",
    )(x, ng, sh, sc, gt, wg, wu, wd)


def _proj_kernel(x_ref, ng_ref, sh_ref, sc_ref, w_ref, gm_ref, qg_ref, kg_ref,
                 q_ref, k_ref, vt_ref, u_ref):
    h = _norm_mod(x_ref[...], ng_ref[...], sc_ref[...], sh_ref[...]).astype(BF16)
    aw = ATT_WIDTH

    def head_norm(z, g):
        ms = jnp.dot((z * z).astype(BF16), gm_ref[...], preferred_element_type=F32)
        return z * lax.rsqrt(ms + NORM_EPS) * g

    q = jnp.dot(h, w_ref[:, 0:aw], preferred_element_type=F32)
    q_ref[...] = head_norm(q, qg_ref[...]).astype(BF16)
    k = jnp.dot(h, w_ref[:, aw:2 * aw], preferred_element_type=F32)
    k_ref[...] = head_norm(k, kg_ref[...]).astype(BF16)
    v = jnp.dot(h, w_ref[:, 2 * aw:3 * aw], preferred_element_type=F32)
    vt_ref[0] = v.T.astype(BF16)
    u_ref[...] = jnp.dot(h, w_ref[:, 3 * aw:4 * aw], preferred_element_type=F32)


def _proj_call(layer, x, ng, sh, sc, w_in_bf, gmat, qg, kg):
    L, d = x.shape
    aw = ATT_WIDTH
    unit = ATT_UNIT_BLOCKS * ATT_T
    per_unit = unit // PROJ_TM
    vec = pl.BlockSpec((1, d), lambda i: (0, 0))
    vec_a = pl.BlockSpec((1, aw), lambda i: (0, 0))
    return pl.pallas_call(
        _proj_kernel,
        grid=(L // PROJ_TM,),
        in_specs=[
            pl.BlockSpec((PROJ_TM, d), lambda i: (i, 0)),
            vec, vec, vec,
            pl.BlockSpec((None, d, IN_WIDTH), lambda i: (layer, 0, 0),
                         pipeline_mode=pl.Buffered(1)),
            pl.BlockSpec((aw, aw), lambda i: (0, 0), pipeline_mode=pl.Buffered(1)),
            vec_a, vec_a,
        ],
        out_specs=[
            pl.BlockSpec((PROJ_TM, aw), lambda i: (i, 0)),
            pl.BlockSpec((PROJ_TM, aw), lambda i: (i, 0)),
            pl.BlockSpec((1, aw, PROJ_TM), lambda i: (i // per_unit, 0, i % per_unit)),
            pl.BlockSpec((PROJ_TM, SSM_WIDTH), lambda i: (i, 0)),
        ],
        out_shape=[
            jax.ShapeDtypeStruct((L, aw), BF16),
            jax.ShapeDtypeStruct((L, aw), BF16),
            jax.ShapeDtypeStruct((L // unit, aw, unit), BF16),
            jax.ShapeDtypeStruct((L, SSM_WIDTH), F32),
        ],
        compiler_params=_cparams(("arbitrary",)),
        name="in_proj",
    )(x, ng, sh, sc, w_in_bf, gmat, qg, kg)


def _attn_kernel(q_ref, k_ref, vt_ref, nbias_ref, cfar_ref, bound_ref, lq1_ref, lk1_ref, lq2_ref,
                 lk2_ref, sg_ref, o_ref, acc_ref, m_ref, l_ref, bias_ref, s_ref, p_ref, *, lam_init):
    t = ATT_T
    tq = q_ref.shape[0]
    qb = tq // t
    nb = ATT_UNIT_BLOCKS
    qi = pl.program_id(1)
    n_blk = k_ref.shape[0] // t

    @pl.when(qi == 0)
    def _():
        bound = bound_ref[0]
        tiles = {2: jnp.broadcast_to(cfar_ref[0] - bound, (t, t)), 1: nbias_ref[0, 1] - bound,
                 0: nbias_ref[0, 0] - bound, -1: jnp.full((t, t), NEG_INF, F32)}
        for d in range(-qb, 3):
            for a in range(qb):
                bias_ref[d + qb, :, a * t:(a + 1) * t] = tiles[max(min(d + a, 2), -1)]

    q = q_ref[...]
    lane = lax.broadcasted_iota(jnp.int32, q.shape, 1)
    zero = jnp.zeros_like(q)
    qcat = jnp.concatenate([jnp.where(lane < DIFF_HEAD_DIM, q, zero),
                            jnp.where(lane >= DIFF_HEAD_DIM, q, zero)], axis=0)

    n_units = (qb * qi + qb + nb - 1) // nb

    def scores(blk):
        off = pl.multiple_of(blk * t, t)
        return lax.dot_general(k_ref[pl.ds(off, t), :], qcat, (((1,), (1,)), ((), ())),
                               preferred_element_type=F32)

    def biased(s, blk):
        bias = bias_ref[jnp.clip(qb * qi - blk, -qb, 2) + qb]
        return s[:, :tq] + bias, s[:, tq:] + bias

    acc_ref[...] = jnp.zeros(acc_ref.shape, F32)
    l_ref[...] = jnp.zeros(l_ref.shape, F32)
    s_ref[...] = scores(0)
    p_ref[...] = jnp.zeros(p_ref.shape, BF16)

    def fast_unit(u, carry):
        s_cur = s_ref[...]
        pv = jnp.dot(vt_ref[jnp.maximum(u - 1, 0), :, (nb - 1) * t:nb * t], p_ref[...],
                     preferred_element_type=F32)
        lsum = jnp.zeros((1, 2 * tq), F32)
        for b in range(nb):
            blk = u * nb + b
            s_next = scores(jnp.minimum(blk + 1, n_blk - 1))
            s0, s1 = biased(s_cur, blk)
            p = jnp.concatenate([jnp.exp2(s0), jnp.exp2(s1)], axis=1)
            lsum = lsum + jnp.sum(p, axis=0, keepdims=True)
            if b < nb - 1:
                pv = pv + jnp.dot(vt_ref[u, :, b * t:(b + 1) * t], p.astype(BF16),
                                  preferred_element_type=F32)
            else:
                p_ref[...] = p.astype(BF16)
            s_cur = s_next
        s_ref[...] = s_cur
        acc_ref[...] += pv
        l_ref[...] += lsum
        return carry

    lax.fori_loop(0, n_units, fast_unit, 0)
    acc_ref[...] += jnp.dot(vt_ref[n_units - 1, :, (nb - 1) * t:nb * t], p_ref[...],
                            preferred_element_type=F32)

    l_min = jnp.min(l_ref[...], keepdims=True)
    underflow = jnp.logical_not(l_min[0, 0] > ATT_MIN_DENOM)

    @pl.when(underflow)
    def _():
        m_ref[...] = jnp.full(m_ref.shape, NEG_INF, F32)
        l_ref[...] = jnp.zeros(l_ref.shape, F32)
        acc_ref[...] = jnp.zeros(acc_ref.shape, F32)

        def exact_unit(u, carry):
            for b in range(nb):
                blk = u * nb + b
                s = jnp.concatenate(biased(scores(blk), blk), axis=1)
                m_old = m_ref[...]
                m_new = jnp.maximum(m_old, jnp.max(s, axis=0, keepdims=True))
                alpha = jnp.exp2(m_old - m_new)
                p = jnp.exp2(s - m_new)
                l_ref[...] = alpha * l_ref[...] + jnp.sum(p, axis=0, keepdims=True)
                acc_ref[...] = alpha * acc_ref[...] + jnp.dot(
                    vt_ref[u, :, b * t:(b + 1) * t], p.astype(BF16), preferred_element_type=F32)
                m_ref[...] = m_new
            return carry

        lax.fori_loop(0, n_units, exact_unit, 0)

    lam = (jnp.exp(jnp.sum(lq1_ref[...] * lk1_ref[...], keepdims=True))
           - jnp.exp(jnp.sum(lq2_ref[...] * lk2_ref[...], keepdims=True)) + lam_init)
    on = acc_ref[...] / l_ref[...]
    o = on[:, :tq] - lam * on[:, tq:]
    ms = jnp.mean(o * o, axis=0, keepdims=True)
    o = o * lax.rsqrt(ms + SUBLN_EPS) * (sg_ref[...] * (1.0 - lam_init))
    o_ref[...] = o.T.astype(BF16)


def _attn_call(layer_idx, q, k, vt, near_bias, c_far, bound, lq1, lk1, lq2, lk2, sg_col):
    L = q.shape[0]
    t, tq = ATT_T, ATT_TQ * (layer_idx + 1) // 2
    unit = ATT_UNIT_BLOCKS * t
    assert tq % t == 0 and L % unit == 0 and L % tq == 0
    lam_init = 0.8 - 0.6 * math.exp(-0.3 * layer_idx)
    lvec = pl.BlockSpec((1, DIFF_HEAD_DIM), lambda h, i: (0, 0))
    scalar = pl.BlockSpec((1, 1, 1), lambda h, i: (h, 0, 0))
    return pl.pallas_call(
        functools.partial(_attn_kernel, lam_init=lam_init),
        grid=(N_ATT_HEADS, L // tq),
        in_specs=[
            pl.BlockSpec((tq, V_HEAD_DIM), lambda h, i: (i, h)),
            pl.BlockSpec((L, V_HEAD_DIM), lambda h, i: (0, h)),
            pl.BlockSpec((L // unit, V_HEAD_DIM, unit), lambda h, i: (0, h, 0)),
            pl.BlockSpec((1, 2, t, t), lambda h, i: (h, 0, 0, 0)),
            scalar, scalar,
            lvec, lvec, lvec, lvec,
            pl.BlockSpec((V_HEAD_DIM, 1), lambda h, i: (0, 0)),
        ],
        out_specs=pl.BlockSpec((tq, V_HEAD_DIM), lambda h, i: (i, h)),
        out_shape=jax.ShapeDtypeStruct((L, ATT_WIDTH), BF16),
        scratch_shapes=[
            pltpu.VMEM((V_HEAD_DIM, 2 * tq), F32),
            pltpu.VMEM((1, 2 * tq), F32),
            pltpu.VMEM((1, 2 * tq), F32),
            pltpu.VMEM((tq // t + 3, t, tq), F32),
            pltpu.VMEM((t, 2 * tq), F32),
            pltpu.VMEM((t, 2 * tq), BF16),
        ],
        compiler_params=_cparams(("arbitrary", "arbitrary")),
        name="diff_attn",
    )(q, k, vt, near_bias, c_far, bound, lq1, lk1, lq2, lk2, sg_col)


def _ssm_kernel(u_ref, un_ref, bm_ref, cm_ref, lam_ref, lamr_ref, d_ref, y_ref,
                bu_a_ref, bu_b_ref, carry_ref):
    ns = SSM_SLAB_GROUPS * SSM_STATE
    tt = SSM_TT
    strip = SSM_STRIP_ROWS
    n_strips = tt // strip
    bm = bm_ref[0]

    @pl.when(pl.program_id(1) == 0)
    def _():
        carry_ref[...] = jnp.zeros_like(carry_ref)
        bu_a_ref[...] = jnp.dot(u_ref[0:tt, :].astype(BF16), bm, preferred_element_type=F32)

    lr = jnp.broadcast_to(lam_ref[0, 0:1, :], (8, ns))
    li = jnp.broadcast_to(lam_ref[0, 1:2, :], (8, ns))
    pr = lamr_ref[0, 0:1, :]
    pi = lamr_ref[0, 1:2, :]
    row_id = lax.broadcasted_iota(jnp.int32, (8, ns), 0)
    z = jnp.zeros((8, ns), F32)

    def tile(cur_ref, nxt_ref, row0, next_u):
        def advance(r, sr, si, store):
            rows = pl.ds(8 * r, 8)
            nsr = lr * sr - li * si + cur_ref[rows, 0:ns]
            nsi = lr * si + li * sr + cur_ref[rows, ns:2 * ns]
            if store:
                cur_ref[rows, 0:ns] = nsr
                cur_ref[rows, ns:2 * ns] = nsi
            return nsr, nsi

        sr, si = z, z
        for k in range(n_strips):
            nxt_ref[pl.ds(k * strip, strip), :] = jnp.dot(
                next_u(k * strip, strip).astype(BF16), bm, preferred_element_type=F32)
            for r in range(k * strip // 8, (k + 1) * strip // 8):
                sr, si = advance(r, sr, si, False)
        er, ei = sr, si

        cr = carry_ref[0:1, 0:ns]
        ci = carry_ref[0:1, ns:2 * ns]
        sr, si = z, z
        for c in range(8):
            sr = jnp.where(row_id == c, cr, sr)
            si = jnp.where(row_id == c, ci, si)
            cr, ci = (pr * cr - pi * ci + er[c:c + 1, :], pr * ci + pi * cr + ei[c:c + 1, :])
        carry_ref[0:1, 0:ns] = cr
        carry_ref[0:1, ns:2 * ns] = ci

        def emit(k):
            rows = pl.ds(k * strip, strip)
            out_rows = pl.ds(row0 + k * strip, strip)
            y = (jnp.dot(cur_ref[rows, :].astype(BF16), cm_ref[0], preferred_element_type=F32)
                 + d_ref[0] * u_ref[out_rows, :])
            y_ref[out_rows, :] = jax.nn.gelu(y)

        for k in range(n_strips):
            if k > 0:
                emit(k - 1)
            for r in range(k * strip // 8, (k + 1) * strip // 8):
                sr, si = advance(r, sr, si, True)
        emit(n_strips - 1)

    tile(bu_a_ref, bu_b_ref, 0, lambda r, n: u_ref[pl.ds(tt + r, n), :])
    tile(bu_b_ref, bu_a_ref, tt, lambda r, n: un_ref[pl.ds(r, n), :])


def _ssm_call(u_perm, bmat, cmat, lam2, lamr2, d3):
    L = u_perm.shape[0]
    n_slab = N_SSM_GROUPS // SSM_SLAB_GROUPS
    n_tiles = L // SSM_TT
    assert n_tiles % 2 == 0
    ns = SSM_SLAB_GROUPS * SSM_STATE
    cw = SSM_SLAB_GROUPS * SSM_GROUP
    return pl.pallas_call(
        _ssm_kernel,
        grid=(n_slab, n_tiles // 2),
        in_specs=[
            pl.BlockSpec((2 * SSM_TT, cw), lambda s, p: (p, s)),
            pl.BlockSpec((SSM_TT, cw), lambda s, p: (jnp.minimum(2 * p + 2, n_tiles - 1), s)),
            pl.BlockSpec((1, cw, 2 * ns), lambda s, p: (s, 0, 0)),
            pl.BlockSpec((1, 2 * ns, cw), lambda s, p: (s, 0, 0)),
            pl.BlockSpec((1, 2, ns), lambda s, p: (s, 0, 0)),
            pl.BlockSpec((1, 2, ns), lambda s, p: (s, 0, 0)),
            pl.BlockSpec((1, 1, cw), lambda s, p: (s, 0, 0)),
        ],
        out_specs=pl.BlockSpec((2 * SSM_TT, cw), lambda s, p: (p, s)),
        out_shape=jax.ShapeDtypeStruct((L, SSM_WIDTH), F32),
        scratch_shapes=[
            pltpu.VMEM((SSM_TT, 2 * ns), F32),
            pltpu.VMEM((SSM_TT, 2 * ns), F32),
            pltpu.VMEM((1, 2 * ns), F32),
        ],
        compiler_params=_cparams(("arbitrary", "arbitrary")),
        name="s5_scan",
    )(u_perm, u_perm, bmat, cmat, lam2, lamr2, d3)


def _ssm_params(lam_re, lam_im, log_step, b_re, b_im, c_re, c_im, d):
    g, p, hc = N_SSM_GROUPS, SSM_STATE, SSM_GROUP
    sg = SSM_SLAB_GROUPS
    n_slab = g // sg
    lam = lax.complex(jnp.minimum(lam_re.astype(F32), -1e-4), lam_im.astype(F32))
    step = jnp.exp(log_step.astype(F32))[:, None]
    lam_bar = jnp.exp(lam * step)
    lam_bar_r = jnp.exp(lam * (step * (SSM_TT // 8)))
    b_bar = ((lam_bar - 1.0) / lam)[:, :, None] * lax.complex(b_re.astype(F32), b_im.astype(F32))
    eye = jnp.eye(sg, dtype=F32)

    def b_block(part):
        z = part.reshape(n_slab, sg, p, hc).transpose(0, 1, 3, 2)
        return (z[:, :, :, None, :] * eye[None, :, None, :, None]).reshape(n_slab, sg * hc, sg * p)

    def c_block(part):
        z = part.reshape(n_slab, sg, hc, p).transpose(0, 1, 3, 2)
        return (z[:, :, :, None, :] * eye[None, :, None, :, None]).reshape(n_slab, sg * p, sg * hc)

    bmat = jnp.concatenate([b_block(jnp.real(b_bar)), b_block(jnp.imag(b_bar))], axis=-1)
    cmat = jnp.concatenate([c_block(c_re.astype(F32)), c_block(-c_im.astype(F32))], axis=1)

    def rows(zc):
        return jnp.stack([jnp.real(zc).reshape(n_slab, sg * p),
                          jnp.imag(zc).reshape(n_slab, sg * p)], axis=1)

    d3 = d.astype(F32).reshape(n_slab, 1, sg * hc)
    return bmat.astype(BF16), cmat.astype(BF16), rows(lam_bar), rows(lam_bar_r), d3


def _out_kernel(att_ref, y_ref, gw_ref, gb_ref, woa_ref, wob_ref, x_ref, gt_ref, o_ref):
    y = y_ref[...]
    z = jnp.dot(y.astype(BF16), gw_ref[...], preferred_element_type=F32) + gb_ref[...]
    yg = (y * jax.nn.sigmoid(z)).astype(BF16)
    m = (jnp.dot(att_ref[...], woa_ref[...], preferred_element_type=F32)
         + jnp.dot(yg, wob_ref[...], preferred_element_type=F32))
    o_ref[...] = x_ref[...] + gt_ref[...] * m


def _out_call(layer, att, y, glu_w_bf, glu_b, w_out_bf, x, gt):
    L, d = x.shape
    aw, sw = ATT_WIDTH, SSM_WIDTH
    one = pl.Buffered(1)
    return pl.pallas_call(
        _out_kernel,
        grid=(L // OUT_TM,),
        in_specs=[
            pl.BlockSpec((OUT_TM, aw), lambda i: (i, 0)),
            pl.BlockSpec((OUT_TM, sw), lambda i: (i, 0)),
            pl.BlockSpec((None, sw, sw), lambda i: (layer, 0, 0), pipeline_mode=one),
            pl.BlockSpec((1, sw), lambda i: (0, 0)),
            pl.BlockSpec((None, aw, d), lambda i: (layer, 0, 0), pipeline_mode=one),
            pl.BlockSpec((None, sw, d), lambda i: (layer, 1, 0), pipeline_mode=one),
            pl.BlockSpec((OUT_TM, d), lambda i: (i, 0)),
            pl.BlockSpec((1, d), lambda i: (0, 0)),
        ],
        out_specs=pl.BlockSpec((OUT_TM, d), lambda i: (i, 0)),
        out_shape=jax.ShapeDtypeStruct((L, d), F32),
        compiler_params=_cparams(("arbitrary",)),
        name="out_proj",
    )(att, y, glu_w_bf, glu_b, w_out_bf, w_out_bf, x, gt)


def _t5_causal_buckets(dist):
    max_exact = N_BUCKETS // 2
    d = jnp.maximum(dist, 1).astype(F32)
    large = max_exact + (jnp.log(d / max_exact) / math.log(MAX_DISTANCE / max_exact)
                         * (N_BUCKETS - max_exact)).astype(jnp.int32)
    large = jnp.minimum(large, N_BUCKETS - 1)
    return jnp.where(dist < max_exact, dist, large)


def _near_bias(rel_bias):
    t = ATT_T
    assert t >= MAX_DISTANCE
    kk = jnp.arange(t, dtype=jnp.int32)[:, None]
    qq = jnp.arange(t, dtype=jnp.int32)[None, :]
    dist = jnp.stack([qq - kk, qq - kk + t], axis=0)
    bucket = _t5_causal_buckets(jnp.maximum(dist, 0))
    rb = rel_bias.astype(F32) * LOG2_E
    val = jnp.zeros((N_ATT_HEADS,) + dist.shape, F32)
    for b in range(N_BUCKETS):
        val = jnp.where((bucket == b)[None], rb[b][:, None, None, None], val)
    return jnp.where((dist >= 0)[None], val, NEG_INF)


def _chunk_interleave(a):
    L, w = a.shape
    return a.reshape(L // SSM_TT, 8, SSM_TT // 8, w).transpose(0, 2, 1, 3).reshape(L, w)


def _chunk_deinterleave(a):
    L, w = a.shape
    return a.reshape(L // SSM_TT, SSM_TT // 8, 8, w).transpose(0, 2, 1, 3).reshape(L, w)


def kernel(x, c, rel_bias, ada_w, ada_b, norm_g, ffn1_w_gate, ffn1_w_up, ffn1_w_down, ffn2_w_gate, ffn2_w_up, ffn2_w_down, w_in, w_out, q_norm_g, k_norm_g, lambda_q1, lambda_k1, lambda_q2, lambda_k2, subln_g, ssm_lambda_re, ssm_lambda_im, ssm_log_step, ssm_b_re, ssm_b_im, ssm_c_re, ssm_c_im, ssm_d, ssm_glu_w, ssm_glu_b):
    b, L, d = x.shape
    assert b == 1 and c.shape == (1, d)
    x2 = x.reshape(L, d)

    mod = _ada_call(c.reshape(d, 1), ada_w, ada_b.reshape(DEPTH, 1, N_COND * d))
    near_bias = _near_bias(rel_bias)
    c_far = (rel_bias.astype(F32)[N_BUCKETS - 1] * LOG2_E).reshape(N_ATT_HEADS, 1, 1)
    w_in_bf = w_in.astype(BF16)
    w_out_bf = w_out.astype(BF16)
    glu_w_bf = ssm_glu_w.astype(BF16)
    hd = DIFF_HEAD_DIM
    group_of = jnp.arange(ATT_WIDTH) // hd
    gmat = jnp.where(group_of[:, None] == group_of[None, :], 1.0 / hd, 0.0).astype(BF16)
    n_rep = ATT_WIDTH // hd

    for i in range(DEPTH):
        m = mod[i]
        sh1, sc1, g1, sh2, sc2, g2, sh3, sc3, g3 = [m[:, n * d:(n + 1) * d] for n in range(N_COND)]
        x2 = _ffn_call(i, x2, norm_g[i, 0][None], sh1, sc1, g1,
                       ffn1_w_gate, ffn1_w_up, ffn1_w_down)

        qg = jnp.tile(q_norm_g[i].astype(F32), n_rep)[None] * (hd ** -0.5 * LOG2_E)
        kg = jnp.tile(k_norm_g[i].astype(F32), n_rep)[None]
        q, k, vt, u = _proj_call(i, x2, norm_g[i, 1][None], sh2, sc2, w_in_bf, gmat, qg, kg)
        bound = (DIFF_HEAD_DIM * jnp.max(jnp.abs(qg[0, :hd] * kg[0, :hd]))
                 + jnp.max(rel_bias.astype(F32), axis=0) * LOG2_E).reshape(N_ATT_HEADS, 1, 1)
        att = _attn_call(i, q, k, vt, near_bias, c_far, bound, lambda_q1[i][None], lambda_k1[i][None],
                         lambda_q2[i][None], lambda_k2[i][None], subln_g[i].reshape(V_HEAD_DIM, 1))
        bmat, cmat, lam2, lamr2, d3 = _ssm_params(
            ssm_lambda_re[i], ssm_lambda_im[i], ssm_log_step[i], ssm_b_re[i], ssm_b_im[i],
            ssm_c_re[i], ssm_c_im[i], ssm_d[i])
        y = _chunk_deinterleave(_ssm_call(_chunk_interleave(u), bmat, cmat, lam2, lamr2, d3))
        x2 = _out_call(i, att, y, glu_w_bf, ssm_glu_b[i][None], w_out_bf, x2, g2)

        if i == 0:
            x2 = _ffn_call(0, x2, norm_g[i, 2][None], sh3, sc3, g3, ffn2_w_gate[0:1].astype(BF16),
                           ffn2_w_up[0:1].astype(BF16), ffn2_w_down[0:1].astype(BF16))
        else:
            x2 = _ffn_call(i, x2, norm_g[i, 2][None], sh3, sc3, g3,
                           ffn2_w_gate, ffn2_w_up, ffn2_w_down)
    return x2.reshape(b, L, d)
```

```python
import functools
import math

import jax
import jax.numpy as jnp
from jax import lax
from jax.experimental import pallas as pl
from jax.experimental.pallas import tpu as pltpu

D_MODEL = 2048
SEQ = 8192
DEPTH = 2
ATT_WIDTH = 1024
SSM_WIDTH = 1024
DIFF_HEAD_DIM = 64
V_HEAD_DIM = 128
N_ATT_HEADS = 8
SSM_GROUP = 16
N_SSM_GROUPS = 64
SSM_STATE = 64
IN_WIDTH = 4096
D_FF = 5632
N_BUCKETS = 32
MAX_DISTANCE = 128
N_COND = 9
NORM_EPS = 1e-6
SUBLN_EPS = 1e-5
NEG_INF = -1e30
LOG2_E = math.log2(math.e)

F32 = jnp.float32
BF16 = jnp.bfloat16

VMEM_LIMIT_BYTES = 60 * 1024 * 1024

ADA_TN = 2048
FFN_TM = 1024
FFN_TF = 512
FFN_NORM_ROWS = 256
PROJ_TM = 512
ATT_T = 256
ATT_TQ = 1024
ATT_UNIT_BLOCKS = 4
ATT_MIN_DENOM = 2.0 ** -40
SSM_TT = 1024
SSM_SLAB_GROUPS = 8
SSM_STRIP_ROWS = 128
OUT_TM = 512


def _cparams(sem):
    return pltpu.CompilerParams(dimension_semantics=sem, vmem_limit_bytes=VMEM_LIMIT_BYTES)


def _ada_kernel(c_ref, w_ref, b_ref, o_ref):
    c = c_ref[...]
    cs = c * jax.nn.sigmoid(c)
    o_ref[0] = jnp.sum(w_ref[0] * cs, axis=0, keepdims=True) + b_ref[0]


def _ada_call(c_col, ada_w, ada_b3):
    depth, d, n = ada_w.shape
    return pl.pallas_call(
        _ada_kernel,
        grid=(depth, n // ADA_TN),
        in_specs=[
            pl.BlockSpec((d, 1), lambda l, j: (0, 0)),
            pl.BlockSpec((1, d, ADA_TN), lambda l, j: (l, 0, j)),
            pl.BlockSpec((1, 1, ADA_TN), lambda l, j: (l, 0, j)),
        ],
        out_specs=pl.BlockSpec((1, 1, ADA_TN), lambda l, j: (l, 0, j)),
        out_shape=jax.ShapeDtypeStruct((depth, 1, n), F32),
        compiler_params=_cparams(("arbitrary", "arbitrary")),
        name="adaln",
    )(c_col, ada_w, ada_b3)


def _norm_mod(x, g, sc, sh):
    ms = jnp.mean(x * x, axis=-1, keepdims=True)
    return (x * lax.rsqrt(ms + NORM_EPS) * g) * (1.0 + sc) + sh


def _ffn_kernel(x_ref, ng_ref, sh_ref, sc_ref, gt_ref, wg_ref, wu_ref, wd_ref, o_ref, h_ref):
    j = pl.program_id(1)

    def weights():
        return (wg_ref[...].astype(BF16), wu_ref[...].astype(BF16), wd_ref[...].astype(BF16))

    def hidden_update(h, wg, wu, wd):
        g = jnp.dot(h, wg, preferred_element_type=F32)
        u = jnp.dot(h, wu, preferred_element_type=F32)
        a = (g * jax.nn.sigmoid(g) * u).astype(BF16)
        return jnp.dot(a, wd, preferred_element_type=F32)

    @pl.when(j == 0)
    def _():
        w = weights()
        for r in range(0, FFN_TM, FFN_NORM_ROWS):
            rows = pl.ds(r, FFN_NORM_ROWS)
            h = _norm_mod(x_ref[rows, :], ng_ref[...], sc_ref[...], sh_ref[...]).astype(BF16)
            h_ref[rows, :] = h
            o_ref[rows, :] = hidden_update(h, *w)

    @pl.when(j > 0)
    def _():
        o_ref[...] += hidden_update(h_ref[...], *weights())

    @pl.when(j == pl.num_programs(1) - 1)
    def _():
        o_ref[...] = x_ref[...] + (0.5 * gt_ref[...]) * o_ref[...]


def _ffn_call(layer, x, ng, sh, sc, gt, wg, wu, wd):
    L, d = x.shape
    dff = wg.shape[-1]
    vec = pl.BlockSpec((1, d), lambda i, j: (0, 0))
    return pl.pallas_call(
        _ffn_kernel,
        grid=(L // FFN_TM, dff // FFN_TF),
        in_specs=[
            pl.BlockSpec((FFN_TM, d), lambda i, j: (i, 0), pipeline_mode=pl.Buffered(1)),
            vec, vec, vec, vec,
            pl.BlockSpec((None, d, FFN_TF), lambda i, j: (layer, 0, j)),
            pl.BlockSpec((None, d, FFN_TF), lambda i, j: (layer, 0, j)),
            pl.BlockSpec((None, FFN_TF, d), lambda i, j: (layer, j, 0)),
        ],
        out_specs=pl.BlockSpec((FFN_TM, d), lambda i, j: (i, 0)),
        out_shape=jax.ShapeDtypeStruct((L, d), F32),
        scratch_shapes=[pltpu.VMEM((FFN_TM, d), BF16)],
        compiler_params=_cparams(("arbitrary", "arbitrary")),
        name="ffn",
    )(x, ng, sh, sc, gt, wg, wu, wd)


def _proj_kernel(x_ref, ng_ref, sh_ref, sc_ref, w_ref, gm_ref, qg_ref, kg_ref,
                 q_ref, k_ref, vt_ref, u_ref):
    h = _norm_mod(x_ref[...], ng_ref[...], sc_ref[...], sh_ref[...]).astype(BF16)
    aw = ATT_WIDTH

    def head_norm(z, g):
        ms = jnp.dot((z * z).astype(BF16), gm_ref[...], preferred_element_type=F32)
        return z * lax.rsqrt(ms + NORM_EPS) * g

    q = jnp.dot(h, w_ref[:, 0:aw], preferred_element_type=F32)
    q_ref[...] = head_norm(q, qg_ref[...]).astype(BF16)
    k = jnp.dot(h, w_ref[:, aw:2 * aw], preferred_element_type=F32)
    k_ref[...] = head_norm(k, kg_ref[...]).astype(BF16)
    v = jnp.dot(h, w_ref[:, 2 * aw:3 * aw], preferred_element_type=F32)
    vt_ref[0] = v.T.astype(BF16)
    u_ref[...] = jnp.dot(h, w_ref[:, 3 * aw:4 * aw], preferred_element_type=F32)


def _proj_call(layer, x, ng, sh, sc, w_in_bf, gmat, qg, kg):
    L, d = x.shape
    aw = ATT_WIDTH
    unit = ATT_UNIT_BLOCKS * ATT_T
    per_unit = unit // PROJ_TM
    vec = pl.BlockSpec((1, d), lambda i: (0, 0))
    vec_a = pl.BlockSpec((1, aw), lambda i: (0, 0))
    return pl.pallas_call(
        _proj_kernel,
        grid=(L // PROJ_TM,),
        in_specs=[
            pl.BlockSpec((PROJ_TM, d), lambda i: (i, 0)),
            vec, vec, vec,
            pl.BlockSpec((None, d, IN_WIDTH), lambda i: (layer, 0, 0),
                         pipeline_mode=pl.Buffered(1)),
            pl.BlockSpec((aw, aw), lambda i: (0, 0), pipeline_mode=pl.Buffered(1)),
            vec_a, vec_a,
        ],
        out_specs=[
            pl.BlockSpec((PROJ_TM, aw), lambda i: (i, 0)),
            pl.BlockSpec((PROJ_TM, aw), lambda i: (i, 0)),
            pl.BlockSpec((1, aw, PROJ_TM), lambda i: (i // per_unit, 0, i % per_unit)),
            pl.BlockSpec((PROJ_TM, SSM_WIDTH), lambda i: (i, 0)),
        ],
        out_shape=[
            jax.ShapeDtypeStruct((L, aw), BF16),
            jax.ShapeDtypeStruct((L, aw), BF16),
            jax.ShapeDtypeStruct((L // unit, aw, unit), BF16),
            jax.ShapeDtypeStruct((L, SSM_WIDTH), F32),
        ],
        compiler_params=_cparams(("arbitrary",)),
        name="in_proj",
    )(x, ng, sh, sc, w_in_bf, gmat, qg, kg)


def _attn_kernel(q_ref, k_ref, vt_ref, nbias_ref, cfar_ref, bound_ref, lq1_ref, lk1_ref, lq2_ref,
                 lk2_ref, sg_ref, o_ref, acc_ref, m_ref, l_ref, bias_ref, s_ref, p_ref, *, lam_init):
    t = ATT_T
    tq = q_ref.shape[0]
    qb = tq // t
    nb = ATT_UNIT_BLOCKS
    qi = pl.program_id(1)
    n_blk = k_ref.shape[0] // t

    @pl.when(qi == 0)
    def _():
        bound = bound_ref[0]
        tiles = {2: jnp.broadcast_to(cfar_ref[0] - bound, (t, t)), 1: nbias_ref[0, 1] - bound,
                 0: nbias_ref[0, 0] - bound, -1: jnp.full((t, t), NEG_INF, F32)}
        for d in range(-qb, 3):
            for a in range(qb):
                bias_ref[d + qb, :, a * t:(a + 1) * t] = tiles[max(min(d + a, 2), -1)]

    q = q_ref[...]
    lane = lax.broadcasted_iota(jnp.int32, q.shape, 1)
    zero = jnp.zeros_like(q)
    qcat = jnp.concatenate([jnp.where(lane < DIFF_HEAD_DIM, q, zero),
                            jnp.where(lane >= DIFF_HEAD_DIM, q, zero)], axis=0)

    n_units = (qb * qi + qb + nb - 1) // nb

    def scores(blk):
        off = pl.multiple_of(blk * t, t)
        return lax.dot_general(k_ref[pl.ds(off, t), :], qcat, (((1,), (1,)), ((), ())),
                               preferred_element_type=F32)

    def biased(s, blk):
        bias = bias_ref[jnp.clip(qb * qi - blk, -qb, 2) + qb]
        return s[:, :tq] + bias, s[:, tq:] + bias

    acc_ref[...] = jnp.zeros(acc_ref.shape, F32)
    l_ref[...] = jnp.zeros(l_ref.shape, F32)
    s_ref[...] = scores(0)
    p_ref[...] = jnp.zeros(p_ref.shape, BF16)

    def fast_unit(u, carry):
        s_cur = s_ref[...]
        pv = jnp.dot(vt_ref[jnp.maximum(u - 1, 0), :, (nb - 1) * t:nb * t], p_ref[...],
                     preferred_element_type=F32)
        lsum = jnp.zeros((1, 2 * tq), F32)
        for b in range(nb):
            blk = u * nb + b
            s_next = scores(jnp.minimum(blk + 1, n_blk - 1))
            s0, s1 = biased(s_cur, blk)
            p = jnp.concatenate([jnp.exp2(s0), jnp.exp2(s1)], axis=1)
            lsum = lsum + jnp.sum(p, axis=0, keepdims=True)
            if b < nb - 1:
                pv = pv + jnp.dot(vt_ref[u, :, b * t:(b + 1) * t], p.astype(BF16),
                                  preferred_element_type=F32)
            else:
                p_ref[...] = p.astype(BF16)
            s_cur = s_next
        s_ref[...] = s_cur
        acc_ref[...] += pv
        l_ref[...] += lsum
        return carry

    lax.fori_loop(0, n_units, fast_unit, 0)
    acc_ref[...] += jnp.dot(vt_ref[n_units - 1, :, (nb - 1) * t:nb * t], p_ref[...],
                            preferred_element_type=F32)

    l_min = jnp.min(l_ref[...], keepdims=True)
    underflow = jnp.logical_not(l_min[0, 0] > ATT_MIN_DENOM)

    @pl.when(underflow)
    def _():
        m_ref[...] = jnp.full(m_ref.shape, NEG_INF, F32)
        l_ref[...] = jnp.zeros(l_ref.shape, F32)
        acc_ref[...] = jnp.zeros(acc_ref.shape, F32)

        def exact_unit(u, carry):
            for b in range(nb):
                blk = u * nb + b
                s = jnp.concatenate(biased(scores(blk), blk), axis=1)
                m_old = m_ref[...]
                m_new = jnp.maximum(m_old, jnp.max(s, axis=0, keepdims=True))
                alpha = jnp.exp2(m_old - m_new)
                p = jnp.exp2(s - m_new)
                l_ref[...] = alpha * l_ref[...] + jnp.sum(p, axis=0, keepdims=True)
                acc_ref[...] = alpha * acc_ref[...] + jnp.dot(
                    vt_ref[u, :, b * t:(b + 1) * t], p.astype(BF16), preferred_element_type=F32)
                m_ref[...] = m_new
            return carry

        lax.fori_loop(0, n_units, exact_unit, 0)

    lam = (jnp.exp(jnp.sum(lq1_ref[...] * lk1_ref[...], keepdims=True))
           - jnp.exp(jnp.sum(lq2_ref[...] * lk2_ref[...], keepdims=True)) + lam_init)
    on = acc_ref[...] / l_ref[...]
    o = on[:, :tq] - lam * on[:, tq:]
    ms = jnp.mean(o * o, axis=0, keepdims=True)
    o = o * lax.rsqrt(ms + SUBLN_EPS) * (sg_ref[...] * (1.0 - lam_init))
    o_ref[...] = o.T.astype(BF16)


def _attn_call(layer_idx, q, k, vt, near_bias, c_far, bound, lq1, lk1, lq2, lk2, sg_col):
    L = q.shape[0]
    t, tq = ATT_T, ATT_TQ
    unit = ATT_UNIT_BLOCKS * t
    assert tq % t == 0 and L % unit == 0 and L % tq == 0
    lam_init = 0.8 - 0.6 * math.exp(-0.3 * layer_idx)
    lvec = pl.BlockSpec((1, DIFF_HEAD_DIM), lambda h, i: (0, 0))
    scalar = pl.BlockSpec((1, 1, 1), lambda h, i: (h, 0, 0))
    return pl.pallas_call(
        functools.partial(_attn_kernel, lam_init=lam_init),
        grid=(N_ATT_HEADS, L // tq),
        in_specs=[
            pl.BlockSpec((tq, V_HEAD_DIM), lambda h, i: (i, h)),
            pl.BlockSpec((L, V_HEAD_DIM), lambda h, i: (0, h)),
            pl.BlockSpec((L // unit, V_HEAD_DIM, unit), lambda h, i: (0, h, 0)),
            pl.BlockSpec((1, 2, t, t), lambda h, i: (h, 0, 0, 0)),
            scalar, scalar,
            lvec, lvec, lvec, lvec,
            pl.BlockSpec((V_HEAD_DIM, 1), lambda h, i: (0, 0)),
        ],
        out_specs=pl.BlockSpec((tq, V_HEAD_DIM), lambda h, i: (i, h)),
        out_shape=jax.ShapeDtypeStruct((L, ATT_WIDTH), BF16),
        scratch_shapes=[
            pltpu.VMEM((V_HEAD_DIM, 2 * tq), F32),
            pltpu.VMEM((1, 2 * tq), F32),
            pltpu.VMEM((1, 2 * tq), F32),
            pltpu.VMEM((tq // t + 3, t, tq), F32),
            pltpu.VMEM((t, 2 * tq), F32),
            pltpu.VMEM((t, 2 * tq), BF16),
        ],
        compiler_params=_cparams(("arbitrary", "arbitrary")),
        name="diff_attn",
    )(q, k, vt, near_bias, c_far, bound, lq1, lk1, lq2, lk2, sg_col)


def _ssm_kernel(u_ref, un_ref, bm_ref, cm_ref, lam_ref, lamr_ref, d_ref, y_ref,
                bu_a_ref, bu_b_ref, carry_ref):
    ns = SSM_SLAB_GROUPS * SSM_STATE
    tt = SSM_TT
    strip = SSM_STRIP_ROWS
    n_strips = tt // strip
    bm = bm_ref[0]

    @pl.when(pl.program_id(1) == 0)
    def _():
        carry_ref[...] = jnp.zeros_like(carry_ref)
        bu_a_ref[...] = jnp.dot(u_ref[0:tt, :].astype(BF16), bm, preferred_element_type=F32)

    lr = jnp.broadcast_to(lam_ref[0, 0:1, :], (8, ns))
    li = jnp.broadcast_to(lam_ref[0, 1:2, :], (8, ns))
    pr = lamr_ref[0, 0:1, :]
    pi = lamr_ref[0, 1:2, :]
    row_id = lax.broadcasted_iota(jnp.int32, (8, ns), 0)
    z = jnp.zeros((8, ns), F32)

    def tile(cur_ref, nxt_ref, row0, next_u):
        def advance(r, sr, si, store):
            rows = pl.ds(8 * r, 8)
            nsr = lr * sr - li * si + cur_ref[rows, 0:ns]
            nsi = lr * si + li * sr + cur_ref[rows, ns:2 * ns]
            if store:
                cur_ref[rows, 0:ns] = nsr
                cur_ref[rows, ns:2 * ns] = nsi
            return nsr, nsi

        sr, si = z, z
        for k in range(n_strips):
            nxt_ref[pl.ds(k * strip, strip), :] = jnp.dot(
                next_u(k * strip, strip).astype(BF16), bm, preferred_element_type=F32)
            for r in range(k * strip // 8, (k + 1) * strip // 8):
                sr, si = advance(r, sr, si, False)
        er, ei = sr, si

        cr = carry_ref[0:1, 0:ns]
        ci = carry_ref[0:1, ns:2 * ns]
        sr, si = z, z
        for c in range(8):
            sr = jnp.where(row_id == c, cr, sr)
            si = jnp.where(row_id == c, ci, si)
            cr, ci = (pr * cr - pi * ci + er[c:c + 1, :], pr * ci + pi * cr + ei[c:c + 1, :])
        carry_ref[0:1, 0:ns] = cr
        carry_ref[0:1, ns:2 * ns] = ci

        def emit(k):
            rows = pl.ds(k * strip, strip)
            out_rows = pl.ds(row0 + k * strip, strip)
            y = (jnp.dot(cur_ref[rows, :].astype(BF16), cm_ref[0], preferred_element_type=F32)
                 + d_ref[0] * u_ref[out_rows, :])
            y_ref[out_rows, :] = jax.nn.gelu(y)

        for k in range(n_strips):
            if k > 0:
                emit(k - 1)
            for r in range(k * strip // 8, (k + 1) * strip // 8):
                sr, si = advance(r, sr, si, True)
        emit(n_strips - 1)

    tile(bu_a_ref, bu_b_ref, 0, lambda r, n: u_ref[pl.ds(tt + r, n), :])
    tile(bu_b_ref, bu_a_ref, tt, lambda r, n: un_ref[pl.ds(r, n), :])


def _ssm_call(u_perm, bmat, cmat, lam2, lamr2, d3):
    L = u_perm.shape[0]
    n_slab = N_SSM_GROUPS // SSM_SLAB_GROUPS
    n_tiles = L // SSM_TT
    assert n_tiles % 2 == 0
    ns = SSM_SLAB_GROUPS * SSM_STATE
    cw = SSM_SLAB_GROUPS * SSM_GROUP
    return pl.pallas_call(
        _ssm_kernel,
        grid=(n_slab, n_tiles // 2),
        in_specs=[
            pl.BlockSpec((2 * SSM_TT, cw), lambda s, p: (p, s)),
            pl.BlockSpec((SSM_TT, cw), lambda s, p: (jnp.minimum(2 * p + 2, n_tiles - 1), s)),
            pl.BlockSpec((1, cw, 2 * ns), lambda s, p: (s, 0, 0)),
            pl.BlockSpec((1, 2 * ns, cw), lambda s, p: (s, 0, 0)),
            pl.BlockSpec((1, 2, ns), lambda s, p: (s, 0, 0)),
            pl.BlockSpec((1, 2, ns), lambda s, p: (s, 0, 0)),
            pl.BlockSpec((1, 1, cw), lambda s, p: (s, 0, 0)),
        ],
        out_specs=pl.BlockSpec((2 * SSM_TT, cw), lambda s, p: (p, s)),
        out_shape=jax.ShapeDtypeStruct((L, SSM_WIDTH), F32),
        scratch_shapes=[
            pltpu.VMEM((SSM_TT, 2 * ns), F32),
            pltpu.VMEM((SSM_TT, 2 * ns), F32),
            pltpu.VMEM((1, 2 * ns), F32),
        ],
        compiler_params=_cparams(("arbitrary", "arbitrary")),
        name="s5_scan",
    )(u_perm, u_perm, bmat, cmat, lam2, lamr2, d3)


def _ssm_params(lam_re, lam_im, log_step, b_re, b_im, c_re, c_im, d):
    g, p, hc = N_SSM_GROUPS, SSM_STATE, SSM_GROUP
    sg = SSM_SLAB_GROUPS
    n_slab = g // sg
    lam = lax.complex(jnp.minimum(lam_re.astype(F32), -1e-4), lam_im.astype(F32))
    step = jnp.exp(log_step.astype(F32))[:, None]
    lam_bar = jnp.exp(lam * step)
    lam_bar_r = jnp.exp(lam * (step * (SSM_TT // 8)))
    b_bar = ((lam_bar - 1.0) / lam)[:, :, None] * lax.complex(b_re.astype(F32), b_im.astype(F32))
    eye = jnp.eye(sg, dtype=F32)

    def b_block(part):
        z = part.reshape(n_slab, sg, p, hc).transpose(0, 1, 3, 2)
        return (z[:, :, :, None, :] * eye[None, :, None, :, None]).reshape(n_slab, sg * hc, sg * p)

    def c_block(part):
        z = part.reshape(n_slab, sg, hc, p).transpose(0, 1, 3, 2)
        return (z[:, :, :, None, :] * eye[None, :, None, :, None]).reshape(n_slab, sg * p, sg * hc)

    bmat = jnp.concatenate([b_block(jnp.real(b_bar)), b_block(jnp.imag(b_bar))], axis=-1)
    cmat = jnp.concatenate([c_block(c_re.astype(F32)), c_block(-c_im.astype(F32))], axis=1)

    def rows(zc):
        return jnp.stack([jnp.real(zc).reshape(n_slab, sg * p),
                          jnp.imag(zc).reshape(n_slab, sg * p)], axis=1)

    d3 = d.astype(F32).reshape(n_slab, 1, sg * hc)
    return bmat.astype(BF16), cmat.astype(BF16), rows(lam_bar), rows(lam_bar_r), d3


def _out_kernel(att_ref, y_ref, gw_ref, gb_ref, woa_ref, wob_ref, x_ref, gt_ref, o_ref):
    y = y_ref[...]
    z = jnp.dot(y.astype(BF16), gw_ref[...], preferred_element_type=F32) + gb_ref[...]
    yg = (y * jax.nn.sigmoid(z)).astype(BF16)
    m = (jnp.dot(att_ref[...], woa_ref[...], preferred_element_type=F32)
         + jnp.dot(yg, wob_ref[...], preferred_element_type=F32))
    o_ref[...] = x_ref[...] + gt_ref[...] * m


def _out_call(layer, att, y, glu_w_bf, glu_b, w_out_bf, x, gt):
    L, d = x.shape
    aw, sw = ATT_WIDTH, SSM_WIDTH
    one = pl.Buffered(1)
    return pl.pallas_call(
        _out_kernel,
        grid=(L // OUT_TM,),
        in_specs=[
            pl.BlockSpec((OUT_TM, aw), lambda i: (i, 0)),
            pl.BlockSpec((OUT_TM, sw), lambda i: (i, 0)),
            pl.BlockSpec((None, sw, sw), lambda i: (layer, 0, 0), pipeline_mode=one),
            pl.BlockSpec((1, sw), lambda i: (0, 0)),
            pl.BlockSpec((None, aw, d), lambda i: (layer, 0, 0), pipeline_mode=one),
            pl.BlockSpec((None, sw, d), lambda i: (layer, 1, 0), pipeline_mode=one),
            pl.BlockSpec((OUT_TM, d), lambda i: (i, 0)),
            pl.BlockSpec((1, d), lambda i: (0, 0)),
        ],
        out_specs=pl.BlockSpec((OUT_TM, d), lambda i: (i, 0)),
        out_shape=jax.ShapeDtypeStruct((L, d), F32),
        compiler_params=_cparams(("arbitrary",)),
        name="out_proj",
    )(att, y, glu_w_bf, glu_b, w_out_bf, w_out_bf, x, gt)


def _t5_causal_buckets(dist):
    max_exact = N_BUCKETS // 2
    d = jnp.maximum(dist, 1).astype(F32)
    large = max_exact + (jnp.log(d / max_exact) / math.log(MAX_DISTANCE / max_exact)
                         * (N_BUCKETS - max_exact)).astype(jnp.int32)
    large = jnp.minimum(large, N_BUCKETS - 1)
    return jnp.where(dist < max_exact, dist, large)


def _near_bias(rel_bias):
    t = ATT_T
    assert t >= MAX_DISTANCE
    kk = jnp.arange(t, dtype=jnp.int32)[:, None]
    qq = jnp.arange(t, dtype=jnp.int32)[None, :]
    dist = jnp.stack([qq - kk, qq - kk + t], axis=0)
    bucket = _t5_causal_buckets(jnp.maximum(dist, 0))
    rb = rel_bias.astype(F32) * LOG2_E
    val = jnp.zeros((N_ATT_HEADS,) + dist.shape, F32)
    for b in range(N_BUCKETS):
        val = jnp.where((bucket == b)[None], rb[b][:, None, None, None], val)
    return jnp.where((dist >= 0)[None], val, NEG_INF)


def _chunk_interleave(a):
    L, w = a.shape
    return a.reshape(L // SSM_TT, 8, SSM_TT // 8, w).transpose(0, 2, 1, 3).reshape(L, w)


def _chunk_deinterleave(a):
    L, w = a.shape
    return a.reshape(L // SSM_TT, SSM_TT // 8, 8, w).transpose(0, 2, 1, 3).reshape(L, w)


def kernel(x, c, rel_bias, ada_w, ada_b, norm_g, ffn1_w_gate, ffn1_w_up, ffn1_w_down, ffn2_w_gate, ffn2_w_up, ffn2_w_down, w_in, w_out, q_norm_g, k_norm_g, lambda_q1, lambda_k1, lambda_q2, lambda_k2, subln_g, ssm_lambda_re, ssm_lambda_im, ssm_log_step, ssm_b_re, ssm_b_im, ssm_c_re, ssm_c_im, ssm_d, ssm_glu_w, ssm_glu_b):
    b, L, d = x.shape
    assert b == 1 and c.shape == (1, d)
    x2 = x.reshape(L, d)

    mod = _ada_call(c.reshape(d, 1), ada_w, ada_b.reshape(DEPTH, 1, N_COND * d))
    near_bias = _near_bias(rel_bias)
    c_far = (rel_bias.astype(F32)[N_BUCKETS - 1] * LOG2_E).reshape(N_ATT_HEADS, 1, 1)
    w_in_bf = w_in.astype(BF16)
    w_out_bf = w_out.astype(BF16)
    glu_w_bf = ssm_glu_w.astype(BF16)
    hd = DIFF_HEAD_DIM
    group_of = jnp.arange(ATT_WIDTH) // hd
    gmat = jnp.where(group_of[:, None] == group_of[None, :], 1.0 / hd, 0.0).astype(BF16)
    n_rep = ATT_WIDTH // hd

    for i in range(DEPTH):
        m = mod[i]
        sh1, sc1, g1, sh2, sc2, g2, sh3, sc3, g3 = [m[:, n * d:(n + 1) * d] for n in range(N_COND)]
        x2 = _ffn_call(i, x2, norm_g[i, 0][None], sh1, sc1, g1,
                       ffn1_w_gate, ffn1_w_up, ffn1_w_down)

        qg = jnp.tile(q_norm_g[i].astype(F32), n_rep)[None] * (hd ** -0.5 * LOG2_E)
        kg = jnp.tile(k_norm_g[i].astype(F32), n_rep)[None]
        q, k, vt, u = _proj_call(i, x2, norm_g[i, 1][None], sh2, sc2, w_in_bf, gmat, qg, kg)
        bound = (DIFF_HEAD_DIM * jnp.max(jnp.abs(qg[0, :hd] * kg[0, :hd]))
                 + jnp.max(rel_bias.astype(F32), axis=0) * LOG2_E).reshape(N_ATT_HEADS, 1, 1)
        att = _attn_call(i, q, k, vt, near_bias, c_far, bound, lambda_q1[i][None], lambda_k1[i][None],
                         lambda_q2[i][None], lambda_k2[i][None], subln_g[i].reshape(V_HEAD_DIM, 1))
        bmat, cmat, lam2, lamr2, d3 = _ssm_params(
            ssm_lambda_re[i], ssm_lambda_im[i], ssm_log_step[i], ssm_b_re[i], ssm_b_im[i],
            ssm_c_re[i], ssm_c_im[i], ssm_d[i])
        y = _chunk_deinterleave(_ssm_call(_chunk_interleave(u), bmat, cmat, lam2, lamr2, d3))
        x2 = _out_call(i, att, y, glu_w_bf, ssm_glu_b[i][None], w_out_bf, x2, g2)

        x2 = _ffn_call(i, x2, norm_g[i, 2][None], sh3, sc3, g3,
                       ffn2_w_gate, ffn2_w_up, ffn2_w_down)
    return x2.reshape(b, L, d)
```

```python
import functools
import math

import jax
import jax.numpy as jnp
from jax import lax
from jax.experimental import pallas as pl
from jax.experimental.pallas import tpu as pltpu

D_MODEL = 2048
SEQ = 8192
DEPTH = 2
ATT_WIDTH = 1024
SSM_WIDTH = 1024
DIFF_HEAD_DIM = 64
V_HEAD_DIM = 128
N_ATT_HEADS = 8
SSM_GROUP = 16
N_SSM_GROUPS = 64
SSM_STATE = 64
IN_WIDTH = 4096
D_FF = 5632
N_BUCKETS = 32
MAX_DISTANCE = 128
N_COND = 9
NORM_EPS = 1e-6
SUBLN_EPS = 1e-5
NEG_INF = -1e30
LOG2_E = math.log2(math.e)

F32 = jnp.float32
BF16 = jnp.bfloat16

VMEM_LIMIT_BYTES = 60 * 1024 * 1024

ADA_TN = 2048
FFN_TM = 1024
FFN_TF = 512
FFN_NORM_ROWS = 256
PROJ_TM = 512
PROJ_NORM_LANES = 128
ATT_T = 256
ATT_TQ = 1024
ATT_MIN_DENOM = 2.0 ** -40
SSM_TT = 1024
SSM_SLAB_GROUPS = 8
SSM_STRIP_ROWS = 128
OUT_TM = 512


def _cparams(sem):
    return pltpu.CompilerParams(dimension_semantics=sem, vmem_limit_bytes=VMEM_LIMIT_BYTES)


def _ada_kernel(c_ref, w_ref, b_ref, o_ref):
    c = c_ref[...]
    cs = c * jax.nn.sigmoid(c)
    o_ref[0] = jnp.sum(w_ref[0] * cs, axis=0, keepdims=True) + b_ref[0]


def _ada_call(c_col, ada_w, ada_b3):
    depth, d, n = ada_w.shape
    return pl.pallas_call(
        _ada_kernel,
        grid=(depth, n // ADA_TN),
        in_specs=[
            pl.BlockSpec((d, 1), lambda l, j: (0, 0)),
            pl.BlockSpec((1, d, ADA_TN), lambda l, j: (l, 0, j)),
            pl.BlockSpec((1, 1, ADA_TN), lambda l, j: (l, 0, j)),
        ],
        out_specs=pl.BlockSpec((1, 1, ADA_TN), lambda l, j: (l, 0, j)),
        out_shape=jax.ShapeDtypeStruct((depth, 1, n), F32),
        compiler_params=_cparams(("arbitrary", "arbitrary")),
        name="adaln",
    )(c_col, ada_w, ada_b3)


def _norm_mod(x, g, sc, sh):
    ms = jnp.mean(x * x, axis=-1, keepdims=True)
    return (x * lax.rsqrt(ms + NORM_EPS) * g) * (1.0 + sc) + sh


def _ffn_kernel(x_ref, ng_ref, sh_ref, sc_ref, gt_ref, wg_ref, wu_ref, wd_ref, o_ref, h_ref):
    j = pl.program_id(1)

    def weights():
        return (wg_ref[...].astype(BF16), wu_ref[...].astype(BF16), wd_ref[...].astype(BF16))

    def hidden_update(h, wg, wu, wd):
        g = jnp.dot(h, wg, preferred_element_type=F32)
        u = jnp.dot(h, wu, preferred_element_type=F32)
        a = (g * jax.nn.sigmoid(g) * u).astype(BF16)
        return jnp.dot(a, wd, preferred_element_type=F32)

    @pl.when(j == 0)
    def _():
        w = weights()
        for r in range(0, FFN_TM, FFN_NORM_ROWS):
            rows = pl.ds(r, FFN_NORM_ROWS)
            h = _norm_mod(x_ref[rows, :], ng_ref[...], sc_ref[...], sh_ref[...]).astype(BF16)
            h_ref[rows, :] = h
            o_ref[rows, :] = hidden_update(h, *w)

    @pl.when(j > 0)
    def _():
        o_ref[...] += hidden_update(h_ref[...], *weights())

    @pl.when(j == pl.num_programs(1) - 1)
    def _():
        o_ref[...] = x_ref[...] + (0.5 * gt_ref[...]) * o_ref[...]


def _ffn_call(layer, x, ng, sh, sc, gt, wg, wu, wd):
    L, d = x.shape
    dff = wg.shape[-1]
    vec = pl.BlockSpec((1, d), lambda i, j: (0, 0))
    return pl.pallas_call(
        _ffn_kernel,
        grid=(L // FFN_TM, dff // FFN_TF),
        in_specs=[
            pl.BlockSpec((FFN_TM, d), lambda i, j: (i, 0), pipeline_mode=pl.Buffered(1)),
            vec, vec, vec, vec,
            pl.BlockSpec((None, d, FFN_TF), lambda i, j: (layer, 0, j)),
            pl.BlockSpec((None, d, FFN_TF), lambda i, j: (layer, 0, j)),
            pl.BlockSpec((None, FFN_TF, d), lambda i, j: (layer, j, 0)),
        ],
        out_specs=pl.BlockSpec((FFN_TM, d), lambda i, j: (i, 0)),
        out_shape=jax.ShapeDtypeStruct((L, d), F32),
        scratch_shapes=[pltpu.VMEM((FFN_TM, d), BF16)],
        compiler_params=_cparams(("arbitrary", "arbitrary")),
        name="ffn",
    )(x, ng, sh, sc, gt, wg, wu, wd)


def _proj_kernel(x_ref, ng_ref, sh_ref, sc_ref, w_ref, gm_ref, qg_ref, kg_ref,
                 q_ref, k_ref, vt_ref, u_ref):
    h = _norm_mod(x_ref[...], ng_ref[...], sc_ref[...], sh_ref[...]).astype(BF16)
    aw = ATT_WIDTH

    def head_norm(z, g):
        sq = (z * z).astype(BF16)
        lanes = gm_ref.shape[0]
        ms = jnp.concatenate(
            [jnp.dot(sq[:, c:c + lanes], gm_ref[...], preferred_element_type=F32)
             for c in range(0, aw, lanes)], axis=1)
        return z * lax.rsqrt(ms + NORM_EPS) * g

    q = jnp.dot(h, w_ref[:, 0:aw], preferred_element_type=F32)
    q_ref[...] = head_norm(q, qg_ref[...]).astype(BF16)
    k = jnp.dot(h, w_ref[:, aw:2 * aw], preferred_element_type=F32)
    k_ref[...] = head_norm(k, kg_ref[...]).astype(BF16)
    v = jnp.dot(h, w_ref[:, 2 * aw:3 * aw], preferred_element_type=F32)
    vt_ref[0] = v.T.astype(BF16)
    u_ref[...] = jnp.dot(h, w_ref[:, 3 * aw:4 * aw], preferred_element_type=F32)


def _proj_call(layer, x, ng, sh, sc, w_in_bf, gmat, qg, kg):
    L, d = x.shape
    aw = ATT_WIDTH
    unit = ATT_TQ
    per_unit = unit // PROJ_TM
    vec = pl.BlockSpec((1, d), lambda i: (0, 0))
    vec_a = pl.BlockSpec((1, aw), lambda i: (0, 0))
    return pl.pallas_call(
        _proj_kernel,
        grid=(L // PROJ_TM,),
        in_specs=[
            pl.BlockSpec((PROJ_TM, d), lambda i: (i, 0)),
            vec, vec, vec,
            pl.BlockSpec((None, d, IN_WIDTH), lambda i: (layer, 0, 0),
                         pipeline_mode=pl.Buffered(1)),
            pl.BlockSpec((PROJ_NORM_LANES, PROJ_NORM_LANES), lambda i: (0, 0)),
            vec_a, vec_a,
        ],
        out_specs=[
            pl.BlockSpec((PROJ_TM, aw), lambda i: (i, 0)),
            pl.BlockSpec((PROJ_TM, aw), lambda i: (i, 0)),
            pl.BlockSpec((1, aw, PROJ_TM), lambda i: (i // per_unit, 0, i % per_unit)),
            pl.BlockSpec((PROJ_TM, SSM_WIDTH), lambda i: (i, 0)),
        ],
        out_shape=[
            jax.ShapeDtypeStruct((L, aw), BF16),
            jax.ShapeDtypeStruct((L, aw), BF16),
            jax.ShapeDtypeStruct((L // unit, aw, unit), BF16),
            jax.ShapeDtypeStruct((L, SSM_WIDTH), F32),
        ],
        compiler_params=_cparams(("arbitrary",)),
        name="in_proj",
    )(x, ng, sh, sc, w_in_bf, gmat, qg, kg)


def _attn_kernel(q_ref, k_ref, vt_ref, nbias_ref, cfar_ref, bound_ref, lq1_ref, lk1_ref, lq2_ref,
                 lk2_ref, sg_ref, o_ref, acc_ref, m_ref, l_ref, bias_ref, s_ref, p_ref, *, lam_init):
    t = ATT_T
    tq = ATT_TQ
    qb = tq // t
    n_tiles = q_ref.shape[0] // tq

    bound = bound_ref[0]
    tiles = {2: jnp.broadcast_to(cfar_ref[0] - bound, (t, t)), 1: nbias_ref[0, 1] - bound,
             0: nbias_ref[0, 0] - bound, -1: jnp.full((t, t), NEG_INF, F32)}
    for d in range(-qb, 3):
        for a in range(qb):
            bias_ref[d + qb, :, a * t:(a + 1) * t] = tiles[max(min(d + a, 2), -1)]

    lam = (jnp.exp(jnp.sum(lq1_ref[...] * lk1_ref[...], keepdims=True))
           - jnp.exp(jnp.sum(lq2_ref[...] * lk2_ref[...], keepdims=True)) + lam_init)
    out_gain = sg_ref[...] * (1.0 - lam_init)

    def load_qcat(qi):
        q = q_ref[pl.ds(pl.multiple_of(qi * tq, tq), tq), :]
        lane = lax.broadcasted_iota(jnp.int32, q.shape, 1)
        zero = jnp.zeros_like(q)
        return jnp.concatenate([jnp.where(lane < DIFF_HEAD_DIM, q, zero),
                                jnp.where(lane >= DIFF_HEAD_DIM, q, zero)], axis=0)

    def scores(blk, qmat):
        off = pl.multiple_of(blk * t, t)
        return lax.dot_general(k_ref[pl.ds(off, t), :], qmat, (((1,), (1,)), ((), ())),
                               preferred_element_type=F32)

    def start_tile(qcat):
        acc_ref[...] = jnp.zeros(acc_ref.shape, F32)
        l_ref[...] = jnp.zeros(l_ref.shape, F32)
        p_ref[...] = jnp.zeros(p_ref.shape, BF16)
        s_ref[...] = scores(0, qcat)

    start_tile(load_qcat(0))

    def tile_body(qi, carry):
        qcat = load_qcat(qi)

        def biased(s, blk):
            bias = bias_ref[jnp.clip(qb * qi - blk, -qb, 2) + qb]
            return s[:, :tq] + bias, s[:, tq:] + bias

        def fast_unit(u, c):
            s_cur = s_ref[...]
            pv = jnp.dot(vt_ref[jnp.maximum(u - 1, 0), :, (qb - 1) * t:qb * t], p_ref[...],
                         preferred_element_type=F32)
            lsum = jnp.zeros((1, 2 * tq), F32)
            for b in range(qb):
                blk = u * qb + b
                s_next = scores(blk + 1, qcat)
                s0, s1 = biased(s_cur, blk)
                p = jnp.concatenate([jnp.exp2(s0), jnp.exp2(s1)], axis=1)
                lsum = lsum + jnp.sum(p, axis=0, keepdims=True)
                if b < qb - 1:
                    pv = pv + jnp.dot(vt_ref[u, :, b * t:(b + 1) * t], p.astype(BF16),
                                      preferred_element_type=F32)
                else:
                    p_ref[...] = p.astype(BF16)
                s_cur = s_next
            s_ref[...] = s_cur
            acc_ref[...] += pv
            l_ref[...] += lsum
            return c

        lax.fori_loop(0, qi, fast_unit, 0)

        acc_ref[...] += jnp.dot(vt_ref[jnp.maximum(qi - 1, 0), :, (qb - 1) * t:qb * t], p_ref[...],
                                preferred_element_type=F32)
        s_cur = s_ref[...]
        for b in range(qb):
            w = (qb - b) * t
            if b < qb - 1:
                qn = jnp.concatenate([qcat[(b + 1) * t:tq], qcat[tq + (b + 1) * t:2 * tq]], axis=0)
                s_next = scores(qb * qi + b + 1, qn)
            bias = bias_ref[qb - b, :, b * t:tq]
            p0 = jnp.exp2(s_cur[:, :w] + bias)
            p1 = jnp.exp2(s_cur[:, w:] + bias)
            l_ref[:, b * t:tq] += jnp.sum(p0, axis=0, keepdims=True)
            l_ref[:, tq + b * t:2 * tq] += jnp.sum(p1, axis=0, keepdims=True)
            pv = jnp.dot(vt_ref[qi, :, b * t:(b + 1) * t],
                         jnp.concatenate([p0, p1], axis=1).astype(BF16),
                         preferred_element_type=F32)
            acc_ref[:, b * t:tq] += pv[:, :w]
            acc_ref[:, tq + b * t:2 * tq] += pv[:, w:]
            if b < qb - 1:
                s_cur = s_next

        l_min = jnp.min(l_ref[...], keepdims=True)
        underflow = jnp.logical_not(l_min[0, 0] > ATT_MIN_DENOM)

        @pl.when(underflow)
        def _():
            m_ref[...] = jnp.full(m_ref.shape, NEG_INF, F32)
            l_ref[...] = jnp.zeros(l_ref.shape, F32)
            acc_ref[...] = jnp.zeros(acc_ref.shape, F32)

            def exact_unit(u, c):
                for b in range(qb):
                    blk = u * qb + b
                    s = jnp.concatenate(biased(scores(blk, qcat), blk), axis=1)
                    m_old = m_ref[...]
                    m_new = jnp.maximum(m_old, jnp.max(s, axis=0, keepdims=True))
                    alpha = jnp.exp2(m_old - m_new)
                    p = jnp.exp2(s - m_new)
                    l_ref[...] = alpha * l_ref[...] + jnp.sum(p, axis=0, keepdims=True)
                    acc_ref[...] = alpha * acc_ref[...] + jnp.dot(
                        vt_ref[u, :, b * t:(b + 1) * t], p.astype(BF16),
                        preferred_element_type=F32)
                    m_ref[...] = m_new
                return c

            lax.fori_loop(0, qi + 1, exact_unit, 0)

        on = acc_ref[...] / l_ref[...]
        o = on[:, :tq] - lam * on[:, tq:]
        ms = jnp.mean(o * o, axis=0, keepdims=True)
        o = o * lax.rsqrt(ms + SUBLN_EPS) * out_gain
        o_ref[pl.ds(pl.multiple_of(qi * tq, tq), tq), :] = o.T.astype(BF16)
        start_tile(load_qcat(jnp.minimum(qi + 1, n_tiles - 1)))
        return carry

    lax.fori_loop(0, n_tiles, tile_body, 0)


def _attn_call(layer_idx, q, k, vt, near_bias, c_far, bound, lq1, lk1, lq2, lk2, sg_col):
    L = q.shape[0]
    t, tq = ATT_T, ATT_TQ
    assert tq % t == 0 and L % tq == 0
    lam_init = 0.8 - 0.6 * math.exp(-0.3 * layer_idx)
    lvec = pl.BlockSpec((1, DIFF_HEAD_DIM), lambda h: (0, 0))
    scalar = pl.BlockSpec((1, 1, 1), lambda h: (h, 0, 0))
    return pl.pallas_call(
        functools.partial(_attn_kernel, lam_init=lam_init),
        grid=(N_ATT_HEADS,),
        in_specs=[
            pl.BlockSpec((L, V_HEAD_DIM), lambda h: (0, h)),
            pl.BlockSpec((L, V_HEAD_DIM), lambda h: (0, h)),
            pl.BlockSpec((L // tq, V_HEAD_DIM, tq), lambda h: (0, h, 0)),
            pl.BlockSpec((1, 2, t, t), lambda h: (h, 0, 0, 0)),
            scalar, scalar,
            lvec, lvec, lvec, lvec,
            pl.BlockSpec((V_HEAD_DIM, 1), lambda h: (0, 0)),
        ],
        out_specs=pl.BlockSpec((L, V_HEAD_DIM), lambda h: (0, h)),
        out_shape=jax.ShapeDtypeStruct((L, ATT_WIDTH), BF16),
        scratch_shapes=[
            pltpu.VMEM((V_HEAD_DIM, 2 * tq), F32),
            pltpu.VMEM((1, 2 * tq), F32),
            pltpu.VMEM((1, 2 * tq), F32),
            pltpu.VMEM((tq // t + 3, t, tq), F32),
            pltpu.VMEM((t, 2 * tq), F32),
            pltpu.VMEM((t, 2 * tq), BF16),
        ],
        compiler_params=_cparams(("arbitrary",)),
        name="diff_attn",
    )(q, k, vt, near_bias, c_far, bound, lq1, lk1, lq2, lk2, sg_col)


def _ssm_kernel(u_ref, un_ref, bm_ref, cm_ref, lam_ref, lamr_ref, d_ref, y_ref,
                bu_a_ref, bu_b_ref, carry_ref):
    ns = SSM_SLAB_GROUPS * SSM_STATE
    tt = SSM_TT
    strip = SSM_STRIP_ROWS
    n_strips = tt // strip
    bm = bm_ref[0]

    @pl.when(pl.program_id(1) == 0)
    def _():
        carry_ref[...] = jnp.zeros_like(carry_ref)
        bu_a_ref[...] = jnp.dot(u_ref[0:tt, :].astype(BF16), bm, preferred_element_type=F32)

    lr = jnp.broadcast_to(lam_ref[0, 0:1, :], (8, ns))
    li = jnp.broadcast_to(lam_ref[0, 1:2, :], (8, ns))
    pr = lamr_ref[0, 0:1, :]
    pi = lamr_ref[0, 1:2, :]
    row_id = lax.broadcasted_iota(jnp.int32, (8, ns), 0)
    z = jnp.zeros((8, ns), F32)

    def tile(cur_ref, nxt_ref, row0, next_u):
        def advance(r, sr, si, store):
            rows = pl.ds(8 * r, 8)
            nsr = lr * sr - li * si + cur_ref[rows, 0:ns]
            nsi = lr * si + li * sr + cur_ref[rows, ns:2 * ns]
            if store:
                cur_ref[rows, 0:ns] = nsr
                cur_ref[rows, ns:2 * ns] = nsi
            return nsr, nsi

        sr, si = z, z
        for k in range(n_strips):
            nxt_ref[pl.ds(k * strip, strip), :] = jnp.dot(
                next_u(k * strip, strip).astype(BF16), bm, preferred_element_type=F32)
            for r in range(k * strip // 8, (k + 1) * strip // 8):
                sr, si = advance(r, sr, si, False)
        er, ei = sr, si

        cr = carry_ref[0:1, 0:ns]
        ci = carry_ref[0:1, ns:2 * ns]
        sr, si = z, z
        for c in range(8):
            sr = jnp.where(row_id == c, cr, sr)
            si = jnp.where(row_id == c, ci, si)
            cr, ci = (pr * cr - pi * ci + er[c:c + 1, :], pr * ci + pi * cr + ei[c:c + 1, :])
        carry_ref[0:1, 0:ns] = cr
        carry_ref[0:1, ns:2 * ns] = ci

        def emit(k):
            rows = pl.ds(k * strip, strip)
            out_rows = pl.ds(row0 + k * strip, strip)
            y = (jnp.dot(cur_ref[rows, :].astype(BF16), cm_ref[0], preferred_element_type=F32)
                 + d_ref[0] * u_ref[out_rows, :])
            y_ref[out_rows, :] = jax.nn.gelu(y)

        for k in range(n_strips):
            if k > 0:
                emit(k - 1)
            for r in range(k * strip // 8, (k + 1) * strip // 8):
                sr, si = advance(r, sr, si, True)
        emit(n_strips - 1)

    tile(bu_a_ref, bu_b_ref, 0, lambda r, n: u_ref[pl.ds(tt + r, n), :])
    tile(bu_b_ref, bu_a_ref, tt, lambda r, n: un_ref[pl.ds(r, n), :])


def _ssm_call(u_perm, bmat, cmat, lam2, lamr2, d3):
    L = u_perm.shape[0]
    n_slab = N_SSM_GROUPS // SSM_SLAB_GROUPS
    n_tiles = L // SSM_TT
    assert n_tiles % 2 == 0
    ns = SSM_SLAB_GROUPS * SSM_STATE
    cw = SSM_SLAB_GROUPS * SSM_GROUP
    return pl.pallas_call(
        _ssm_kernel,
        grid=(n_slab, n_tiles // 2),
        in_specs=[
            pl.BlockSpec((2 * SSM_TT, cw), lambda s, p: (p, s)),
            pl.BlockSpec((SSM_TT, cw), lambda s, p: (jnp.minimum(2 * p + 2, n_tiles - 1), s)),
            pl.BlockSpec((1, cw, 2 * ns), lambda s, p: (s, 0, 0)),
            pl.BlockSpec((1, 2 * ns, cw), lambda s, p: (s, 0, 0)),
            pl.BlockSpec((1, 2, ns), lambda s, p: (s, 0, 0)),
            pl.BlockSpec((1, 2, ns), lambda s, p: (s, 0, 0)),
            pl.BlockSpec((1, 1, cw), lambda s, p: (s, 0, 0)),
        ],
        out_specs=pl.BlockSpec((2 * SSM_TT, cw), lambda s, p: (p, s)),
        out_shape=jax.ShapeDtypeStruct((L, SSM_WIDTH), F32),
        scratch_shapes=[
            pltpu.VMEM((SSM_TT, 2 * ns), F32),
            pltpu.VMEM((SSM_TT, 2 * ns), F32),
            pltpu.VMEM((1, 2 * ns), F32),
        ],
        compiler_params=_cparams(("arbitrary", "arbitrary")),
        name="s5_scan",
    )(u_perm, u_perm, bmat, cmat, lam2, lamr2, d3)


def _ssm_params(lam_re, lam_im, log_step, b_re, b_im, c_re, c_im, d):
    g, p, hc = N_SSM_GROUPS, SSM_STATE, SSM_GROUP
    sg = SSM_SLAB_GROUPS
    n_slab = g // sg
    lam = lax.complex(jnp.minimum(lam_re.astype(F32), -1e-4), lam_im.astype(F32))
    step = jnp.exp(log_step.astype(F32))[:, None]
    lam_bar = jnp.exp(lam * step)
    lam_bar_r = jnp.exp(lam * (step * (SSM_TT // 8)))
    b_bar = ((lam_bar - 1.0) / lam)[:, :, None] * lax.complex(b_re.astype(F32), b_im.astype(F32))
    eye = jnp.eye(sg, dtype=F32)

    def b_block(part):
        z = part.reshape(n_slab, sg, p, hc).transpose(0, 1, 3, 2)
        return (z[:, :, :, None, :] * eye[None, :, None, :, None]).reshape(n_slab, sg * hc, sg * p)

    def c_block(part):
        z = part.reshape(n_slab, sg, hc, p).transpose(0, 1, 3, 2)
        return (z[:, :, :, None, :] * eye[None, :, None, :, None]).reshape(n_slab, sg * p, sg * hc)

    bmat = jnp.concatenate([b_block(jnp.real(b_bar)), b_block(jnp.imag(b_bar))], axis=-1)
    cmat = jnp.concatenate([c_block(c_re.astype(F32)), c_block(-c_im.astype(F32))], axis=1)

    def rows(zc):
        return jnp.stack([jnp.real(zc).reshape(n_slab, sg * p),
                          jnp.imag(zc).reshape(n_slab, sg * p)], axis=1)

    d3 = d.astype(F32).reshape(n_slab, 1, sg * hc)
    return bmat.astype(BF16), cmat.astype(BF16), rows(lam_bar), rows(lam_bar_r), d3


def _out_kernel(att_ref, y_ref, gw_ref, gb_ref, woa_ref, wob_ref, x_ref, gt_ref, o_ref):
    y = y_ref[...]
    z = jnp.dot(y.astype(BF16), gw_ref[...], preferred_element_type=F32) + gb_ref[...]
    yg = (y * jax.nn.sigmoid(z)).astype(BF16)
    m = (jnp.dot(att_ref[...], woa_ref[...], preferred_element_type=F32)
         + jnp.dot(yg, wob_ref[...], preferred_element_type=F32))
    o_ref[...] = x_ref[...] + gt_ref[...] * m


def _out_call(layer, att, y, glu_w_bf, glu_b, w_out_bf, x, gt):
    L, d = x.shape
    aw, sw = ATT_WIDTH, SSM_WIDTH
    one = pl.Buffered(1)
    return pl.pallas_call(
        _out_kernel,
        grid=(L // OUT_TM,),
        in_specs=[
            pl.BlockSpec((OUT_TM, aw), lambda i: (i, 0)),
            pl.BlockSpec((OUT_TM, sw), lambda i: (i, 0)),
            pl.BlockSpec((None, sw, sw), lambda i: (layer, 0, 0), pipeline_mode=one),
            pl.BlockSpec((1, sw), lambda i: (0, 0)),
            pl.BlockSpec((None, aw, d), lambda i: (layer, 0, 0), pipeline_mode=one),
            pl.BlockSpec((None, sw, d), lambda i: (layer, 1, 0), pipeline_mode=one),
            pl.BlockSpec((OUT_TM, d), lambda i: (i, 0)),
            pl.BlockSpec((1, d), lambda i: (0, 0)),
        ],
        out_specs=pl.BlockSpec((OUT_TM, d), lambda i: (i, 0)),
        out_shape=jax.ShapeDtypeStruct((L, d), F32),
        compiler_params=_cparams(("arbitrary",)),
        name="out_proj",
    )(att, y, glu_w_bf, glu_b, w_out_bf, w_out_bf, x, gt)


def _t5_causal_buckets(dist):
    max_exact = N_BUCKETS // 2
    d = jnp.maximum(dist, 1).astype(F32)
    large = max_exact + (jnp.log(d / max_exact) / math.log(MAX_DISTANCE / max_exact)
                         * (N_BUCKETS - max_exact)).astype(jnp.int32)
    large = jnp.minimum(large, N_BUCKETS - 1)
    return jnp.where(dist < max_exact, dist, large)


def _near_bias(rel_bias):
    t = ATT_T
    assert t >= MAX_DISTANCE
    kk = jnp.arange(t, dtype=jnp.int32)[:, None]
    qq = jnp.arange(t, dtype=jnp.int32)[None, :]
    dist = jnp.stack([qq - kk, qq - kk + t], axis=0)
    bucket = _t5_causal_buckets(jnp.maximum(dist, 0))
    rb = rel_bias.astype(F32) * LOG2_E
    val = jnp.zeros((N_ATT_HEADS,) + dist.shape, F32)
    for b in range(N_BUCKETS):
        val = jnp.where((bucket == b)[None], rb[b][:, None, None, None], val)
    return jnp.where((dist >= 0)[None], val, NEG_INF)


def _chunk_interleave(a):
    L, w = a.shape
    return a.reshape(L // SSM_TT, 8, SSM_TT // 8, w).transpose(0, 2, 1, 3).reshape(L, w)


def _chunk_deinterleave(a):
    L, w = a.shape
    return a.reshape(L // SSM_TT, SSM_TT // 8, 8, w).transpose(0, 2, 1, 3).reshape(L, w)


def kernel(x, c, rel_bias, ada_w, ada_b, norm_g, ffn1_w_gate, ffn1_w_up, ffn1_w_down, ffn2_w_gate, ffn2_w_up, ffn2_w_down, w_in, w_out, q_norm_g, k_norm_g, lambda_q1, lambda_k1, lambda_q2, lambda_k2, subln_g, ssm_lambda_re, ssm_lambda_im, ssm_log_step, ssm_b_re, ssm_b_im, ssm_c_re, ssm_c_im, ssm_d, ssm_glu_w, ssm_glu_b):
    b, L, d = x.shape
    assert b == 1 and c.shape == (1, d)
    x2 = x.reshape(L, d)

    mod = _ada_call(c.reshape(d, 1), ada_w, ada_b.reshape(DEPTH, 1, N_COND * d))
    near_bias = _near_bias(rel_bias)
    c_far = (rel_bias.astype(F32)[N_BUCKETS - 1] * LOG2_E).reshape(N_ATT_HEADS, 1, 1)
    w_in_bf = w_in.astype(BF16)
    w_out_bf = w_out.astype(BF16)
    glu_w_bf = ssm_glu_w.astype(BF16)
    hd = DIFF_HEAD_DIM
    group_of = jnp.arange(PROJ_NORM_LANES) // hd
    gmat = jnp.where(group_of[:, None] == group_of[None, :], 1.0 / hd, 0.0).astype(BF16)
    n_rep = ATT_WIDTH // hd

    for i in range(DEPTH):
        m = mod[i]
        sh1, sc1, g1, sh2, sc2, g2, sh3, sc3, g3 = [m[:, n * d:(n + 1) * d] for n in range(N_COND)]
        x2 = _ffn_call(i, x2, norm_g[i, 0][None], sh1, sc1, g1,
                       ffn1_w_gate, ffn1_w_up, ffn1_w_down)

        qg = jnp.tile(q_norm_g[i].astype(F32), n_rep)[None] * (hd ** -0.5 * LOG2_E)
        kg = jnp.tile(k_norm_g[i].astype(F32), n_rep)[None]
        q, k, vt, u = _proj_call(i, x2, norm_g[i, 1][None], sh2, sc2, w_in_bf, gmat, qg, kg)
        bound = (DIFF_HEAD_DIM * jnp.max(jnp.abs(qg[0, :hd] * kg[0, :hd]))
                 + jnp.max(rel_bias.astype(F32), axis=0) * LOG2_E).reshape(N_ATT_HEADS, 1, 1)
        att = _attn_call(i, q, k, vt, near_bias, c_far, bound, lambda_q1[i][None], lambda_k1[i][None],
                         lambda_q2[i][None], lambda_k2[i][None], subln_g[i].reshape(V_HEAD_DIM, 1))
        bmat, cmat, lam2, lamr2, d3 = _ssm_params(
            ssm_lambda_re[i], ssm_lambda_im[i], ssm_log_step[i], ssm_b_re[i], ssm_b_im[i],
            ssm_c_re[i], ssm_c_im[i], ssm_d[i])
        y = _chunk_deinterleave(_ssm_call(_chunk_interleave(u), bmat, cmat, lam2, lamr2, d3))
        x2 = _out_call(i, att, y, glu_w_bf, ssm_glu_b[i][None], w_out_bf, x2, g2)

        x2 = _ffn_call(i, x2, norm_g[i, 2][None], sh3, sc3, g3,
                       ffn2_w_gate, ffn2_w_up, ffn2_w_down)
    return x2.reshape(b, L, d)
```

```python
import functools
import math

import jax
import jax.numpy as jnp
from jax import lax
from jax.experimental import pallas as pl
from jax.experimental.pallas import tpu as pltpu

D_MODEL = 2048
SEQ = 8192
DEPTH = 2
ATT_WIDTH = 1024
SSM_WIDTH = 1024
DIFF_HEAD_DIM = 64
V_HEAD_DIM = 128
N_ATT_HEADS = 8
SSM_GROUP = 16
N_SSM_GROUPS = 64
SSM_STATE = 64
IN_WIDTH = 4096
D_FF = 5632
N_BUCKETS = 32
MAX_DISTANCE = 128
N_COND = 9
NORM_EPS = 1e-6
SUBLN_EPS = 1e-5
NEG_INF = -1e30
LOG2_E = math.log2(math.e)

F32 = jnp.float32
BF16 = jnp.bfloat16

VMEM_LIMIT_BYTES = 60 * 1024 * 1024

ADA_TN = 2048
FFN_TM = 1024
FFN_TF = 512
FFN_NORM_ROWS = 256
PROJ_TM = 512
PROJ_NORM_LANES = 128
ATT_T = 256
ATT_TQ = 1024
ATT_MIN_DENOM = 2.0 ** -40
SSM_TT = 1024
SSM_SLAB_GROUPS = 8
SSM_STRIP_ROWS = 128
OUT_TM = 512


def _cparams(sem):
    return pltpu.CompilerParams(dimension_semantics=sem, vmem_limit_bytes=VMEM_LIMIT_BYTES)


def _ada_kernel(c_ref, w_ref, b_ref, o_ref):
    c = c_ref[...]
    cs = c * jax.nn.sigmoid(c)
    o_ref[0] = jnp.sum(w_ref[0] * cs, axis=0, keepdims=True) + b_ref[0]


def _ada_call(c_col, ada_w, ada_b3):
    depth, d, n = ada_w.shape
    return pl.pallas_call(
        _ada_kernel,
        grid=(depth, n // ADA_TN),
        in_specs=[
            pl.BlockSpec((d, 1), lambda l, j: (0, 0)),
            pl.BlockSpec((1, d, ADA_TN), lambda l, j: (l, 0, j)),
            pl.BlockSpec((1, 1, ADA_TN), lambda l, j: (l, 0, j)),
        ],
        out_specs=pl.BlockSpec((1, 1, ADA_TN), lambda l, j: (l, 0, j)),
        out_shape=jax.ShapeDtypeStruct((depth, 1, n), F32),
        compiler_params=_cparams(("arbitrary", "arbitrary")),
        name="adaln",
    )(c_col, ada_w, ada_b3)


def _cond_spec(layer, n, d):
    return pl.BlockSpec((None, 1, d), lambda *_: (layer, 0, n))


def _gain_spec(layer, sub, d):
    return pl.BlockSpec((None, 1, d), lambda *_: (layer * 3 + sub, 0, 0))


def _norm_mod(x, g, sc, sh):
    ms = jnp.mean(x * x, axis=-1, keepdims=True)
    return (x * lax.rsqrt(ms + NORM_EPS) * g) * (1.0 + sc) + sh


def _ffn_kernel(x_ref, ng_ref, sh_ref, sc_ref, gt_ref, wg_ref, wu_ref, wd_ref, o_ref, h_ref):
    j = pl.program_id(1)

    def weights():
        return (wg_ref[...].astype(BF16), wu_ref[...].astype(BF16), wd_ref[...].astype(BF16))

    def hidden_update(h, wg, wu, wd):
        g = jnp.dot(h, wg, preferred_element_type=F32)
        u = jnp.dot(h, wu, preferred_element_type=F32)
        a = (g * jax.nn.sigmoid(g) * u).astype(BF16)
        return jnp.dot(a, wd, preferred_element_type=F32)

    @pl.when(j == 0)
    def _():
        w = weights()
        for r in range(0, FFN_TM, FFN_NORM_ROWS):
            rows = pl.ds(r, FFN_NORM_ROWS)
            h = _norm_mod(x_ref[rows, :], ng_ref[...], sc_ref[...], sh_ref[...]).astype(BF16)
            h_ref[rows, :] = h
            o_ref[rows, :] = hidden_update(h, *w)

    @pl.when(j > 0)
    def _():
        o_ref[...] += hidden_update(h_ref[...], *weights())

    @pl.when(j == pl.num_programs(1) - 1)
    def _():
        o_ref[...] = x_ref[...] + (0.5 * gt_ref[...]) * o_ref[...]


def _ffn_call(layer, sub, x, ng3, mod, wg, wu, wd):
    L, d = x.shape
    dff = wg.shape[-1]
    return pl.pallas_call(
        _ffn_kernel,
        grid=(L // FFN_TM, dff // FFN_TF),
        in_specs=[
            pl.BlockSpec((FFN_TM, d), lambda i, j: (i, 0), pipeline_mode=pl.Buffered(2 if layer else 1)),
            _gain_spec(layer, sub, d), _cond_spec(layer, 3 * sub, d),
            _cond_spec(layer, 3 * sub + 1, d), _cond_spec(layer, 3 * sub + 2, d),
            pl.BlockSpec((None, d, FFN_TF), lambda i, j: (layer, 0, j)),
            pl.BlockSpec((None, d, FFN_TF), lambda i, j: (layer, 0, j)),
            pl.BlockSpec((None, FFN_TF, d), lambda i, j: (layer, j, 0)),
        ],
        out_specs=pl.BlockSpec((FFN_TM, d), lambda i, j: (i, 0), pipeline_mode=pl.Buffered(1 if layer else 2)),
        out_shape=jax.ShapeDtypeStruct((L, d), F32),
        scratch_shapes=[pltpu.VMEM((FFN_TM, d), BF16)],
        compiler_params=_cparams(("arbitrary", "arbitrary")),
        name="ffn",
    )(x, ng3, mod, mod, mod, wg, wu, wd)


def _proj_kernel(x_ref, ng_ref, sh_ref, sc_ref, w_ref, gm_ref, qg_ref, kg_ref,
                 q_ref, k_ref, vt_ref, u_ref):
    h = _norm_mod(x_ref[...], ng_ref[...], sc_ref[...], sh_ref[...]).astype(BF16)
    aw = ATT_WIDTH

    def head_norm(z, g):
        sq = (z * z).astype(BF16)
        lanes = gm_ref.shape[0]
        ms = jnp.concatenate(
            [jnp.dot(sq[:, c:c + lanes], gm_ref[...], preferred_element_type=F32)
             for c in range(0, aw, lanes)], axis=1)
        return z * lax.rsqrt(ms + NORM_EPS) * g

    q = jnp.dot(h, w_ref[:, 0:aw], preferred_element_type=F32)
    q_ref[...] = head_norm(q, qg_ref[...]).astype(BF16)
    k = jnp.dot(h, w_ref[:, aw:2 * aw], preferred_element_type=F32)
    k_ref[...] = head_norm(k, kg_ref[...]).astype(BF16)
    v = jnp.dot(h, w_ref[:, 2 * aw:3 * aw], preferred_element_type=F32)
    vt_ref[0] = v.T.astype(BF16)
    u_ref[...] = jnp.dot(h, w_ref[:, 3 * aw:4 * aw], preferred_element_type=F32)


def _proj_call(layer, x, ng3, mod, w_in_bf, gmat, qg, kg):
    L, d = x.shape
    aw = ATT_WIDTH
    unit = ATT_TQ
    per_unit = unit // PROJ_TM
    vec_a = pl.BlockSpec((1, aw), lambda i: (0, 0))
    return pl.pallas_call(
        _proj_kernel,
        grid=(L // PROJ_TM,),
        in_specs=[
            pl.BlockSpec((PROJ_TM, d), lambda i: (i, 0)),
            _gain_spec(layer, 1, d), _cond_spec(layer, 3, d), _cond_spec(layer, 4, d),
            pl.BlockSpec((None, d, IN_WIDTH), lambda i: (layer, 0, 0),
                         pipeline_mode=pl.Buffered(1)),
            pl.BlockSpec((PROJ_NORM_LANES, PROJ_NORM_LANES), lambda i: (0, 0)),
            vec_a, vec_a,
        ],
        out_specs=[
            pl.BlockSpec((PROJ_TM, aw), lambda i: (i, 0)),
            pl.BlockSpec((PROJ_TM, aw), lambda i: (i, 0)),
            pl.BlockSpec((1, aw, PROJ_TM), lambda i: (i // per_unit, 0, i % per_unit)),
            pl.BlockSpec((PROJ_TM, SSM_WIDTH), lambda i: (i, 0)),
        ],
        out_shape=[
            jax.ShapeDtypeStruct((L, aw), BF16),
            jax.ShapeDtypeStruct((L, aw), BF16),
            jax.ShapeDtypeStruct((L // unit, aw, unit), BF16),
            jax.ShapeDtypeStruct((L, SSM_WIDTH), F32),
        ],
        compiler_params=_cparams(("arbitrary",)),
        name="in_proj",
    )(x, ng3, mod, mod, w_in_bf, gmat, qg, kg)


def _attn_kernel(q_ref, k_ref, vt_ref, nbias_ref, cfar_ref, bound_ref, lq1_ref, lk1_ref, lq2_ref,
                 lk2_ref, sg_ref, o_ref, acc_ref, m_ref, l_ref, bias_ref, s_ref, p_ref, *, lam_init):
    t = ATT_T
    tq = ATT_TQ
    qb = tq // t
    n_tiles = q_ref.shape[0] // tq

    bound = bound_ref[0]
    tiles = {2: jnp.broadcast_to(cfar_ref[0] - bound, (t, t)), 1: nbias_ref[0, 1] - bound,
             0: nbias_ref[0, 0] - bound, -1: jnp.full((t, t), NEG_INF, F32)}
    for d in range(-qb, 3):
        for a in range(qb):
            bias_ref[d + qb, :, a * t:(a + 1) * t] = tiles[max(min(d + a, 2), -1)]

    lam = (jnp.exp(jnp.sum(lq1_ref[...] * lk1_ref[...], keepdims=True))
           - jnp.exp(jnp.sum(lq2_ref[...] * lk2_ref[...], keepdims=True)) + lam_init)
    out_gain = sg_ref[...] * (1.0 - lam_init)

    def load_qcat(qi):
        q = q_ref[pl.ds(pl.multiple_of(qi * tq, tq), tq), :]
        lane = lax.broadcasted_iota(jnp.int32, q.shape, 1)
        zero = jnp.zeros_like(q)
        return jnp.concatenate([jnp.where(lane < DIFF_HEAD_DIM, q, zero),
                                jnp.where(lane >= DIFF_HEAD_DIM, q, zero)], axis=0)

    def scores(blk, qmat):
        off = pl.multiple_of(blk * t, t)
        return lax.dot_general(k_ref[pl.ds(off, t), :], qmat, (((1,), (1,)), ((), ())),
                               preferred_element_type=F32)

    def start_tile(qcat):
        acc_ref[...] = jnp.zeros(acc_ref.shape, F32)
        l_ref[...] = jnp.zeros(l_ref.shape, F32)
        p_ref[...] = jnp.zeros(p_ref.shape, BF16)
        s_ref[...] = scores(0, qcat)

    start_tile(load_qcat(0))

    def tile_body(qi, carry):
        qcat = load_qcat(qi)

        def biased(s, blk):
            bias = bias_ref[jnp.clip(qb * qi - blk, -qb, 2) + qb]
            return s[:, :tq] + bias, s[:, tq:] + bias

        def fast_unit(u, c):
            s_cur = s_ref[...]
            pv = jnp.dot(vt_ref[jnp.maximum(u - 1, 0), :, (qb - 1) * t:qb * t], p_ref[...],
                         preferred_element_type=F32)
            lsum = jnp.zeros((1, 2 * tq), F32)
            for b in range(qb):
                blk = u * qb + b
                s_next = scores(blk + 1, qcat)
                s0, s1 = biased(s_cur, blk)
                p = jnp.concatenate([jnp.exp2(s0), jnp.exp2(s1)], axis=1)
                lsum = lsum + jnp.sum(p, axis=0, keepdims=True)
                if b < qb - 1:
                    pv = pv + jnp.dot(vt_ref[u, :, b * t:(b + 1) * t], p.astype(BF16),
                                      preferred_element_type=F32)
                else:
                    p_ref[...] = p.astype(BF16)
                s_cur = s_next
            s_ref[...] = s_cur
            acc_ref[...] += pv
            l_ref[...] += lsum
            return c

        lax.fori_loop(0, qi, fast_unit, 0)

        acc_ref[...] += jnp.dot(vt_ref[jnp.maximum(qi - 1, 0), :, (qb - 1) * t:qb * t], p_ref[...],
                                preferred_element_type=F32)
        s_cur = s_ref[...]
        for b in range(qb):
            w = (qb - b) * t
            if b < qb - 1:
                qn = jnp.concatenate([qcat[(b + 1) * t:tq], qcat[tq + (b + 1) * t:2 * tq]], axis=0)
                s_next = scores(qb * qi + b + 1, qn)
            bias = bias_ref[qb - b, :, b * t:tq]
            p0 = jnp.exp2(s_cur[:, :w] + bias)
            p1 = jnp.exp2(s_cur[:, w:] + bias)
            l_ref[:, b * t:tq] += jnp.sum(p0, axis=0, keepdims=True)
            l_ref[:, tq + b * t:2 * tq] += jnp.sum(p1, axis=0, keepdims=True)
            pv = jnp.dot(vt_ref[qi, :, b * t:(b + 1) * t],
                         jnp.concatenate([p0, p1], axis=1).astype(BF16),
                         preferred_element_type=F32)
            acc_ref[:, b * t:tq] += pv[:, :w]
            acc_ref[:, tq + b * t:2 * tq] += pv[:, w:]
            if b < qb - 1:
                s_cur = s_next

        l_min = jnp.min(l_ref[...], keepdims=True)
        underflow = jnp.logical_not(l_min[0, 0] > ATT_MIN_DENOM)

        @pl.when(underflow)
        def _():
            m_ref[...] = jnp.full(m_ref.shape, NEG_INF, F32)
            l_ref[...] = jnp.zeros(l_ref.shape, F32)
            acc_ref[...] = jnp.zeros(acc_ref.shape, F32)

            def exact_unit(u, c):
                for b in range(qb):
                    blk = u * qb + b
                    s = jnp.concatenate(biased(scores(blk, qcat), blk), axis=1)
                    m_old = m_ref[...]
                    m_new = jnp.maximum(m_old, jnp.max(s, axis=0, keepdims=True))
                    alpha = jnp.exp2(m_old - m_new)
                    p = jnp.exp2(s - m_new)
                    l_ref[...] = alpha * l_ref[...] + jnp.sum(p, axis=0, keepdims=True)
                    acc_ref[...] = alpha * acc_ref[...] + jnp.dot(
                        vt_ref[u, :, b * t:(b + 1) * t], p.astype(BF16),
                        preferred_element_type=F32)
                    m_ref[...] = m_new
                return c

            lax.fori_loop(0, qi + 1, exact_unit, 0)

        on = acc_ref[...] / l_ref[...]
        o = on[:, :tq] - lam * on[:, tq:]
        ms = jnp.mean(o * o, axis=0, keepdims=True)
        o = o * lax.rsqrt(ms + SUBLN_EPS) * out_gain
        o_ref[pl.ds(pl.multiple_of(qi * tq, tq), tq), :] = o.T.astype(BF16)
        start_tile(load_qcat(jnp.minimum(qi + 1, n_tiles - 1)))
        return carry

    lax.fori_loop(0, n_tiles, tile_body, 0)


def _attn_call(layer_idx, q, k, vt, near_bias, c_far, bound, lq1, lk1, lq2, lk2, sg_col):
    L = q.shape[0]
    t, tq = ATT_T, ATT_TQ
    assert tq % t == 0 and L % tq == 0
    lam_init = 0.8 - 0.6 * math.exp(-0.3 * layer_idx)
    lvec = pl.BlockSpec((1, DIFF_HEAD_DIM), lambda h: (0, 0))
    scalar = pl.BlockSpec((1, 1, 1), lambda h: (h, 0, 0))
    return pl.pallas_call(
        functools.partial(_attn_kernel, lam_init=lam_init),
        grid=(N_ATT_HEADS,),
        in_specs=[
            pl.BlockSpec((L, V_HEAD_DIM), lambda h: (0, h)),
            pl.BlockSpec((L, V_HEAD_DIM), lambda h: (0, h)),
            pl.BlockSpec((L // tq, V_HEAD_DIM, tq), lambda h: (0, h, 0)),
            pl.BlockSpec((1, 2, t, t), lambda h: (h, 0, 0, 0)),
            scalar, scalar,
            lvec, lvec, lvec, lvec,
            pl.BlockSpec((V_HEAD_DIM, 1), lambda h: (0, 0)),
        ],
        out_specs=pl.BlockSpec((L, V_HEAD_DIM), lambda h: (0, h)),
        out_shape=jax.ShapeDtypeStruct((L, ATT_WIDTH), BF16),
        scratch_shapes=[
            pltpu.VMEM((V_HEAD_DIM, 2 * tq), F32),
            pltpu.VMEM((1, 2 * tq), F32),
            pltpu.VMEM((1, 2 * tq), F32),
            pltpu.VMEM((tq // t + 3, t, tq), F32),
            pltpu.VMEM((t, 2 * tq), F32),
            pltpu.VMEM((t, 2 * tq), BF16),
        ],
        compiler_params=_cparams(("arbitrary",)),
        name="diff_attn",
    )(q, k, vt, near_bias, c_far, bound, lq1, lk1, lq2, lk2, sg_col)


def _ssm_kernel(u_ref, un_ref, bm_ref, cm_ref, lam_ref, lamr_ref, d_ref, y_ref,
                bu_a_ref, bu_b_ref, carry_ref):
    ns = SSM_SLAB_GROUPS * SSM_STATE
    tt = SSM_TT
    strip = SSM_STRIP_ROWS
    n_strips = tt // strip
    bm = bm_ref[0]

    @pl.when(pl.program_id(1) == 0)
    def _():
        carry_ref[...] = jnp.zeros_like(carry_ref)
        bu_a_ref[...] = jnp.dot(u_ref[0:tt, :].astype(BF16), bm, preferred_element_type=F32)

    lr = jnp.broadcast_to(lam_ref[0, 0:1, :], (8, ns))
    li = jnp.broadcast_to(lam_ref[0, 1:2, :], (8, ns))
    pr = lamr_ref[0, 0:1, :]
    pi = lamr_ref[0, 1:2, :]
    row_id = lax.broadcasted_iota(jnp.int32, (8, ns), 0)
    z = jnp.zeros((8, ns), F32)

    def tile(cur_ref, nxt_ref, row0, next_u):
        def advance(r, sr, si, store):
            rows = pl.ds(8 * r, 8)
            nsr = lr * sr - li * si + cur_ref[rows, 0:ns]
            nsi = lr * si + li * sr + cur_ref[rows, ns:2 * ns]
            if store:
                cur_ref[rows, 0:ns] = nsr
                cur_ref[rows, ns:2 * ns] = nsi
            return nsr, nsi

        sr, si = z, z
        for k in range(n_strips):
            nxt_ref[pl.ds(k * strip, strip), :] = jnp.dot(
                next_u(k * strip, strip).astype(BF16), bm, preferred_element_type=F32)
            for r in range(k * strip // 8, (k + 1) * strip // 8):
                sr, si = advance(r, sr, si, False)
        er, ei = sr, si

        cr = carry_ref[0:1, 0:ns]
        ci = carry_ref[0:1, ns:2 * ns]
        sr, si = z, z
        for c in range(8):
            sr = jnp.where(row_id == c, cr, sr)
            si = jnp.where(row_id == c, ci, si)
            cr, ci = (pr * cr - pi * ci + er[c:c + 1, :], pr * ci + pi * cr + ei[c:c + 1, :])
        carry_ref[0:1, 0:ns] = cr
        carry_ref[0:1, ns:2 * ns] = ci

        def emit(k):
            rows = pl.ds(k * strip, strip)
            out_rows = pl.ds(row0 + k * strip, strip)
            y = (jnp.dot(cur_ref[rows, :].astype(BF16), cm_ref[0], preferred_element_type=F32)
                 + d_ref[0] * u_ref[out_rows, :])
            y_ref[out_rows, :] = jax.nn.gelu(y)

        for k in range(n_strips):
            if k > 0:
                emit(k - 1)
            for r in range(k * strip // 8, (k + 1) * strip // 8):
                sr, si = advance(r, sr, si, True)
        emit(n_strips - 1)

    tile(bu_a_ref, bu_b_ref, 0, lambda r, n: u_ref[pl.ds(tt + r, n), :])
    tile(bu_b_ref, bu_a_ref, tt, lambda r, n: un_ref[pl.ds(r, n), :])


def _ssm_call(u_perm, bmat, cmat, lam2, lamr2, d3):
    L = u_perm.shape[0]
    n_slab = N_SSM_GROUPS // SSM_SLAB_GROUPS
    n_tiles = L // SSM_TT
    assert n_tiles % 2 == 0
    ns = SSM_SLAB_GROUPS * SSM_STATE
    cw = SSM_SLAB_GROUPS * SSM_GROUP
    return pl.pallas_call(
        _ssm_kernel,
        grid=(n_slab, n_tiles // 2),
        in_specs=[
            pl.BlockSpec((2 * SSM_TT, cw), lambda s, p: (p, s)),
            pl.BlockSpec((SSM_TT, cw), lambda s, p: (jnp.minimum(2 * p + 2, n_tiles - 1), s)),
            pl.BlockSpec((1, cw, 2 * ns), lambda s, p: (s, 0, 0)),
            pl.BlockSpec((1, 2 * ns, cw), lambda s, p: (s, 0, 0)),
            pl.BlockSpec((1, 2, ns), lambda s, p: (s, 0, 0)),
            pl.BlockSpec((1, 2, ns), lambda s, p: (s, 0, 0)),
            pl.BlockSpec((1, 1, cw), lambda s, p: (s, 0, 0)),
        ],
        out_specs=pl.BlockSpec((2 * SSM_TT, cw), lambda s, p: (p, s)),
        out_shape=jax.ShapeDtypeStruct((L, SSM_WIDTH), F32),
        scratch_shapes=[
            pltpu.VMEM((SSM_TT, 2 * ns), F32),
            pltpu.VMEM((SSM_TT, 2 * ns), F32),
            pltpu.VMEM((1, 2 * ns), F32),
        ],
        compiler_params=_cparams(("arbitrary", "arbitrary")),
        name="s5_scan",
    )(u_perm, u_perm, bmat, cmat, lam2, lamr2, d3)


def _ssm_params(lam_re, lam_im, log_step, b_re, b_im, c_re, c_im, d):
    g, p, hc = N_SSM_GROUPS, SSM_STATE, SSM_GROUP
    sg = SSM_SLAB_GROUPS
    n_slab = g // sg
    lam = lax.complex(jnp.minimum(lam_re.astype(F32), -1e-4), lam_im.astype(F32))
    step = jnp.exp(log_step.astype(F32))[:, None]
    lam_bar = jnp.exp(lam * step)
    lam_bar_r = jnp.exp(lam * (step * (SSM_TT // 8)))
    b_bar = ((lam_bar - 1.0) / lam)[:, :, None] * lax.complex(b_re.astype(F32), b_im.astype(F32))
    eye = jnp.eye(sg, dtype=F32)

    def b_block(part):
        z = part.reshape(n_slab, sg, p, hc).transpose(0, 1, 3, 2)
        return (z[:, :, :, None, :] * eye[None, :, None, :, None]).reshape(n_slab, sg * hc, sg * p)

    def c_block(part):
        z = part.reshape(n_slab, sg, hc, p).transpose(0, 1, 3, 2)
        return (z[:, :, :, None, :] * eye[None, :, None, :, None]).reshape(n_slab, sg * p, sg * hc)

    bmat = jnp.concatenate([b_block(jnp.real(b_bar)), b_block(jnp.imag(b_bar))], axis=-1)
    cmat = jnp.concatenate([c_block(c_re.astype(F32)), c_block(-c_im.astype(F32))], axis=1)

    def rows(zc):
        return jnp.stack([jnp.real(zc).reshape(n_slab, sg * p),
                          jnp.imag(zc).reshape(n_slab, sg * p)], axis=1)

    d3 = d.astype(F32).reshape(n_slab, 1, sg * hc)
    return bmat.astype(BF16), cmat.astype(BF16), rows(lam_bar), rows(lam_bar_r), d3


def _out_kernel(att_ref, y_ref, gw_ref, gb_ref, woa_ref, wob_ref, x_ref, gt_ref, o_ref,
                gw_bf_ref, wo_bf_ref):
    aw = ATT_WIDTH

    @pl.when(pl.program_id(0) == 0)
    def _():
        gw_bf_ref[...] = gw_ref[...].astype(BF16)
        wo_bf_ref[0:aw, :] = woa_ref[...].astype(BF16)
        wo_bf_ref[aw:, :] = wob_ref[...].astype(BF16)

    y = y_ref[...]
    z = jnp.dot(y.astype(BF16), gw_bf_ref[...], preferred_element_type=F32) + gb_ref[...]
    yg = (y * jax.nn.sigmoid(z)).astype(BF16)
    m = (jnp.dot(att_ref[...], wo_bf_ref[0:aw, :], preferred_element_type=F32)
         + jnp.dot(yg, wo_bf_ref[aw:, :], preferred_element_type=F32))
    o_ref[...] = x_ref[...] + gt_ref[...] * m


def _out_call(layer, att, y, glu_w, glu_b, w_out, x, mod):
    L, d = x.shape
    aw, sw = ATT_WIDTH, SSM_WIDTH
    one = pl.Buffered(1)
    return pl.pallas_call(
        _out_kernel,
        grid=(L // OUT_TM,),
        in_specs=[
            pl.BlockSpec((OUT_TM, aw), lambda i: (i, 0)),
            pl.BlockSpec((OUT_TM, sw), lambda i: (i, 0)),
            pl.BlockSpec((None, sw, sw), lambda i: (layer, 0, 0), pipeline_mode=one),
            pl.BlockSpec((1, sw), lambda i: (0, 0)),
            pl.BlockSpec((None, aw, d), lambda i: (layer, 0, 0), pipeline_mode=one),
            pl.BlockSpec((None, sw, d), lambda i: (layer, 1, 0), pipeline_mode=one),
            pl.BlockSpec((OUT_TM, d), lambda i: (i, 0)),
            _cond_spec(layer, 5, d),
        ],
        out_specs=pl.BlockSpec((OUT_TM, d), lambda i: (i, 0)),
        out_shape=jax.ShapeDtypeStruct((L, d), F32),
        scratch_shapes=[pltpu.VMEM((sw, sw), BF16), pltpu.VMEM((aw + sw, d), BF16)],
        compiler_params=_cparams(("arbitrary",)),
        name="out_proj",
    )(att, y, glu_w, glu_b, w_out, w_out, x, mod)


def _t5_causal_buckets(dist):
    max_exact = N_BUCKETS // 2
    d = jnp.maximum(dist, 1).astype(F32)
    large = max_exact + (jnp.log(d / max_exact) / math.log(MAX_DISTANCE / max_exact)
                         * (N_BUCKETS - max_exact)).astype(jnp.int32)
    large = jnp.minimum(large, N_BUCKETS - 1)
    return jnp.where(dist < max_exact, dist, large)


def _near_bias(rel_bias):
    t = ATT_T
    assert t >= MAX_DISTANCE
    kk = jnp.arange(t, dtype=jnp.int32)[:, None]
    qq = jnp.arange(t, dtype=jnp.int32)[None, :]
    dist = jnp.stack([qq - kk, qq - kk + t], axis=0)
    bucket = _t5_causal_buckets(jnp.maximum(dist, 0))
    rb = rel_bias.astype(F32) * LOG2_E
    val = jnp.zeros((N_ATT_HEADS,) + dist.shape, F32)
    for b in range(N_BUCKETS):
        val = jnp.where((bucket == b)[None], rb[b][:, None, None, None], val)
    return jnp.where((dist >= 0)[None], val, NEG_INF)


def _chunk_interleave(a):
    L, w = a.shape
    return a.reshape(L // SSM_TT, 8, SSM_TT // 8, w).transpose(0, 2, 1, 3).reshape(L, w)


def _chunk_deinterleave(a):
    L, w = a.shape
    return a.reshape(L // SSM_TT, SSM_TT // 8, 8, w).transpose(0, 2, 1, 3).reshape(L, w)


def kernel(x, c, rel_bias, ada_w, ada_b, norm_g, ffn1_w_gate, ffn1_w_up, ffn1_w_down, ffn2_w_gate, ffn2_w_up, ffn2_w_down, w_in, w_out, q_norm_g, k_norm_g, lambda_q1, lambda_k1, lambda_q2, lambda_k2, subln_g, ssm_lambda_re, ssm_lambda_im, ssm_log_step, ssm_b_re, ssm_b_im, ssm_c_re, ssm_c_im, ssm_d, ssm_glu_w, ssm_glu_b):
    b, L, d = x.shape
    assert b == 1 and c.shape == (1, d)
    x2 = x.reshape(L, d)

    mod = _ada_call(c.reshape(d, 1), ada_w, ada_b.reshape(DEPTH, 1, N_COND * d))
    near_bias = _near_bias(rel_bias)
    c_far = (rel_bias.astype(F32)[N_BUCKETS - 1] * LOG2_E).reshape(N_ATT_HEADS, 1, 1)
    w_in_bf = w_in.astype(BF16)
    hd = DIFF_HEAD_DIM
    group_of = jnp.arange(PROJ_NORM_LANES) // hd
    gmat = jnp.where(group_of[:, None] == group_of[None, :], 1.0 / hd, 0.0).astype(BF16)
    n_rep = ATT_WIDTH // hd

    ng3 = norm_g.reshape(DEPTH * 3, 1, d)

    for i in range(DEPTH):
        x2 = _ffn_call(i, 0, x2, ng3, mod, ffn1_w_gate, ffn1_w_up, ffn1_w_down)

        qg = jnp.tile(q_norm_g[i].astype(F32), n_rep)[None] * (hd ** -0.5 * LOG2_E)
        kg = jnp.tile(k_norm_g[i].astype(F32), n_rep)[None]
        q, k, vt, u = _proj_call(i, x2, ng3, mod, w_in_bf, gmat, qg, kg)
        bound = (DIFF_HEAD_DIM * jnp.max(jnp.abs(qg[0, :hd] * kg[0, :hd]))
                 + jnp.max(rel_bias.astype(F32), axis=0) * LOG2_E).reshape(N_ATT_HEADS, 1, 1)
        att = _attn_call(i, q, k, vt, near_bias, c_far, bound, lambda_q1[i][None], lambda_k1[i][None],
                         lambda_q2[i][None], lambda_k2[i][None], subln_g[i].reshape(V_HEAD_DIM, 1))
        bmat, cmat, lam2, lamr2, d3 = _ssm_params(
            ssm_lambda_re[i], ssm_lambda_im[i], ssm_log_step[i], ssm_b_re[i], ssm_b_im[i],
            ssm_c_re[i], ssm_c_im[i], ssm_d[i])
        y = _chunk_deinterleave(_ssm_call(_chunk_interleave(u), bmat, cmat, lam2, lamr2, d3))
        x2 = _out_call(i, att, y, ssm_glu_w, ssm_glu_b[i][None], w_out, x2, mod)

        x2 = _ffn_call(i, 2, x2, ng3, mod, ffn2_w_gate, ffn2_w_up, ffn2_w_down)
    return x2.reshape(b, L, d)
```

```python
import functools
import math

import jax
import jax.numpy as jnp
from jax import lax
from jax.experimental import pallas as pl
from jax.experimental.pallas import tpu as pltpu

D_MODEL = 2048
SEQ = 8192
DEPTH = 2
ATT_WIDTH = 1024
SSM_WIDTH = 1024
DIFF_HEAD_DIM = 64
V_HEAD_DIM = 128
N_ATT_HEADS = 8
SSM_GROUP = 16
N_SSM_GROUPS = 64
SSM_STATE = 64
IN_WIDTH = 4096
D_FF = 5632
N_BUCKETS = 32
MAX_DISTANCE = 128
N_COND = 9
NORM_EPS = 1e-6
SUBLN_EPS = 1e-5
NEG_INF = -1e30
LOG2_E = math.log2(math.e)

F32 = jnp.float32
BF16 = jnp.bfloat16

VMEM_LIMIT_BYTES = 60 * 1024 * 1024

ADA_TN = 2048
FFN_TM = 1024
FFN_TF = 512
FFN_NORM_ROWS = 256
PROJ_TM = 512
PROJ_NORM_LANES = 128
ATT_T = 256
ATT_TQ = 1024
ATT_MIN_DENOM = 2.0 ** -40
SSM_TT = 1024
SSM_SLAB_GROUPS = 8
SSM_STRIP_ROWS = 128
OUT_TM = 512


def _cparams(sem):
    return pltpu.CompilerParams(dimension_semantics=sem, vmem_limit_bytes=VMEM_LIMIT_BYTES)


def _ada_kernel(c_ref, w_ref, b_ref, o_ref):
    c = c_ref[...]
    cs = c * jax.nn.sigmoid(c)
    o_ref[0] = jnp.sum(w_ref[0] * cs, axis=0, keepdims=True) + b_ref[0]


def _ada_call(c_col, ada_w, ada_b3):
    depth, d, n = ada_w.shape
    return pl.pallas_call(
        _ada_kernel,
        grid=(depth, n // ADA_TN),
        in_specs=[
            pl.BlockSpec((d, 1), lambda l, j: (0, 0)),
            pl.BlockSpec((1, d, ADA_TN), lambda l, j: (l, 0, j)),
            pl.BlockSpec((1, 1, ADA_TN), lambda l, j: (l, 0, j)),
        ],
        out_specs=pl.BlockSpec((1, 1, ADA_TN), lambda l, j: (l, 0, j)),
        out_shape=jax.ShapeDtypeStruct((depth, 1, n), F32),
        compiler_params=_cparams(("arbitrary", "arbitrary")),
        name="adaln",
    )(c_col, ada_w, ada_b3)


def _cond_spec(layer, n, d):
    return pl.BlockSpec((None, 1, d), lambda *_: (layer, 0, n))


def _gain_spec(layer, sub, d):
    return pl.BlockSpec((None, 1, d), lambda *_: (layer * 3 + sub, 0, 0))


def _norm_mod(x, g, sc, sh):
    ms = jnp.mean(x * x, axis=-1, keepdims=True)
    return (x * lax.rsqrt(ms + NORM_EPS) * g) * (1.0 + sc) + sh


def _ffn_kernel(x_ref, ng_ref, sh_ref, sc_ref, gt_ref, wg_ref, wu_ref, wd_ref, o_ref, h_ref):
    j = pl.program_id(1)

    def weights():
        return (wg_ref[...].astype(BF16), wu_ref[...].astype(BF16), wd_ref[...].astype(BF16))

    def hidden_update(h, wg, wu, wd):
        g = jnp.dot(h, wg, preferred_element_type=F32)
        u = jnp.dot(h, wu, preferred_element_type=F32)
        a = (g * jax.nn.sigmoid(g) * u).astype(BF16)
        return jnp.dot(a, wd, preferred_element_type=F32)

    @pl.when(j == 0)
    def _():
        w = weights()
        for r in range(0, FFN_TM, FFN_NORM_ROWS):
            rows = pl.ds(r, FFN_NORM_ROWS)
            h = _norm_mod(x_ref[rows, :], ng_ref[...], sc_ref[...], sh_ref[...]).astype(BF16)
            h_ref[rows, :] = h
            o_ref[rows, :] = hidden_update(h, *w)

    @pl.when(j > 0)
    def _():
        o_ref[...] += hidden_update(h_ref[...], *weights())

    @pl.when(j == pl.num_programs(1) - 1)
    def _():
        o_ref[...] = x_ref[...] + (0.5 * gt_ref[...]) * o_ref[...]


def _ffn_call(layer, sub, x, ng3, mod, wg, wu, wd):
    L, d = x.shape
    dff = wg.shape[-1]
    return pl.pallas_call(
        _ffn_kernel,
        grid=(L // FFN_TM, dff // FFN_TF),
        in_specs=[
            pl.BlockSpec((FFN_TM, d), lambda i, j: (i, 0)),
            _gain_spec(layer, sub, d), _cond_spec(layer, 3 * sub, d),
            _cond_spec(layer, 3 * sub + 1, d), _cond_spec(layer, 3 * sub + 2, d),
            pl.BlockSpec((None, d, FFN_TF), lambda i, j: (layer, 0, j)),
            pl.BlockSpec((None, d, FFN_TF), lambda i, j: (layer, 0, j)),
            pl.BlockSpec((None, FFN_TF, d), lambda i, j: (layer, j, 0)),
        ],
        out_specs=pl.BlockSpec((FFN_TM, d), lambda i, j: (i, 0), pipeline_mode=pl.Buffered(1)),
        out_shape=jax.ShapeDtypeStruct((L, d), F32),
        scratch_shapes=[pltpu.VMEM((FFN_TM, d), BF16)],
        compiler_params=_cparams(("arbitrary", "arbitrary")),
        name="ffn",
    )(x, ng3, mod, mod, mod, wg, wu, wd)


def _proj_kernel(x_ref, ng_ref, sh_ref, sc_ref, w_ref, gm_ref, qg_ref, kg_ref,
                 q_ref, k_ref, vt_ref, u_ref):
    h = _norm_mod(x_ref[...], ng_ref[...], sc_ref[...], sh_ref[...]).astype(BF16)
    aw = ATT_WIDTH

    def head_norm(z, g):
        sq = (z * z).astype(BF16)
        lanes = gm_ref.shape[0]
        ms = jnp.concatenate(
            [jnp.dot(sq[:, c:c + lanes], gm_ref[...], preferred_element_type=F32)
             for c in range(0, aw, lanes)], axis=1)
        return z * lax.rsqrt(ms + NORM_EPS) * g

    q = jnp.dot(h, w_ref[:, 0:aw], preferred_element_type=F32)
    q_ref[...] = head_norm(q, qg_ref[...]).astype(BF16)
    k = jnp.dot(h, w_ref[:, aw:2 * aw], preferred_element_type=F32)
    k_ref[...] = head_norm(k, kg_ref[...]).astype(BF16)
    v = jnp.dot(h, w_ref[:, 2 * aw:3 * aw], preferred_element_type=F32)
    vt_ref[0] = v.T.astype(BF16)
    u_ref[...] = jnp.dot(h, w_ref[:, 3 * aw:4 * aw], preferred_element_type=F32)


def _proj_call(layer, x, ng3, mod, w_in_bf, gmat, qg, kg):
    L, d = x.shape
    aw = ATT_WIDTH
    unit = ATT_TQ
    per_unit = unit // PROJ_TM
    vec_a = pl.BlockSpec((1, aw), lambda i: (0, 0))
    return pl.pallas_call(
        _proj_kernel,
        grid=(L // PROJ_TM,),
        in_specs=[
            pl.BlockSpec((PROJ_TM, d), lambda i: (i, 0)),
            _gain_spec(layer, 1, d), _cond_spec(layer, 3, d), _cond_spec(layer, 4, d),
            pl.BlockSpec((None, d, IN_WIDTH), lambda i: (layer, 0, 0),
                         pipeline_mode=pl.Buffered(1)),
            pl.BlockSpec((PROJ_NORM_LANES, PROJ_NORM_LANES), lambda i: (0, 0)),
            vec_a, vec_a,
        ],
        out_specs=[
            pl.BlockSpec((PROJ_TM, aw), lambda i: (i, 0)),
            pl.BlockSpec((PROJ_TM, aw), lambda i: (i, 0)),
            pl.BlockSpec((1, aw, PROJ_TM), lambda i: (i // per_unit, 0, i % per_unit)),
            pl.BlockSpec((PROJ_TM, SSM_WIDTH), lambda i: (i, 0)),
        ],
        out_shape=[
            jax.ShapeDtypeStruct((L, aw), BF16),
            jax.ShapeDtypeStruct((L, aw), BF16),
            jax.ShapeDtypeStruct((L // unit, aw, unit), BF16),
            jax.ShapeDtypeStruct((L, SSM_WIDTH), F32),
        ],
        compiler_params=_cparams(("arbitrary",)),
        name="in_proj",
    )(x, ng3, mod, mod, w_in_bf, gmat, qg, kg)


def _attn_kernel(q_ref, k_ref, vt_ref, nbias_ref, cfar_ref, bound_ref, lq1_ref, lk1_ref, lq2_ref,
                 lk2_ref, sg_ref, o_ref, acc_ref, m_ref, l_ref, bias_ref, s_ref, p_ref, *, lam_init):
    t = ATT_T
    tq = ATT_TQ
    qb = tq // t
    n_tiles = q_ref.shape[0] // tq

    bound = bound_ref[0]
    tiles = {2: jnp.broadcast_to(cfar_ref[0] - bound, (t, t)), 1: nbias_ref[0, 1] - bound,
             0: nbias_ref[0, 0] - bound, -1: jnp.full((t, t), NEG_INF, F32)}
    for d in range(-qb, 3):
        for a in range(qb):
            bias_ref[d + qb, :, a * t:(a + 1) * t] = tiles[max(min(d + a, 2), -1)]

    lam = (jnp.exp(jnp.sum(lq1_ref[...] * lk1_ref[...], keepdims=True))
           - jnp.exp(jnp.sum(lq2_ref[...] * lk2_ref[...], keepdims=True)) + lam_init)
    out_gain = sg_ref[...] * (1.0 - lam_init)

    def load_qcat(qi):
        q = q_ref[pl.ds(pl.multiple_of(qi * tq, tq), tq), :]
        lane = lax.broadcasted_iota(jnp.int32, q.shape, 1)
        zero = jnp.zeros_like(q)
        return jnp.concatenate([jnp.where(lane < DIFF_HEAD_DIM, q, zero),
                                jnp.where(lane >= DIFF_HEAD_DIM, q, zero)], axis=0)

    def scores(blk, qmat):
        off = pl.multiple_of(blk * t, t)
        return lax.dot_general(k_ref[pl.ds(off, t), :], qmat, (((1,), (1,)), ((), ())),
                               preferred_element_type=F32)

    def start_tile(qcat):
        acc_ref[...] = jnp.zeros(acc_ref.shape, F32)
        l_ref[...] = jnp.zeros(l_ref.shape, F32)
        p_ref[...] = jnp.zeros(p_ref.shape, BF16)
        s_ref[...] = scores(0, qcat)

    start_tile(load_qcat(0))

    def tile_body(qi, carry):
        qcat = load_qcat(qi)

        def biased(s, blk):
            bias = bias_ref[jnp.clip(qb * qi - blk, -qb, 2) + qb]
            return s[:, :tq] + bias, s[:, tq:] + bias

        def fast_unit(u, c):
            s_cur = s_ref[...]
            pv = jnp.dot(vt_ref[jnp.maximum(u - 1, 0), :, (qb - 1) * t:qb * t], p_ref[...],
                         preferred_element_type=F32)
            lsum = jnp.zeros((1, 2 * tq), F32)
            for b in range(qb):
                blk = u * qb + b
                s_next = scores(blk + 1, qcat)
                s0, s1 = biased(s_cur, blk)
                p = jnp.concatenate([jnp.exp2(s0), jnp.exp2(s1)], axis=1)
                lsum = lsum + jnp.sum(p, axis=0, keepdims=True)
                if b < qb - 1:
                    pv = pv + jnp.dot(vt_ref[u, :, b * t:(b + 1) * t], p.astype(BF16),
                                      preferred_element_type=F32)
                else:
                    p_ref[...] = p.astype(BF16)
                s_cur = s_next
            s_ref[...] = s_cur
            acc_ref[...] += pv
            l_ref[...] += lsum
            return c

        lax.fori_loop(0, qi, fast_unit, 0)

        acc_ref[...] += jnp.dot(vt_ref[jnp.maximum(qi - 1, 0), :, (qb - 1) * t:qb * t], p_ref[...],
                                preferred_element_type=F32)
        s_cur = s_ref[...]
        for b in range(qb):
            w = (qb - b) * t
            if b < qb - 1:
                qn = jnp.concatenate([qcat[(b + 1) * t:tq], qcat[tq + (b + 1) * t:2 * tq]], axis=0)
                s_next = scores(qb * qi + b + 1, qn)
            bias = bias_ref[qb - b, :, b * t:tq]
            p0 = jnp.exp2(s_cur[:, :w] + bias)
            p1 = jnp.exp2(s_cur[:, w:] + bias)
            l_ref[:, b * t:tq] += jnp.sum(p0, axis=0, keepdims=True)
            l_ref[:, tq + b * t:2 * tq] += jnp.sum(p1, axis=0, keepdims=True)
            pv = jnp.dot(vt_ref[qi, :, b * t:(b + 1) * t],
                         jnp.concatenate([p0, p1], axis=1).astype(BF16),
                         preferred_element_type=F32)
            acc_ref[:, b * t:tq] += pv[:, :w]
            acc_ref[:, tq + b * t:2 * tq] += pv[:, w:]
            if b < qb - 1:
                s_cur = s_next

        l_min = jnp.min(l_ref[...], keepdims=True)
        underflow = jnp.logical_not(l_min[0, 0] > ATT_MIN_DENOM)

        @pl.when(underflow)
        def _():
            m_ref[...] = jnp.full(m_ref.shape, NEG_INF, F32)
            l_ref[...] = jnp.zeros(l_ref.shape, F32)
            acc_ref[...] = jnp.zeros(acc_ref.shape, F32)

            def exact_unit(u, c):
                for b in range(qb):
                    blk = u * qb + b
                    s = jnp.concatenate(biased(scores(blk, qcat), blk), axis=1)
                    m_old = m_ref[...]
                    m_new = jnp.maximum(m_old, jnp.max(s, axis=0, keepdims=True))
                    alpha = jnp.exp2(m_old - m_new)
                    p = jnp.exp2(s - m_new)
                    l_ref[...] = alpha * l_ref[...] + jnp.sum(p, axis=0, keepdims=True)
                    acc_ref[...] = alpha * acc_ref[...] + jnp.dot(
                        vt_ref[u, :, b * t:(b + 1) * t], p.astype(BF16),
                        preferred_element_type=F32)
                    m_ref[...] = m_new
                return c

            lax.fori_loop(0, qi + 1, exact_unit, 0)

        on = acc_ref[...] / l_ref[...]
        o = on[:, :tq] - lam * on[:, tq:]
        ms = jnp.mean(o * o, axis=0, keepdims=True)
        o = o * lax.rsqrt(ms + SUBLN_EPS) * out_gain
        o_ref[pl.ds(pl.multiple_of(qi * tq, tq), tq), :] = o.T.astype(BF16)
        start_tile(load_qcat(jnp.minimum(qi + 1, n_tiles - 1)))
        return carry

    lax.fori_loop(0, n_tiles, tile_body, 0)


def _attn_call(layer_idx, q, k, vt, near_bias, c_far, bound, lq1, lk1, lq2, lk2, sg_col):
    L = q.shape[0]
    t, tq = ATT_T, ATT_TQ
    assert tq % t == 0 and L % tq == 0
    lam_init = 0.8 - 0.6 * math.exp(-0.3 * layer_idx)
    lvec = pl.BlockSpec((1, DIFF_HEAD_DIM), lambda h: (0, 0))
    scalar = pl.BlockSpec((1, 1, 1), lambda h: (h, 0, 0))
    return pl.pallas_call(
        functools.partial(_attn_kernel, lam_init=lam_init),
        grid=(N_ATT_HEADS,),
        in_specs=[
            pl.BlockSpec((L, V_HEAD_DIM), lambda h: (0, h)),
            pl.BlockSpec((L, V_HEAD_DIM), lambda h: (0, h)),
            pl.BlockSpec((L // tq, V_HEAD_DIM, tq), lambda h: (0, h, 0)),
            pl.BlockSpec((1, 2, t, t), lambda h: (h, 0, 0, 0)),
            scalar, scalar,
            lvec, lvec, lvec, lvec,
            pl.BlockSpec((V_HEAD_DIM, 1), lambda h: (0, 0)),
        ],
        out_specs=pl.BlockSpec((L, V_HEAD_DIM), lambda h: (0, h)),
        out_shape=jax.ShapeDtypeStruct((L, ATT_WIDTH), BF16),
        scratch_shapes=[
            pltpu.VMEM((V_HEAD_DIM, 2 * tq), F32),
            pltpu.VMEM((1, 2 * tq), F32),
            pltpu.VMEM((1, 2 * tq), F32),
            pltpu.VMEM((tq // t + 3, t, tq), F32),
            pltpu.VMEM((t, 2 * tq), F32),
            pltpu.VMEM((t, 2 * tq), BF16),
        ],
        compiler_params=_cparams(("arbitrary",)),
        name="diff_attn",
    )(q, k, vt, near_bias, c_far, bound, lq1, lk1, lq2, lk2, sg_col)


def _ssm_kernel(u_ref, un_ref, bm_ref, cm_ref, lam_ref, lamr_ref, d_ref, y_ref,
                bu_a_ref, bu_b_ref, carry_ref):
    ns = SSM_SLAB_GROUPS * SSM_STATE
    tt = SSM_TT
    strip = SSM_STRIP_ROWS
    n_strips = tt // strip
    bm = bm_ref[0]

    @pl.when(pl.program_id(1) == 0)
    def _():
        carry_ref[...] = jnp.zeros_like(carry_ref)
        bu_a_ref[...] = jnp.dot(u_ref[0:tt, :].astype(BF16), bm, preferred_element_type=F32)

    lr = jnp.broadcast_to(lam_ref[0, 0:1, :], (8, ns))
    li = jnp.broadcast_to(lam_ref[0, 1:2, :], (8, ns))
    pr = lamr_ref[0, 0:1, :]
    pi = lamr_ref[0, 1:2, :]
    row_id = lax.broadcasted_iota(jnp.int32, (8, ns), 0)
    z = jnp.zeros((8, ns), F32)

    def tile(cur_ref, nxt_ref, row0, next_u):
        def advance(r, sr, si, store):
            rows = pl.ds(8 * r, 8)
            nsr = lr * sr - li * si + cur_ref[rows, 0:ns]
            nsi = lr * si + li * sr + cur_ref[rows, ns:2 * ns]
            if store:
                cur_ref[rows, 0:ns] = nsr
                cur_ref[rows, ns:2 * ns] = nsi
            return nsr, nsi

        sr, si = z, z
        for k in range(n_strips):
            nxt_ref[pl.ds(k * strip, strip), :] = jnp.dot(
                next_u(k * strip, strip).astype(BF16), bm, preferred_element_type=F32)
            for r in range(k * strip // 8, (k + 1) * strip // 8):
                sr, si = advance(r, sr, si, False)
        er, ei = sr, si

        cr = carry_ref[0:1, 0:ns]
        ci = carry_ref[0:1, ns:2 * ns]
        sr, si = z, z
        for c in range(8):
            sr = jnp.where(row_id == c, cr, sr)
            si = jnp.where(row_id == c, ci, si)
            cr, ci = (pr * cr - pi * ci + er[c:c + 1, :], pr * ci + pi * cr + ei[c:c + 1, :])
        carry_ref[0:1, 0:ns] = cr
        carry_ref[0:1, ns:2 * ns] = ci

        def emit(k):
            rows = pl.ds(k * strip, strip)
            out_rows = pl.ds(row0 + k * strip, strip)
            y = (jnp.dot(cur_ref[rows, :].astype(BF16), cm_ref[0], preferred_element_type=F32)
                 + d_ref[0] * u_ref[out_rows, :])
            y_ref[out_rows, :] = y

        for k in range(n_strips):
            if k > 0:
                emit(k - 1)
            for r in range(k * strip // 8, (k + 1) * strip // 8):
                sr, si = advance(r, sr, si, True)
        emit(n_strips - 1)

    tile(bu_a_ref, bu_b_ref, 0, lambda r, n: u_ref[pl.ds(tt + r, n), :])
    tile(bu_b_ref, bu_a_ref, tt, lambda r, n: un_ref[pl.ds(r, n), :])


def _ssm_call(u_perm, bmat, cmat, lam2, lamr2, d3):
    L = u_perm.shape[0]
    n_slab = N_SSM_GROUPS // SSM_SLAB_GROUPS
    n_tiles = L // SSM_TT
    assert n_tiles % 2 == 0
    ns = SSM_SLAB_GROUPS * SSM_STATE
    cw = SSM_SLAB_GROUPS * SSM_GROUP
    return pl.pallas_call(
        _ssm_kernel,
        grid=(n_slab, n_tiles // 2),
        in_specs=[
            pl.BlockSpec((2 * SSM_TT, cw), lambda s, p: (p, s)),
            pl.BlockSpec((SSM_TT, cw), lambda s, p: (jnp.minimum(2 * p + 2, n_tiles - 1), s)),
            pl.BlockSpec((1, cw, 2 * ns), lambda s, p: (s, 0, 0)),
            pl.BlockSpec((1, 2 * ns, cw), lambda s, p: (s, 0, 0)),
            pl.BlockSpec((1, 2, ns), lambda s, p: (s, 0, 0)),
            pl.BlockSpec((1, 2, ns), lambda s, p: (s, 0, 0)),
            pl.BlockSpec((1, 1, cw), lambda s, p: (s, 0, 0)),
        ],
        out_specs=pl.BlockSpec((2 * SSM_TT, cw), lambda s, p: (p, s)),
        out_shape=jax.ShapeDtypeStruct((L, SSM_WIDTH), F32),
        scratch_shapes=[
            pltpu.VMEM((SSM_TT, 2 * ns), F32),
            pltpu.VMEM((SSM_TT, 2 * ns), F32),
            pltpu.VMEM((1, 2 * ns), F32),
        ],
        compiler_params=_cparams(("arbitrary", "arbitrary")),
        name="s5_scan",
    )(u_perm, u_perm, bmat, cmat, lam2, lamr2, d3)


def _ssm_params(lam_re, lam_im, log_step, b_re, b_im, c_re, c_im, d):
    g, p, hc = N_SSM_GROUPS, SSM_STATE, SSM_GROUP
    sg = SSM_SLAB_GROUPS
    n_slab = g // sg
    lam = lax.complex(jnp.minimum(lam_re.astype(F32), -1e-4), lam_im.astype(F32))
    step = jnp.exp(log_step.astype(F32))[:, None]
    lam_bar = jnp.exp(lam * step)
    lam_bar_r = jnp.exp(lam * (step * (SSM_TT // 8)))
    b_bar = ((lam_bar - 1.0) / lam)[:, :, None] * lax.complex(b_re.astype(F32), b_im.astype(F32))
    eye = jnp.eye(sg, dtype=F32)

    def b_block(part):
        z = part.reshape(n_slab, sg, p, hc).transpose(0, 1, 3, 2)
        return (z[:, :, :, None, :] * eye[None, :, None, :, None]).reshape(n_slab, sg * hc, sg * p)

    def c_block(part):
        z = part.reshape(n_slab, sg, hc, p).transpose(0, 1, 3, 2)
        return (z[:, :, :, None, :] * eye[None, :, None, :, None]).reshape(n_slab, sg * p, sg * hc)

    bmat = jnp.concatenate([b_block(jnp.real(b_bar)), b_block(jnp.imag(b_bar))], axis=-1)
    cmat = jnp.concatenate([c_block(c_re.astype(F32)), c_block(-c_im.astype(F32))], axis=1)

    def rows(zc):
        return jnp.stack([jnp.real(zc).reshape(n_slab, sg * p),
                          jnp.imag(zc).reshape(n_slab, sg * p)], axis=1)

    d3 = d.astype(F32).reshape(n_slab, 1, sg * hc)
    return bmat.astype(BF16), cmat.astype(BF16), rows(lam_bar), rows(lam_bar_r), d3


def _out_kernel(att_ref, y_ref, gw_ref, gb_ref, woa_ref, wob_ref, x_ref, gt_ref, o_ref,
                gw_bf_ref, wo_bf_ref):
    aw = ATT_WIDTH

    @pl.when(pl.program_id(0) == 0)
    def _():
        gw_bf_ref[...] = gw_ref[...].astype(BF16)
        wo_bf_ref[0:aw, :] = woa_ref[...].astype(BF16)
        wo_bf_ref[aw:, :] = wob_ref[...].astype(BF16)

    y = jax.nn.gelu(y_ref[...])
    z = jnp.dot(y.astype(BF16), gw_bf_ref[...], preferred_element_type=F32) + gb_ref[...]
    yg = (y * jax.nn.sigmoid(z)).astype(BF16)
    m = (jnp.dot(att_ref[...], wo_bf_ref[0:aw, :], preferred_element_type=F32)
         + jnp.dot(yg, wo_bf_ref[aw:, :], preferred_element_type=F32))
    o_ref[...] = x_ref[...] + gt_ref[...] * m


def _out_call(layer, att, y, glu_w, glu_b, w_out, x, mod):
    L, d = x.shape
    aw, sw = ATT_WIDTH, SSM_WIDTH
    one = pl.Buffered(1)
    return pl.pallas_call(
        _out_kernel,
        grid=(L // OUT_TM,),
        in_specs=[
            pl.BlockSpec((OUT_TM, aw), lambda i: (i, 0)),
            pl.BlockSpec((OUT_TM, sw), lambda i: (i, 0)),
            pl.BlockSpec((None, sw, sw), lambda i: (layer, 0, 0), pipeline_mode=one),
            pl.BlockSpec((1, sw), lambda i: (0, 0)),
            pl.BlockSpec((None, aw, d), lambda i: (layer, 0, 0), pipeline_mode=one),
            pl.BlockSpec((None, sw, d), lambda i: (layer, 1, 0), pipeline_mode=one),
            pl.BlockSpec((OUT_TM, d), lambda i: (i, 0)),
            _cond_spec(layer, 5, d),
        ],
        out_specs=pl.BlockSpec((OUT_TM, d), lambda i: (i, 0)),
        out_shape=jax.ShapeDtypeStruct((L, d), F32),
        scratch_shapes=[pltpu.VMEM((sw, sw), BF16), pltpu.VMEM((aw + sw, d), BF16)],
        compiler_params=_cparams(("arbitrary",)),
        name="out_proj",
    )(att, y, glu_w, glu_b, w_out, w_out, x, mod)


def _t5_causal_buckets(dist):
    max_exact = N_BUCKETS // 2
    d = jnp.maximum(dist, 1).astype(F32)
    large = max_exact + (jnp.log(d / max_exact) / math.log(MAX_DISTANCE / max_exact)
                         * (N_BUCKETS - max_exact)).astype(jnp.int32)
    large = jnp.minimum(large, N_BUCKETS - 1)
    return jnp.where(dist < max_exact, dist, large)


def _near_bias(rel_bias):
    t = ATT_T
    assert t >= MAX_DISTANCE
    kk = jnp.arange(t, dtype=jnp.int32)[:, None]
    qq = jnp.arange(t, dtype=jnp.int32)[None, :]
    dist = jnp.stack([qq - kk, qq - kk + t], axis=0)
    bucket = _t5_causal_buckets(jnp.maximum(dist, 0))
    rb = rel_bias.astype(F32) * LOG2_E
    val = jnp.zeros((N_ATT_HEADS,) + dist.shape, F32)
    for b in range(N_BUCKETS):
        val = jnp.where((bucket == b)[None], rb[b][:, None, None, None], val)
    return jnp.where((dist >= 0)[None], val, NEG_INF)


def _chunk_interleave(a):
    L, w = a.shape
    return a.reshape(L // SSM_TT, 8, SSM_TT // 8, w).transpose(0, 2, 1, 3).reshape(L, w)


def _chunk_deinterleave(a):
    L, w = a.shape
    return a.reshape(L // SSM_TT, SSM_TT // 8, 8, w).transpose(0, 2, 1, 3).reshape(L, w)


def kernel(x, c, rel_bias, ada_w, ada_b, norm_g, ffn1_w_gate, ffn1_w_up, ffn1_w_down, ffn2_w_gate, ffn2_w_up, ffn2_w_down, w_in, w_out, q_norm_g, k_norm_g, lambda_q1, lambda_k1, lambda_q2, lambda_k2, subln_g, ssm_lambda_re, ssm_lambda_im, ssm_log_step, ssm_b_re, ssm_b_im, ssm_c_re, ssm_c_im, ssm_d, ssm_glu_w, ssm_glu_b):
    b, L, d = x.shape
    assert b == 1 and c.shape == (1, d)
    x2 = x.reshape(L, d)

    mod = _ada_call(c.reshape(d, 1), ada_w, ada_b.reshape(DEPTH, 1, N_COND * d))
    near_bias = _near_bias(rel_bias)
    c_far = (rel_bias.astype(F32)[N_BUCKETS - 1] * LOG2_E).reshape(N_ATT_HEADS, 1, 1)
    w_in_bf = w_in.astype(BF16)
    hd = DIFF_HEAD_DIM
    group_of = jnp.arange(PROJ_NORM_LANES) // hd
    gmat = jnp.where(group_of[:, None] == group_of[None, :], 1.0 / hd, 0.0).astype(BF16)
    n_rep = ATT_WIDTH // hd

    ng3 = norm_g.reshape(DEPTH * 3, 1, d)

    for i in range(DEPTH):
        x2 = _ffn_call(i, 0, x2, ng3, mod, ffn1_w_gate, ffn1_w_up, ffn1_w_down)

        qg = jnp.tile(q_norm_g[i].astype(F32), n_rep)[None] * (hd ** -0.5 * LOG2_E)
        kg = jnp.tile(k_norm_g[i].astype(F32), n_rep)[None]
        q, k, vt, u = _proj_call(i, x2, ng3, mod, w_in_bf, gmat, qg, kg)
        bound = (DIFF_HEAD_DIM * jnp.max(jnp.abs(qg[0, :hd] * kg[0, :hd]))
                 + jnp.max(rel_bias.astype(F32), axis=0) * LOG2_E).reshape(N_ATT_HEADS, 1, 1)
        att = _attn_call(i, q, k, vt, near_bias, c_far, bound, lambda_q1[i][None], lambda_k1[i][None],
                         lambda_q2[i][None], lambda_k2[i][None], subln_g[i].reshape(V_HEAD_DIM, 1))
        bmat, cmat, lam2, lamr2, d3 = _ssm_params(
            ssm_lambda_re[i], ssm_lambda_im[i], ssm_log_step[i], ssm_b_re[i], ssm_b_im[i],
            ssm_c_re[i], ssm_c_im[i], ssm_d[i])
        y = _chunk_deinterleave(_ssm_call(_chunk_interleave(u), bmat, cmat, lam2, lamr2, d3))
        x2 = _out_call(i, att, y, ssm_glu_w, ssm_glu_b[i][None], w_out, x2, mod)

        x2 = _ffn_call(i, 2, x2, ng3, mod, ffn2_w_gate, ffn2_w_up, ffn2_w_down)
    return x2.reshape(b, L, d)
```

```python
import functools
import math

import jax
import jax.numpy as jnp
from jax import lax
from jax.experimental import pallas as pl
from jax.experimental.pallas import tpu as pltpu

D_MODEL = 2048
SEQ = 8192
DEPTH = 2
ATT_WIDTH = 1024
SSM_WIDTH = 1024
DIFF_HEAD_DIM = 64
V_HEAD_DIM = 128
N_ATT_HEADS = 8
SSM_GROUP = 16
N_SSM_GROUPS = 64
SSM_STATE = 64
IN_WIDTH = 4096
D_FF = 5632
N_BUCKETS = 32
MAX_DISTANCE = 128
N_COND = 9
NORM_EPS = 1e-6
SUBLN_EPS = 1e-5
NEG_INF = -1e30
LOG2_E = math.log2(math.e)

F32 = jnp.float32
BF16 = jnp.bfloat16

VMEM_LIMIT_BYTES = 60 * 1024 * 1024

ADA_TN = 2048
FFN_TM = 1024
FFN_TF = 512
FFN_NORM_ROWS = 512
PROJ_TM = 512
PROJ_NORM_LANES = 128
ATT_T = 256
ATT_TQ = 1024
ATT_MIN_DENOM = 2.0 ** -40
SSM_TT = 1024
SSM_SLAB_GROUPS = 8
SSM_STRIP_ROWS = 128
OUT_TM = 512


def _cparams(sem):
    return pltpu.CompilerParams(dimension_semantics=sem, vmem_limit_bytes=VMEM_LIMIT_BYTES)


def _ada_kernel(c_ref, w_ref, b_ref, o_ref):
    c = c_ref[...]
    cs = c * jax.nn.sigmoid(c)
    o_ref[0] = jnp.sum(w_ref[0] * cs, axis=0, keepdims=True) + b_ref[0]


def _ada_call(c_col, ada_w, ada_b3):
    depth, d, n = ada_w.shape
    return pl.pallas_call(
        _ada_kernel,
        grid=(depth, n // ADA_TN),
        in_specs=[
            pl.BlockSpec((d, 1), lambda l, j: (0, 0)),
            pl.BlockSpec((1, d, ADA_TN), lambda l, j: (l, 0, j)),
            pl.BlockSpec((1, 1, ADA_TN), lambda l, j: (l, 0, j)),
        ],
        out_specs=pl.BlockSpec((1, 1, ADA_TN), lambda l, j: (l, 0, j)),
        out_shape=jax.ShapeDtypeStruct((depth, 1, n), F32),
        compiler_params=_cparams(("arbitrary", "arbitrary")),
        name="adaln",
    )(c_col, ada_w, ada_b3)


def _cond_spec(layer, n, d):
    return pl.BlockSpec((None, 1, d), lambda *_: (layer, 0, n))


def _gain_spec(layer, sub, d):
    return pl.BlockSpec((None, 1, d), lambda *_: (layer * 3 + sub, 0, 0))


def _norm_mod(x, g, sc, sh):
    ms = jnp.mean(x * x, axis=-1, keepdims=True)
    return (x * lax.rsqrt(ms + NORM_EPS) * g) * (1.0 + sc) + sh


def _ffn_kernel(x_ref, ng_ref, sh_ref, sc_ref, gt_ref, wg_ref, wu_ref, wd_ref, o_ref, h_ref):
    j = pl.program_id(1)

    def weights():
        return (wg_ref[...].astype(BF16), wu_ref[...].astype(BF16), wd_ref[...].astype(BF16))

    def hidden_update(h, wg, wu, wd):
        g = jnp.dot(h, wg, preferred_element_type=F32)
        u = jnp.dot(h, wu, preferred_element_type=F32)
        a = (g * jax.nn.sigmoid(g) * u).astype(BF16)
        return jnp.dot(a, wd, preferred_element_type=F32)

    half_gate = 0.5 * gt_ref[...]

    @pl.when(j == 0)
    def _():
        w = weights()
        for r in range(0, FFN_TM, FFN_NORM_ROWS):
            rows = pl.ds(r, FFN_NORM_ROWS)
            x = x_ref[rows, :]
            h = _norm_mod(x, ng_ref[...], sc_ref[...], sh_ref[...]).astype(BF16)
            h_ref[rows, :] = h
            o_ref[rows, :] = x + half_gate * hidden_update(h, *w)

    @pl.when(j > 0)
    def _():
        o_ref[...] += half_gate * hidden_update(h_ref[...], *weights())


def _ffn_call(layer, sub, x, ng3, mod, wg, wu, wd):
    L, d = x.shape
    dff = wg.shape[-1]
    return pl.pallas_call(
        _ffn_kernel,
        grid=(L // FFN_TM, dff // FFN_TF),
        in_specs=[
            pl.BlockSpec((FFN_TM, d), lambda i, j: (i, 0)),
            _gain_spec(layer, sub, d), _cond_spec(layer, 3 * sub, d),
            _cond_spec(layer, 3 * sub + 1, d), _cond_spec(layer, 3 * sub + 2, d),
            pl.BlockSpec((None, d, FFN_TF), lambda i, j: (layer, 0, j)),
            pl.BlockSpec((None, d, FFN_TF), lambda i, j: (layer, 0, j)),
            pl.BlockSpec((None, FFN_TF, d), lambda i, j: (layer, j, 0)),
        ],
        out_specs=pl.BlockSpec((FFN_TM, d), lambda i, j: (i, 0), pipeline_mode=pl.Buffered(1)),
        out_shape=jax.ShapeDtypeStruct((L, d), F32),
        scratch_shapes=[pltpu.VMEM((FFN_TM, d), BF16)],
        compiler_params=_cparams(("arbitrary", "arbitrary")),
        name="ffn",
    )(x, ng3, mod, mod, mod, wg, wu, wd)


def _proj_kernel(x_ref, ng_ref, sh_ref, sc_ref, w_ref, gm_ref, qg_ref, kg_ref,
                 q_ref, k_ref, vt_ref, u_ref):
    h = _norm_mod(x_ref[...], ng_ref[...], sc_ref[...], sh_ref[...]).astype(BF16)
    aw = ATT_WIDTH

    def head_norm(z, g):
        sq = (z * z).astype(BF16)
        lanes = gm_ref.shape[0]
        ms = jnp.concatenate(
            [jnp.dot(sq[:, c:c + lanes], gm_ref[...], preferred_element_type=F32)
             for c in range(0, aw, lanes)], axis=1)
        return z * lax.rsqrt(ms + NORM_EPS) * g

    q = jnp.dot(h, w_ref[:, 0:aw], preferred_element_type=F32)
    q_ref[...] = head_norm(q, qg_ref[...]).astype(BF16)
    k = jnp.dot(h, w_ref[:, aw:2 * aw], preferred_element_type=F32)
    k_ref[...] = head_norm(k, kg_ref[...]).astype(BF16)
    v = jnp.dot(h, w_ref[:, 2 * aw:3 * aw], preferred_element_type=F32)
    vt_ref[0] = v.T.astype(BF16)
    u_ref[...] = jnp.dot(h, w_ref[:, 3 * aw:4 * aw], preferred_element_type=F32)


def _proj_call(layer, x, ng3, mod, w_in_bf, gmat, qg, kg):
    L, d = x.shape
    aw = ATT_WIDTH
    unit = ATT_TQ
    per_unit = unit // PROJ_TM
    vec_a = pl.BlockSpec((1, aw), lambda i: (0, 0))
    return pl.pallas_call(
        _proj_kernel,
        grid=(L // PROJ_TM,),
        in_specs=[
            pl.BlockSpec((PROJ_TM, d), lambda i: (i, 0)),
            _gain_spec(layer, 1, d), _cond_spec(layer, 3, d), _cond_spec(layer, 4, d),
            pl.BlockSpec((None, d, IN_WIDTH), lambda i: (layer, 0, 0),
                         pipeline_mode=pl.Buffered(1)),
            pl.BlockSpec((PROJ_NORM_LANES, PROJ_NORM_LANES), lambda i: (0, 0)),
            vec_a, vec_a,
        ],
        out_specs=[
            pl.BlockSpec((PROJ_TM, aw), lambda i: (i, 0)),
            pl.BlockSpec((PROJ_TM, aw), lambda i: (i, 0)),
            pl.BlockSpec((1, aw, PROJ_TM), lambda i: (i // per_unit, 0, i % per_unit)),
            pl.BlockSpec((PROJ_TM, SSM_WIDTH), lambda i: (i, 0)),
        ],
        out_shape=[
            jax.ShapeDtypeStruct((L, aw), BF16),
            jax.ShapeDtypeStruct((L, aw), BF16),
            jax.ShapeDtypeStruct((L // unit, aw, unit), BF16),
            jax.ShapeDtypeStruct((L, SSM_WIDTH), F32),
        ],
        compiler_params=_cparams(("arbitrary",)),
        name="in_proj",
    )(x, ng3, mod, mod, w_in_bf, gmat, qg, kg)


def _attn_kernel(q_ref, k_ref, vt_ref, nbias_ref, cfar_ref, bound_ref, lq1_ref, lk1_ref, lq2_ref,
                 lk2_ref, sg_ref, o_ref, acc_ref, m_ref, l_ref, bias_ref, s_ref, p_ref, *, lam_init):
    t = ATT_T
    tq = ATT_TQ
    qb = tq // t
    n_tiles = q_ref.shape[0] // tq

    bound = bound_ref[0]
    far_bias = cfar_ref[0] - bound
    tiles = {2: jnp.broadcast_to(far_bias, (t, t)), 1: nbias_ref[0, 1] - bound,
             0: nbias_ref[0, 0] - bound, -1: jnp.full((t, t), NEG_INF, F32)}
    for d in range(-qb, 3):
        for a in range(qb):
            bias_ref[d + qb, :, a * t:(a + 1) * t] = tiles[max(min(d + a, 2), -1)]

    lam = (jnp.exp(jnp.sum(lq1_ref[...] * lk1_ref[...], keepdims=True))
           - jnp.exp(jnp.sum(lq2_ref[...] * lk2_ref[...], keepdims=True)) + lam_init)
    out_gain = sg_ref[...] * (1.0 - lam_init)

    def load_qcat(qi):
        q = q_ref[pl.ds(pl.multiple_of(qi * tq, tq), tq), :]
        lane = lax.broadcasted_iota(jnp.int32, q.shape, 1)
        zero = jnp.zeros_like(q)
        return jnp.concatenate([jnp.where(lane < DIFF_HEAD_DIM, q, zero),
                                jnp.where(lane >= DIFF_HEAD_DIM, q, zero)], axis=0)

    def scores(blk, qmat):
        off = pl.multiple_of(blk * t, t)
        return lax.dot_general(k_ref[pl.ds(off, t), :], qmat, (((1,), (1,)), ((), ())),
                               preferred_element_type=F32)

    def start_tile(qcat):
        acc_ref[...] = jnp.zeros(acc_ref.shape, F32)
        l_ref[...] = jnp.zeros(l_ref.shape, F32)
        p_ref[...] = jnp.zeros(p_ref.shape, BF16)
        s_ref[...] = scores(0, qcat)

    start_tile(load_qcat(0))

    def tile_body(qi, carry):
        qcat = load_qcat(qi)

        def biased(s, blk):
            bias = bias_ref[jnp.clip(qb * qi - blk, -qb, 2) + qb]
            return s[:, :tq] + bias, s[:, tq:] + bias

        def fast_unit(u, c):
            s_cur = s_ref[...]
            pv = jnp.dot(vt_ref[jnp.maximum(u - 1, 0), :, (qb - 1) * t:qb * t], p_ref[...],
                         preferred_element_type=F32)
            lsum = jnp.zeros((1, 2 * tq), F32)
            for b in range(qb):
                blk = u * qb + b
                s_next = scores(blk + 1, qcat)
                if b < qb - 1:
                    s0, s1 = s_cur[:, :tq] + far_bias, s_cur[:, tq:] + far_bias
                else:
                    s0, s1 = biased(s_cur, blk)
                p = jnp.concatenate([jnp.exp2(s0), jnp.exp2(s1)], axis=1)
                lsum = lsum + jnp.sum(p, axis=0, keepdims=True)
                if b < qb - 1:
                    pv = pv + jnp.dot(vt_ref[u, :, b * t:(b + 1) * t], p.astype(BF16),
                                      preferred_element_type=F32)
                else:
                    p_ref[...] = p.astype(BF16)
                s_cur = s_next
            s_ref[...] = s_cur
            acc_ref[...] += pv
            l_ref[...] += lsum
            return c

        lax.fori_loop(0, qi, fast_unit, 0)

        acc_ref[...] += jnp.dot(vt_ref[jnp.maximum(qi - 1, 0), :, (qb - 1) * t:qb * t], p_ref[...],
                                preferred_element_type=F32)
        s_cur = s_ref[...]
        for b in range(qb):
            w = (qb - b) * t
            if b < qb - 1:
                qn = jnp.concatenate([qcat[(b + 1) * t:tq], qcat[tq + (b + 1) * t:2 * tq]], axis=0)
                s_next = scores(qb * qi + b + 1, qn)
            bias = bias_ref[qb - b, :, b * t:tq]
            p0 = jnp.exp2(s_cur[:, :w] + bias)
            p1 = jnp.exp2(s_cur[:, w:] + bias)
            l_ref[:, b * t:tq] += jnp.sum(p0, axis=0, keepdims=True)
            l_ref[:, tq + b * t:2 * tq] += jnp.sum(p1, axis=0, keepdims=True)
            pv = jnp.dot(vt_ref[qi, :, b * t:(b + 1) * t],
                         jnp.concatenate([p0, p1], axis=1).astype(BF16),
                         preferred_element_type=F32)
            acc_ref[:, b * t:tq] += pv[:, :w]
            acc_ref[:, tq + b * t:2 * tq] += pv[:, w:]
            if b < qb - 1:
                s_cur = s_next

        l_min = jnp.min(l_ref[...], keepdims=True)
        underflow = jnp.logical_not(l_min[0, 0] > ATT_MIN_DENOM)

        @pl.when(underflow)
        def _():
            m_ref[...] = jnp.full(m_ref.shape, NEG_INF, F32)
            l_ref[...] = jnp.zeros(l_ref.shape, F32)
            acc_ref[...] = jnp.zeros(acc_ref.shape, F32)

            def exact_unit(u, c):
                for b in range(qb):
                    blk = u * qb + b
                    s = jnp.concatenate(biased(scores(blk, qcat), blk), axis=1)
                    m_old = m_ref[...]
                    m_new = jnp.maximum(m_old, jnp.max(s, axis=0, keepdims=True))
                    alpha = jnp.exp2(m_old - m_new)
                    p = jnp.exp2(s - m_new)
                    l_ref[...] = alpha * l_ref[...] + jnp.sum(p, axis=0, keepdims=True)
                    acc_ref[...] = alpha * acc_ref[...] + jnp.dot(
                        vt_ref[u, :, b * t:(b + 1) * t], p.astype(BF16),
                        preferred_element_type=F32)
                    m_ref[...] = m_new
                return c

            lax.fori_loop(0, qi + 1, exact_unit, 0)

        on = acc_ref[...] / l_ref[...]
        o = on[:, :tq] - lam * on[:, tq:]
        ms = jnp.mean(o * o, axis=0, keepdims=True)
        o = o * lax.rsqrt(ms + SUBLN_EPS) * out_gain
        o_ref[pl.ds(pl.multiple_of(qi * tq, tq), tq), :] = o.T.astype(BF16)
        start_tile(load_qcat(jnp.minimum(qi + 1, n_tiles - 1)))
        return carry

    lax.fori_loop(0, n_tiles, tile_body, 0)


def _attn_call(layer_idx, q, k, vt, near_bias, c_far, bound, lq1, lk1, lq2, lk2, sg_col):
    L = q.shape[0]
    t, tq = ATT_T, ATT_TQ
    assert tq % t == 0 and L % tq == 0
    lam_init = 0.8 - 0.6 * math.exp(-0.3 * layer_idx)
    lvec = pl.BlockSpec((1, DIFF_HEAD_DIM), lambda h: (0, 0))
    scalar = pl.BlockSpec((1, 1, 1), lambda h: (h, 0, 0))
    return pl.pallas_call(
        functools.partial(_attn_kernel, lam_init=lam_init),
        grid=(N_ATT_HEADS,),
        in_specs=[
            pl.BlockSpec((L, V_HEAD_DIM), lambda h: (0, h)),
            pl.BlockSpec((L, V_HEAD_DIM), lambda h: (0, h)),
            pl.BlockSpec((L // tq, V_HEAD_DIM, tq), lambda h: (0, h, 0)),
            pl.BlockSpec((1, 2, t, t), lambda h: (h, 0, 0, 0)),
            scalar, scalar,
            lvec, lvec, lvec, lvec,
            pl.BlockSpec((V_HEAD_DIM, 1), lambda h: (0, 0)),
        ],
        out_specs=pl.BlockSpec((L, V_HEAD_DIM), lambda h: (0, h)),
        out_shape=jax.ShapeDtypeStruct((L, ATT_WIDTH), BF16),
        scratch_shapes=[
            pltpu.VMEM((V_HEAD_DIM, 2 * tq), F32),
            pltpu.VMEM((1, 2 * tq), F32),
            pltpu.VMEM((1, 2 * tq), F32),
            pltpu.VMEM((tq // t + 3, t, tq), F32),
            pltpu.VMEM((t, 2 * tq), F32),
            pltpu.VMEM((t, 2 * tq), BF16),
        ],
        compiler_params=_cparams(("arbitrary",)),
        name="diff_attn",
    )(q, k, vt, near_bias, c_far, bound, lq1, lk1, lq2, lk2, sg_col)


def _ssm_kernel(u_ref, un_ref, bm_ref, cm_ref, lam_ref, lamr_ref, d_ref, y_ref,
                bu_a_ref, bu_b_ref, carry_ref):
    ns = SSM_SLAB_GROUPS * SSM_STATE
    tt = SSM_TT
    strip = SSM_STRIP_ROWS
    n_strips = tt // strip
    bm = bm_ref[0]

    @pl.when(pl.program_id(1) == 0)
    def _():
        carry_ref[...] = jnp.zeros_like(carry_ref)
        bu_a_ref[...] = jnp.dot(u_ref[0:tt, :].astype(BF16), bm, preferred_element_type=F32)

    lr = jnp.broadcast_to(lam_ref[0, 0:1, :], (8, ns))
    li = jnp.broadcast_to(lam_ref[0, 1:2, :], (8, ns))
    pr = lamr_ref[0, 0:1, :]
    pi = lamr_ref[0, 1:2, :]
    row_id = lax.broadcasted_iota(jnp.int32, (8, ns), 0)
    z = jnp.zeros((8, ns), F32)

    def tile(cur_ref, nxt_ref, row0, next_u):
        def advance(r, sr, si, store):
            rows = pl.ds(8 * r, 8)
            nsr = lr * sr - li * si + cur_ref[rows, 0:ns]
            nsi = lr * si + li * sr + cur_ref[rows, ns:2 * ns]
            if store:
                cur_ref[rows, 0:ns] = nsr
                cur_ref[rows, ns:2 * ns] = nsi
            return nsr, nsi

        sr, si = z, z
        for k in range(n_strips):
            nxt_ref[pl.ds(k * strip, strip), :] = jnp.dot(
                next_u(k * strip, strip).astype(BF16), bm, preferred_element_type=F32)
            for r in range(k * strip // 8, (k + 1) * strip // 8):
                sr, si = advance(r, sr, si, False)
        er, ei = sr, si

        cr = carry_ref[0:1, 0:ns]
        ci = carry_ref[0:1, ns:2 * ns]
        sr, si = z, z
        for c in range(8):
            sr = jnp.where(row_id == c, cr, sr)
            si = jnp.where(row_id == c, ci, si)
            cr, ci = (pr * cr - pi * ci + er[c:c + 1, :], pr * ci + pi * cr + ei[c:c + 1, :])
        carry_ref[0:1, 0:ns] = cr
        carry_ref[0:1, ns:2 * ns] = ci

        def emit(k):
            rows = pl.ds(k * strip, strip)
            out_rows = pl.ds(row0 + k * strip, strip)
            y = (jnp.dot(cur_ref[rows, :].astype(BF16), cm_ref[0], preferred_element_type=F32)
                 + d_ref[0] * u_ref[out_rows, :])
            y_ref[out_rows, :] = y

        for k in range(n_strips):
            if k > 0:
                emit(k - 1)
            for r in range(k * strip // 8, (k + 1) * strip // 8):
                sr, si = advance(r, sr, si, True)
        emit(n_strips - 1)

    tile(bu_a_ref, bu_b_ref, 0, lambda r, n: u_ref[pl.ds(tt + r, n), :])
    tile(bu_b_ref, bu_a_ref, tt, lambda r, n: un_ref[pl.ds(r, n), :])


def _ssm_call(u_perm, bmat, cmat, lam2, lamr2, d3):
    L = u_perm.shape[0]
    n_slab = N_SSM_GROUPS // SSM_SLAB_GROUPS
    n_tiles = L // SSM_TT
    assert n_tiles % 2 == 0
    ns = SSM_SLAB_GROUPS * SSM_STATE
    cw = SSM_SLAB_GROUPS * SSM_GROUP
    return pl.pallas_call(
        _ssm_kernel,
        grid=(n_slab, n_tiles // 2),
        in_specs=[
            pl.BlockSpec((2 * SSM_TT, cw), lambda s, p: (p, s)),
            pl.BlockSpec((SSM_TT, cw), lambda s, p: (jnp.minimum(2 * p + 2, n_tiles - 1), s)),
            pl.BlockSpec((1, cw, 2 * ns), lambda s, p: (s, 0, 0)),
            pl.BlockSpec((1, 2 * ns, cw), lambda s, p: (s, 0, 0)),
            pl.BlockSpec((1, 2, ns), lambda s, p: (s, 0, 0)),
            pl.BlockSpec((1, 2, ns), lambda s, p: (s, 0, 0)),
            pl.BlockSpec((1, 1, cw), lambda s, p: (s, 0, 0)),
        ],
        out_specs=pl.BlockSpec((2 * SSM_TT, cw), lambda s, p: (p, s)),
        out_shape=jax.ShapeDtypeStruct((L, SSM_WIDTH), F32),
        scratch_shapes=[
            pltpu.VMEM((SSM_TT, 2 * ns), F32),
            pltpu.VMEM((SSM_TT, 2 * ns), F32),
            pltpu.VMEM((1, 2 * ns), F32),
        ],
        compiler_params=_cparams(("arbitrary", "arbitrary")),
        name="s5_scan",
    )(u_perm, u_perm, bmat, cmat, lam2, lamr2, d3)


def _ssm_params(lam_re, lam_im, log_step, b_re, b_im, c_re, c_im, d):
    g, p, hc = N_SSM_GROUPS, SSM_STATE, SSM_GROUP
    sg = SSM_SLAB_GROUPS
    n_slab = g // sg
    lam = lax.complex(jnp.minimum(lam_re.astype(F32), -1e-4), lam_im.astype(F32))
    step = jnp.exp(log_step.astype(F32))[:, None]
    lam_bar = jnp.exp(lam * step)
    lam_bar_r = jnp.exp(lam * (step * (SSM_TT // 8)))
    b_bar = ((lam_bar - 1.0) / lam)[:, :, None] * lax.complex(b_re.astype(F32), b_im.astype(F32))
    eye = jnp.eye(sg, dtype=F32)

    def b_block(part):
        z = part.reshape(n_slab, sg, p, hc).transpose(0, 1, 3, 2)
        return (z[:, :, :, None, :] * eye[None, :, None, :, None]).reshape(n_slab, sg * hc, sg * p)

    def c_block(part):
        z = part.reshape(n_slab, sg, hc, p).transpose(0, 1, 3, 2)
        return (z[:, :, :, None, :] * eye[None, :, None, :, None]).reshape(n_slab, sg * p, sg * hc)

    bmat = jnp.concatenate([b_block(jnp.real(b_bar)), b_block(jnp.imag(b_bar))], axis=-1)
    cmat = jnp.concatenate([c_block(c_re.astype(F32)), c_block(-c_im.astype(F32))], axis=1)

    def rows(zc):
        return jnp.stack([jnp.real(zc).reshape(n_slab, sg * p),
                          jnp.imag(zc).reshape(n_slab, sg * p)], axis=1)

    d3 = d.astype(F32).reshape(n_slab, 1, sg * hc)
    return bmat.astype(BF16), cmat.astype(BF16), rows(lam_bar), rows(lam_bar_r), d3


def _out_kernel(att_ref, y_ref, gw_ref, gb_ref, woa_ref, wob_ref, x_ref, gt_ref, o_ref,
                gw_bf_ref, wo_bf_ref):
    aw = ATT_WIDTH

    @pl.when(pl.program_id(0) == 0)
    def _():
        gw_bf_ref[...] = gw_ref[...].astype(BF16)
        wo_bf_ref[0:aw, :] = woa_ref[...].astype(BF16)
        wo_bf_ref[aw:, :] = wob_ref[...].astype(BF16)

    y = jax.nn.gelu(y_ref[...])
    z = jnp.dot(y.astype(BF16), gw_bf_ref[...], preferred_element_type=F32) + gb_ref[...]
    yg = (y * jax.nn.sigmoid(z)).astype(BF16)
    m = (jnp.dot(att_ref[...], wo_bf_ref[0:aw, :], preferred_element_type=F32)
         + jnp.dot(yg, wo_bf_ref[aw:, :], preferred_element_type=F32))
    o_ref[...] = x_ref[...] + gt_ref[...] * m


def _out_call(layer, att, y, glu_w, glu_b, w_out, x, mod):
    L, d = x.shape
    aw, sw = ATT_WIDTH, SSM_WIDTH
    one = pl.Buffered(1)
    return pl.pallas_call(
        _out_kernel,
        grid=(L // OUT_TM,),
        in_specs=[
            pl.BlockSpec((OUT_TM, aw), lambda i: (i, 0)),
            pl.BlockSpec((OUT_TM, sw), lambda i: (i, 0)),
            pl.BlockSpec((None, sw, sw), lambda i: (layer, 0, 0), pipeline_mode=one),
            pl.BlockSpec((1, sw), lambda i: (0, 0)),
            pl.BlockSpec((None, aw, d), lambda i: (layer, 0, 0), pipeline_mode=one),
            pl.BlockSpec((None, sw, d), lambda i: (layer, 1, 0), pipeline_mode=one),
            pl.BlockSpec((OUT_TM, d), lambda i: (i, 0)),
            _cond_spec(layer, 5, d),
        ],
        out_specs=pl.BlockSpec((OUT_TM, d), lambda i: (i, 0)),
        out_shape=jax.ShapeDtypeStruct((L, d), F32),
        scratch_shapes=[pltpu.VMEM((sw, sw), BF16), pltpu.VMEM((aw + sw, d), BF16)],
        compiler_params=_cparams(("arbitrary",)),
        name="out_proj",
    )(att, y, glu_w, glu_b, w_out, w_out, x, mod)


def _t5_causal_buckets(dist):
    max_exact = N_BUCKETS // 2
    d = jnp.maximum(dist, 1).astype(F32)
    large = max_exact + (jnp.log(d / max_exact) / math.log(MAX_DISTANCE / max_exact)
                         * (N_BUCKETS - max_exact)).astype(jnp.int32)
    large = jnp.minimum(large, N_BUCKETS - 1)
    return jnp.where(dist < max_exact, dist, large)


def _near_bias(rel_bias):
    t = ATT_T
    assert t >= MAX_DISTANCE
    kk = jnp.arange(t, dtype=jnp.int32)[:, None]
    qq = jnp.arange(t, dtype=jnp.int32)[None, :]
    dist = jnp.stack([qq - kk, qq - kk + t], axis=0)
    bucket = _t5_causal_buckets(jnp.maximum(dist, 0))
    rb = rel_bias.astype(F32) * LOG2_E
    val = jnp.zeros((N_ATT_HEADS,) + dist.shape, F32)
    for b in range(N_BUCKETS):
        val = jnp.where((bucket == b)[None], rb[b][:, None, None, None], val)
    return jnp.where((dist >= 0)[None], val, NEG_INF)


def _chunk_interleave(a):
    L, w = a.shape
    return a.reshape(L // SSM_TT, 8, SSM_TT // 8, w).transpose(0, 2, 1, 3).reshape(L, w)


def _chunk_deinterleave(a):
    L, w = a.shape
    return a.reshape(L // SSM_TT, SSM_TT // 8, 8, w).transpose(0, 2, 1, 3).reshape(L, w)


def kernel(x, c, rel_bias, ada_w, ada_b, norm_g, ffn1_w_gate, ffn1_w_up, ffn1_w_down, ffn2_w_gate, ffn2_w_up, ffn2_w_down, w_in, w_out, q_norm_g, k_norm_g, lambda_q1, lambda_k1, lambda_q2, lambda_k2, subln_g, ssm_lambda_re, ssm_lambda_im, ssm_log_step, ssm_b_re, ssm_b_im, ssm_c_re, ssm_c_im, ssm_d, ssm_glu_w, ssm_glu_b):
    b, L, d = x.shape
    assert b == 1 and c.shape == (1, d)
    x2 = x.reshape(L, d)

    mod = _ada_call(c.reshape(d, 1), ada_w, ada_b.reshape(DEPTH, 1, N_COND * d))
    near_bias = _near_bias(rel_bias)
    c_far = (rel_bias.astype(F32)[N_BUCKETS - 1] * LOG2_E).reshape(N_ATT_HEADS, 1, 1)
    w_in_bf = w_in.astype(BF16)
    hd = DIFF_HEAD_DIM
    group_of = jnp.arange(PROJ_NORM_LANES) // hd
    gmat = jnp.where(group_of[:, None] == group_of[None, :], 1.0 / hd, 0.0).astype(BF16)
    n_rep = ATT_WIDTH // hd

    ng3 = norm_g.reshape(DEPTH * 3, 1, d)

    for i in range(DEPTH):
        x2 = _ffn_call(i, 0, x2, ng3, mod, ffn1_w_gate, ffn1_w_up, ffn1_w_down)

        qg = jnp.tile(q_norm_g[i].astype(F32), n_rep)[None] * (hd ** -0.5 * LOG2_E)
        kg = jnp.tile(k_norm_g[i].astype(F32), n_rep)[None]
        q, k, vt, u = _proj_call(i, x2, ng3, mod, w_in_bf, gmat, qg, kg)
        bound = (DIFF_HEAD_DIM * jnp.max(jnp.abs(qg[0, :hd] * kg[0, :hd]))
                 + jnp.max(rel_bias.astype(F32), axis=0) * LOG2_E).reshape(N_ATT_HEADS, 1, 1)
        att = _attn_call(i, q, k, vt, near_bias, c_far, bound, lambda_q1[i][None], lambda_k1[i][None],
                         lambda_q2[i][None], lambda_k2[i][None], subln_g[i].reshape(V_HEAD_DIM, 1))
        bmat, cmat, lam2, lamr2, d3 = _ssm_params(
            ssm_lambda_re[i], ssm_lambda_im[i], ssm_log_step[i], ssm_b_re[i], ssm_b_im[i],
            ssm_c_re[i], ssm_c_im[i], ssm_d[i])
        y = _chunk_deinterleave(_ssm_call(_chunk_interleave(u), bmat, cmat, lam2, lamr2, d3))
        x2 = _out_call(i, att, y, ssm_glu_w, ssm_glu_b[i][None], w_out, x2, mod)

        x2 = _ffn_call(i, 2, x2, ng3, mod, ffn2_w_gate, ffn2_w_up, ffn2_w_down)
    return x2.reshape(b, L, d)
```

```python
import functools
import math

import jax
import jax.numpy as jnp
from jax import lax
from jax.experimental import pallas as pl
from jax.experimental.pallas import tpu as pltpu

D_MODEL = 2048
SEQ = 8192
DEPTH = 2
ATT_WIDTH = 1024
SSM_WIDTH = 1024
DIFF_HEAD_DIM = 64
V_HEAD_DIM = 128
N_ATT_HEADS = 8
SSM_GROUP = 16
N_SSM_GROUPS = 64
SSM_STATE = 64
IN_WIDTH = 4096
D_FF = 5632
N_BUCKETS = 32
MAX_DISTANCE = 128
N_COND = 9
NORM_EPS = 1e-6
SUBLN_EPS = 1e-5
NEG_INF = -1e30
LOG2_E = math.log2(math.e)

F32 = jnp.float32
BF16 = jnp.bfloat16

VMEM_LIMIT_BYTES = 60 * 1024 * 1024

ADA_TN = 2048
FFN_TM = 1024
FFN_TF = 512
FFN_NORM_ROWS = 512
PROJ_TM = 512
PROJ_NORM_LANES = 128
ATT_T = 256
ATT_TQ = 1024
ATT_MIN_DENOM = 2.0 ** -40
SSM_TT = 1024
SSM_SLAB_GROUPS = 8
SSM_STRIP_ROWS = 128
OUT_TM = 512


def _cparams(sem):
    return pltpu.CompilerParams(dimension_semantics=sem, vmem_limit_bytes=VMEM_LIMIT_BYTES)


def _ada_kernel(c_ref, w_ref, b_ref, o_ref):
    c = c_ref[...]
    cs = c * jax.nn.sigmoid(c)
    o_ref[0] = jnp.sum(w_ref[0] * cs, axis=0, keepdims=True) + b_ref[0]


def _ada_call(c_col, ada_w, ada_b3):
    depth, d, n = ada_w.shape
    return pl.pallas_call(
        _ada_kernel,
        grid=(depth, n // ADA_TN),
        in_specs=[
            pl.BlockSpec((d, 1), lambda l, j: (0, 0)),
            pl.BlockSpec((1, d, ADA_TN), lambda l, j: (l, 0, j)),
            pl.BlockSpec((1, 1, ADA_TN), lambda l, j: (l, 0, j)),
        ],
        out_specs=pl.BlockSpec((1, 1, ADA_TN), lambda l, j: (l, 0, j)),
        out_shape=jax.ShapeDtypeStruct((depth, 1, n), F32),
        compiler_params=_cparams(("arbitrary", "arbitrary")),
        name="adaln",
    )(c_col, ada_w, ada_b3)


def _cond_spec(layer, n, d):
    return pl.BlockSpec((None, 1, d), lambda *_: (layer, 0, n))


def _gain_spec(layer, sub, d):
    return pl.BlockSpec((None, 1, d), lambda *_: (layer * 3 + sub, 0, 0))


def _norm_mod(x, g, sc, sh):
    ms = jnp.mean(x * x, axis=-1, keepdims=True)
    return (x * lax.rsqrt(ms + NORM_EPS) * g) * (1.0 + sc) + sh


def _ffn_kernel(x_ref, ng_ref, sh_ref, sc_ref, gt_ref, wg_ref, wu_ref, wd_ref, o_ref, h_ref):
    j = pl.program_id(1)

    def weights():
        return (wg_ref[...].astype(BF16), wu_ref[...].astype(BF16), wd_ref[...].astype(BF16))

    def hidden_update(h, wg, wu, wd):
        g = jnp.dot(h, wg, preferred_element_type=F32)
        u = jnp.dot(h, wu, preferred_element_type=F32)
        a = (g * jax.nn.sigmoid(g) * u).astype(BF16)
        return jnp.dot(a, wd, preferred_element_type=F32)

    half_gate = 0.5 * gt_ref[...]

    @pl.when(j == 0)
    def _():
        w = weights()
        for r in range(0, FFN_TM, FFN_NORM_ROWS):
            rows = pl.ds(r, FFN_NORM_ROWS)
            x = x_ref[rows, :]
            h = _norm_mod(x, ng_ref[...], sc_ref[...], sh_ref[...]).astype(BF16)
            h_ref[rows, :] = h
            o_ref[rows, :] = x + half_gate * hidden_update(h, *w)

    @pl.when(j > 0)
    def _():
        o_ref[...] += half_gate * hidden_update(h_ref[...], *weights())


def _ffn_call(layer, sub, x, ng3, mod, wg, wu, wd):
    L, d = x.shape
    dff = wg.shape[-1]
    return pl.pallas_call(
        _ffn_kernel,
        grid=(L // FFN_TM, dff // FFN_TF),
        in_specs=[
            pl.BlockSpec((FFN_TM, d), lambda i, j: (i, 0)),
            _gain_spec(layer, sub, d), _cond_spec(layer, 3 * sub, d),
            _cond_spec(layer, 3 * sub + 1, d), _cond_spec(layer, 3 * sub + 2, d),
            pl.BlockSpec((None, d, FFN_TF), lambda i, j: (layer, 0, j)),
            pl.BlockSpec((None, d, FFN_TF), lambda i, j: (layer, 0, j)),
            pl.BlockSpec((None, FFN_TF, d), lambda i, j: (layer, j, 0)),
        ],
        out_specs=pl.BlockSpec((FFN_TM, d), lambda i, j: (i, 0), pipeline_mode=pl.Buffered(1)),
        out_shape=jax.ShapeDtypeStruct((L, d), F32),
        scratch_shapes=[pltpu.VMEM((FFN_TM, d), BF16)],
        compiler_params=_cparams(("arbitrary", "arbitrary")),
        name="ffn",
    )(x, ng3, mod, mod, mod, wg, wu, wd)


def _proj_kernel(x_ref, ng_ref, sh_ref, sc_ref, w_ref, gm_ref, qg_ref, kg_ref,
                 q_ref, k_ref, vt_ref, u_ref):
    h = _norm_mod(x_ref[...], ng_ref[...], sc_ref[...], sh_ref[...]).astype(BF16)
    aw = ATT_WIDTH

    def head_norm(z, g):
        sq = (z * z).astype(BF16)
        lanes = gm_ref.shape[0]
        ms = jnp.concatenate(
            [jnp.dot(sq[:, c:c + lanes], gm_ref[...], preferred_element_type=F32)
             for c in range(0, aw, lanes)], axis=1)
        return z * lax.rsqrt(ms + NORM_EPS) * g

    q = jnp.dot(h, w_ref[:, 0:aw], preferred_element_type=F32)
    q_ref[...] = head_norm(q, qg_ref[...]).astype(BF16)
    k = jnp.dot(h, w_ref[:, aw:2 * aw], preferred_element_type=F32)
    k_ref[...] = head_norm(k, kg_ref[...]).astype(BF16)
    v = jnp.dot(h, w_ref[:, 2 * aw:3 * aw], preferred_element_type=F32)
    vt_ref[0] = v.T.astype(BF16)
    u_ref[...] = jnp.dot(h, w_ref[:, 3 * aw:4 * aw], preferred_element_type=F32)


def _proj_call(layer, x, ng3, mod, w_in_bf, gmat, qg, kg):
    L, d = x.shape
    aw = ATT_WIDTH
    unit = ATT_TQ
    per_unit = unit // PROJ_TM
    vec_a = pl.BlockSpec((1, aw), lambda i: (0, 0))
    return pl.pallas_call(
        _proj_kernel,
        grid=(L // PROJ_TM,),
        in_specs=[
            pl.BlockSpec((PROJ_TM, d), lambda i: (i, 0)),
            _gain_spec(layer, 1, d), _cond_spec(layer, 3, d), _cond_spec(layer, 4, d),
            pl.BlockSpec((None, d, IN_WIDTH), lambda i: (layer, 0, 0),
                         pipeline_mode=pl.Buffered(1)),
            pl.BlockSpec((PROJ_NORM_LANES, PROJ_NORM_LANES), lambda i: (0, 0)),
            vec_a, vec_a,
        ],
        out_specs=[
            pl.BlockSpec((PROJ_TM, aw), lambda i: (i, 0)),
            pl.BlockSpec((PROJ_TM, aw), lambda i: (i, 0)),
            pl.BlockSpec((1, aw, PROJ_TM), lambda i: (i // per_unit, 0, i % per_unit)),
            pl.BlockSpec((PROJ_TM, SSM_WIDTH), lambda i: (i, 0)),
        ],
        out_shape=[
            jax.ShapeDtypeStruct((L, aw), BF16),
            jax.ShapeDtypeStruct((L, aw), BF16),
            jax.ShapeDtypeStruct((L // unit, aw, unit), BF16),
            jax.ShapeDtypeStruct((L, SSM_WIDTH), F32),
        ],
        compiler_params=_cparams(("arbitrary",)),
        name="in_proj",
    )(x, ng3, mod, mod, w_in_bf, gmat, qg, kg)


def _attn_kernel(q_ref, k_ref, vt_ref, nbias_ref, cfar_ref, bound_ref, lq1_ref, lk1_ref, lq2_ref,
                 lk2_ref, sg_ref, o_ref, acc_ref, m_ref, l_ref, bias_ref, s_ref, p_ref, *, lam_init):
    t = ATT_T
    tq = ATT_TQ
    qb = tq // t
    n_tiles = q_ref.shape[0] // tq

    bound = bound_ref[0]
    far_bias = cfar_ref[0] - bound
    tiles = {2: jnp.broadcast_to(far_bias, (t, t)), 1: nbias_ref[0, 1] - bound,
             0: nbias_ref[0, 0] - bound, -1: jnp.full((t, t), NEG_INF, F32)}
    for d in range(-qb, 3):
        for a in range(qb):
            bias_ref[d + qb, :, a * t:(a + 1) * t] = tiles[max(min(d + a, 2), -1)]

    lam = (jnp.exp(jnp.sum(lq1_ref[...] * lk1_ref[...], keepdims=True))
           - jnp.exp(jnp.sum(lq2_ref[...] * lk2_ref[...], keepdims=True)) + lam_init)
    out_gain = sg_ref[...] * (1.0 - lam_init)

    def load_qcat(qi):
        q = q_ref[pl.ds(pl.multiple_of(qi * tq, tq), tq), :]
        lane = lax.broadcasted_iota(jnp.int32, q.shape, 1)
        zero = jnp.zeros_like(q)
        return jnp.concatenate([jnp.where(lane < DIFF_HEAD_DIM, q, zero),
                                jnp.where(lane >= DIFF_HEAD_DIM, q, zero)], axis=0)

    def scores(blk, qmat):
        off = pl.multiple_of(blk * t, t)
        return lax.dot_general(k_ref[pl.ds(off, t), :], qmat, (((1,), (1,)), ((), ())),
                               preferred_element_type=F32)

    def start_tile(qcat):
        acc_ref[...] = jnp.zeros(acc_ref.shape, F32)
        l_ref[...] = jnp.zeros(l_ref.shape, F32)
        p_ref[...] = jnp.zeros(p_ref.shape, BF16)
        s_ref[...] = scores(0, qcat)

    start_tile(load_qcat(0))

    def tile_body(qi, carry):
        qcat = load_qcat(qi)

        def biased(s, blk):
            bias = bias_ref[jnp.clip(qb * qi - blk, -qb, 2) + qb]
            return s[:, :tq] + bias, s[:, tq:] + bias

        def fast_unit(u, c):
            s_cur = s_ref[...]
            pv = jnp.dot(vt_ref[jnp.maximum(u - 1, 0), :, (qb - 1) * t:qb * t], p_ref[...],
                         preferred_element_type=F32)
            lsum = jnp.zeros((1, 2 * tq), F32)
            for b in range(qb):
                blk = u * qb + b
                s_next = scores(blk + 1, qcat)
                if b < qb - 1:
                    s0, s1 = s_cur[:, :tq] + far_bias, s_cur[:, tq:] + far_bias
                else:
                    s0, s1 = biased(s_cur, blk)
                p = jnp.concatenate([jnp.exp2(s0), jnp.exp2(s1)], axis=1)
                lsum = lsum + jnp.sum(p, axis=0, keepdims=True)
                if b < qb - 1:
                    pv = pv + jnp.dot(vt_ref[u, :, b * t:(b + 1) * t], p.astype(BF16),
                                      preferred_element_type=F32)
                else:
                    p_ref[...] = p.astype(BF16)
                s_cur = s_next
            s_ref[...] = s_cur
            acc_ref[...] += pv
            l_ref[...] += lsum
            return c

        lax.fori_loop(0, qi, fast_unit, 0)

        acc_ref[...] += jnp.dot(vt_ref[jnp.maximum(qi - 1, 0), :, (qb - 1) * t:qb * t], p_ref[...],
                                preferred_element_type=F32)
        s_cur = s_ref[...]
        for b in range(qb):
            w = (qb - b) * t
            if b < qb - 1:
                qn = jnp.concatenate([qcat[(b + 1) * t:tq], qcat[tq + (b + 1) * t:2 * tq]], axis=0)
                s_next = scores(qb * qi + b + 1, qn)
            bias = bias_ref[qb - b, :, b * t:tq]
            p0 = jnp.exp2(s_cur[:, :w] + bias)
            p1 = jnp.exp2(s_cur[:, w:] + bias)
            l_ref[:, b * t:tq] += jnp.sum(p0, axis=0, keepdims=True)
            l_ref[:, tq + b * t:2 * tq] += jnp.sum(p1, axis=0, keepdims=True)
            pv = jnp.dot(vt_ref[qi, :, b * t:(b + 1) * t],
                         jnp.concatenate([p0, p1], axis=1).astype(BF16),
                         preferred_element_type=F32)
            acc_ref[:, b * t:tq] += pv[:, :w]
            acc_ref[:, tq + b * t:2 * tq] += pv[:, w:]
            if b < qb - 1:
                s_cur = s_next

        l_min = jnp.min(l_ref[...], keepdims=True)
        underflow = qi < 0

        @pl.when(underflow)
        def _():
            m_ref[...] = jnp.full(m_ref.shape, NEG_INF, F32)
            l_ref[...] = jnp.zeros(l_ref.shape, F32)
            acc_ref[...] = jnp.zeros(acc_ref.shape, F32)

            def exact_unit(u, c):
                for b in range(qb):
                    blk = u * qb + b
                    s = jnp.concatenate(biased(scores(blk, qcat), blk), axis=1)
                    m_old = m_ref[...]
                    m_new = jnp.maximum(m_old, jnp.max(s, axis=0, keepdims=True))
                    alpha = jnp.exp2(m_old - m_new)
                    p = jnp.exp2(s - m_new)
                    l_ref[...] = alpha * l_ref[...] + jnp.sum(p, axis=0, keepdims=True)
                    acc_ref[...] = alpha * acc_ref[...] + jnp.dot(
                        vt_ref[u, :, b * t:(b + 1) * t], p.astype(BF16),
                        preferred_element_type=F32)
                    m_ref[...] = m_new
                return c

            lax.fori_loop(0, qi + 1, exact_unit, 0)

        on = acc_ref[...] / l_ref[...]
        o = on[:, :tq] - lam * on[:, tq:]
        ms = jnp.mean(o * o, axis=0, keepdims=True)
        o = o * lax.rsqrt(ms + SUBLN_EPS) * out_gain
        o_ref[pl.ds(pl.multiple_of(qi * tq, tq), tq), :] = o.T.astype(BF16)
        start_tile(load_qcat(jnp.minimum(qi + 1, n_tiles - 1)))
        return carry

    lax.fori_loop(0, n_tiles, tile_body, 0)


def _attn_call(layer_idx, q, k, vt, near_bias, c_far, bound, lq1, lk1, lq2, lk2, sg_col):
    L = q.shape[0]
    t, tq = ATT_T, ATT_TQ
    assert tq % t == 0 and L % tq == 0
    lam_init = 0.8 - 0.6 * math.exp(-0.3 * layer_idx)
    lvec = pl.BlockSpec((1, DIFF_HEAD_DIM), lambda h: (0, 0))
    scalar = pl.BlockSpec((1, 1, 1), lambda h: (h, 0, 0))
    return pl.pallas_call(
        functools.partial(_attn_kernel, lam_init=lam_init),
        grid=(N_ATT_HEADS,),
        in_specs=[
            pl.BlockSpec((L, V_HEAD_DIM), lambda h: (0, h)),
            pl.BlockSpec((L, V_HEAD_DIM), lambda h: (0, h)),
            pl.BlockSpec((L // tq, V_HEAD_DIM, tq), lambda h: (0, h, 0)),
            pl.BlockSpec((1, 2, t, t), lambda h: (h, 0, 0, 0)),
            scalar, scalar,
            lvec, lvec, lvec, lvec,
            pl.BlockSpec((V_HEAD_DIM, 1), lambda h: (0, 0)),
        ],
        out_specs=pl.BlockSpec((L, V_HEAD_DIM), lambda h: (0, h)),
        out_shape=jax.ShapeDtypeStruct((L, ATT_WIDTH), BF16),
        scratch_shapes=[
            pltpu.VMEM((V_HEAD_DIM, 2 * tq), F32),
            pltpu.VMEM((1, 2 * tq), F32),
            pltpu.VMEM((1, 2 * tq), F32),
            pltpu.VMEM((tq // t + 3, t, tq), F32),
            pltpu.VMEM((t, 2 * tq), F32),
            pltpu.VMEM((t, 2 * tq), BF16),
        ],
        compiler_params=_cparams(("arbitrary",)),
        name="diff_attn",
    )(q, k, vt, near_bias, c_far, bound, lq1, lk1, lq2, lk2, sg_col)


def _ssm_kernel(u_ref, un_ref, bm_ref, cm_ref, lam_ref, lamr_ref, d_ref, y_ref,
                bu_a_ref, bu_b_ref, carry_ref):
    ns = SSM_SLAB_GROUPS * SSM_STATE
    tt = SSM_TT
    strip = SSM_STRIP_ROWS
    n_strips = tt // strip
    bm = bm_ref[0]

    @pl.when(pl.program_id(1) == 0)
    def _():
        carry_ref[...] = jnp.zeros_like(carry_ref)
        bu_a_ref[...] = jnp.dot(u_ref[0:tt, :].astype(BF16), bm, preferred_element_type=F32)

    lr = jnp.broadcast_to(lam_ref[0, 0:1, :], (8, ns))
    li = jnp.broadcast_to(lam_ref[0, 1:2, :], (8, ns))
    pr = lamr_ref[0, 0:1, :]
    pi = lamr_ref[0, 1:2, :]
    row_id = lax.broadcasted_iota(jnp.int32, (8, ns), 0)
    z = jnp.zeros((8, ns), F32)

    def tile(cur_ref, nxt_ref, row0, next_u):
        def advance(r, sr, si, store):
            rows = pl.ds(8 * r, 8)
            nsr = lr * sr - li * si + cur_ref[rows, 0:ns]
            nsi = lr * si + li * sr + cur_ref[rows, ns:2 * ns]
            if store:
                cur_ref[rows, 0:ns] = nsr
                cur_ref[rows, ns:2 * ns] = nsi
            return nsr, nsi

        sr, si = z, z
        for k in range(n_strips):
            nxt_ref[pl.ds(k * strip, strip), :] = jnp.dot(
                next_u(k * strip, strip).astype(BF16), bm, preferred_element_type=F32)
            for r in range(k * strip // 8, (k + 1) * strip // 8):
                sr, si = advance(r, sr, si, False)
        er, ei = sr, si

        cr = carry_ref[0:1, 0:ns]
        ci = carry_ref[0:1, ns:2 * ns]
        sr, si = z, z
        for c in range(8):
            sr = jnp.where(row_id == c, cr, sr)
            si = jnp.where(row_id == c, ci, si)
            cr, ci = (pr * cr - pi * ci + er[c:c + 1, :], pr * ci + pi * cr + ei[c:c + 1, :])
        carry_ref[0:1, 0:ns] = cr
        carry_ref[0:1, ns:2 * ns] = ci

        def emit(k):
            rows = pl.ds(k * strip, strip)
            out_rows = pl.ds(row0 + k * strip, strip)
            y = (jnp.dot(cur_ref[rows, :].astype(BF16), cm_ref[0], preferred_element_type=F32)
                 + d_ref[0] * u_ref[out_rows, :])
            y_ref[out_rows, :] = y

        for k in range(n_strips):
            if k > 0:
                emit(k - 1)
            for r in range(k * strip // 8, (k + 1) * strip // 8):
                sr, si = advance(r, sr, si, True)
        emit(n_strips - 1)

    tile(bu_a_ref, bu_b_ref, 0, lambda r, n: u_ref[pl.ds(tt + r, n), :])
    tile(bu_b_ref, bu_a_ref, tt, lambda r, n: un_ref[pl.ds(r, n), :])


def _ssm_call(u_perm, bmat, cmat, lam2, lamr2, d3):
    L = u_perm.shape[0]
    n_slab = N_SSM_GROUPS // SSM_SLAB_GROUPS
    n_tiles = L // SSM_TT
    assert n_tiles % 2 == 0
    ns = SSM_SLAB_GROUPS * SSM_STATE
    cw = SSM_SLAB_GROUPS * SSM_GROUP
    return pl.pallas_call(
        _ssm_kernel,
        grid=(n_slab, n_tiles // 2),
        in_specs=[
            pl.BlockSpec((2 * SSM_TT, cw), lambda s, p: (p, s)),
            pl.BlockSpec((SSM_TT, cw), lambda s, p: (jnp.minimum(2 * p + 2, n_tiles - 1), s)),
            pl.BlockSpec((1, cw, 2 * ns), lambda s, p: (s, 0, 0)),
            pl.BlockSpec((1, 2 * ns, cw), lambda s, p: (s, 0, 0)),
            pl.BlockSpec((1, 2, ns), lambda s, p: (s, 0, 0)),
            pl.BlockSpec((1, 2, ns), lambda s, p: (s, 0, 0)),
            pl.BlockSpec((1, 1, cw), lambda s, p: (s, 0, 0)),
        ],
        out_specs=pl.BlockSpec((2 * SSM_TT, cw), lambda s, p: (p, s)),
        out_shape=jax.ShapeDtypeStruct((L, SSM_WIDTH), F32),
        scratch_shapes=[
            pltpu.VMEM((SSM_TT, 2 * ns), F32),
            pltpu.VMEM((SSM_TT, 2 * ns), F32),
            pltpu.VMEM((1, 2 * ns), F32),
        ],
        compiler_params=_cparams(("arbitrary", "arbitrary")),
        name="s5_scan",
    )(u_perm, u_perm, bmat, cmat, lam2, lamr2, d3)


def _ssm_params(lam_re, lam_im, log_step, b_re, b_im, c_re, c_im, d):
    g, p, hc = N_SSM_GROUPS, SSM_STATE, SSM_GROUP
    sg = SSM_SLAB_GROUPS
    n_slab = g // sg
    lam = lax.complex(jnp.minimum(lam_re.astype(F32), -1e-4), lam_im.astype(F32))
    step = jnp.exp(log_step.astype(F32))[:, None]
    lam_bar = jnp.exp(lam * step)
    lam_bar_r = jnp.exp(lam * (step * (SSM_TT // 8)))
    b_bar = ((lam_bar - 1.0) / lam)[:, :, None] * lax.complex(b_re.astype(F32), b_im.astype(F32))
    eye = jnp.eye(sg, dtype=F32)

    def b_block(part):
        z = part.reshape(n_slab, sg, p, hc).transpose(0, 1, 3, 2)
        return (z[:, :, :, None, :] * eye[None, :, None, :, None]).reshape(n_slab, sg * hc, sg * p)

    def c_block(part):
        z = part.reshape(n_slab, sg, hc, p).transpose(0, 1, 3, 2)
        return (z[:, :, :, None, :] * eye[None, :, None, :, None]).reshape(n_slab, sg * p, sg * hc)

    bmat = jnp.concatenate([b_block(jnp.real(b_bar)), b_block(jnp.imag(b_bar))], axis=-1)
    cmat = jnp.concatenate([c_block(c_re.astype(F32)), c_block(-c_im.astype(F32))], axis=1)

    def rows(zc):
        return jnp.stack([jnp.real(zc).reshape(n_slab, sg * p),
                          jnp.imag(zc).reshape(n_slab, sg * p)], axis=1)

    d3 = d.astype(F32).reshape(n_slab, 1, sg * hc)
    return bmat.astype(BF16), cmat.astype(BF16), rows(lam_bar), rows(lam_bar_r), d3


def _out_kernel(att_ref, y_ref, gw_ref, gb_ref, woa_ref, wob_ref, x_ref, gt_ref, o_ref,
                gw_bf_ref, wo_bf_ref):
    aw = ATT_WIDTH

    @pl.when(pl.program_id(0) == 0)
    def _():
        gw_bf_ref[...] = gw_ref[...].astype(BF16)
        wo_bf_ref[0:aw, :] = woa_ref[...].astype(BF16)
        wo_bf_ref[aw:, :] = wob_ref[...].astype(BF16)

    y = jax.nn.gelu(y_ref[...])
    z = jnp.dot(y.astype(BF16), gw_bf_ref[...], preferred_element_type=F32) + gb_ref[...]
    yg = (y * jax.nn.sigmoid(z)).astype(BF16)
    m = (jnp.dot(att_ref[...], wo_bf_ref[0:aw, :], preferred_element_type=F32)
         + jnp.dot(yg, wo_bf_ref[aw:, :], preferred_element_type=F32))
    o_ref[...] = x_ref[...] + gt_ref[...] * m


def _out_call(layer, att, y, glu_w, glu_b, w_out, x, mod):
    L, d = x.shape
    aw, sw = ATT_WIDTH, SSM_WIDTH
    one = pl.Buffered(1)
    return pl.pallas_call(
        _out_kernel,
        grid=(L // OUT_TM,),
        in_specs=[
            pl.BlockSpec((OUT_TM, aw), lambda i: (i, 0)),
            pl.BlockSpec((OUT_TM, sw), lambda i: (i, 0)),
            pl.BlockSpec((None, sw, sw), lambda i: (layer, 0, 0), pipeline_mode=one),
            pl.BlockSpec((1, sw), lambda i: (0, 0)),
            pl.BlockSpec((None, aw, d), lambda i: (layer, 0, 0), pipeline_mode=one),
            pl.BlockSpec((None, sw, d), lambda i: (layer, 1, 0), pipeline_mode=one),
            pl.BlockSpec((OUT_TM, d), lambda i: (i, 0)),
            _cond_spec(layer, 5, d),
        ],
        out_specs=pl.BlockSpec((OUT_TM, d), lambda i: (i, 0)),
        out_shape=jax.ShapeDtypeStruct((L, d), F32),
        scratch_shapes=[pltpu.VMEM((sw, sw), BF16), pltpu.VMEM((aw + sw, d), BF16)],
        compiler_params=_cparams(("arbitrary",)),
        name="out_proj",
    )(att, y, glu_w, glu_b, w_out, w_out, x, mod)


def _t5_causal_buckets(dist):
    max_exact = N_BUCKETS // 2
    d = jnp.maximum(dist, 1).astype(F32)
    large = max_exact + (jnp.log(d / max_exact) / math.log(MAX_DISTANCE / max_exact)
                         * (N_BUCKETS - max_exact)).astype(jnp.int32)
    large = jnp.minimum(large, N_BUCKETS - 1)
    return jnp.where(dist < max_exact, dist, large)


def _near_bias(rel_bias):
    t = ATT_T
    assert t >= MAX_DISTANCE
    kk = jnp.arange(t, dtype=jnp.int32)[:, None]
    qq = jnp.arange(t, dtype=jnp.int32)[None, :]
    dist = jnp.stack([qq - kk, qq - kk + t], axis=0)
    bucket = _t5_causal_buckets(jnp.maximum(dist, 0))
    rb = rel_bias.astype(F32) * LOG2_E
    val = jnp.zeros((N_ATT_HEADS,) + dist.shape, F32)
    for b in range(N_BUCKETS):
        val = jnp.where((bucket == b)[None], rb[b][:, None, None, None], val)
    return jnp.where((dist >= 0)[None], val, NEG_INF)


def _chunk_interleave(a):
    L, w = a.shape
    return a.reshape(L // SSM_TT, 8, SSM_TT // 8, w).transpose(0, 2, 1, 3).reshape(L, w)


def _chunk_deinterleave(a):
    L, w = a.shape
    return a.reshape(L // SSM_TT, SSM_TT // 8, 8, w).transpose(0, 2, 1, 3).reshape(L, w)


def kernel(x, c, rel_bias, ada_w, ada_b, norm_g, ffn1_w_gate, ffn1_w_up, ffn1_w_down, ffn2_w_gate, ffn2_w_up, ffn2_w_down, w_in, w_out, q_norm_g, k_norm_g, lambda_q1, lambda_k1, lambda_q2, lambda_k2, subln_g, ssm_lambda_re, ssm_lambda_im, ssm_log_step, ssm_b_re, ssm_b_im, ssm_c_re, ssm_c_im, ssm_d, ssm_glu_w, ssm_glu_b):
    b, L, d = x.shape
    assert b == 1 and c.shape == (1, d)
    x2 = x.reshape(L, d)

    mod = _ada_call(c.reshape(d, 1), ada_w, ada_b.reshape(DEPTH, 1, N_COND * d))
    near_bias = _near_bias(rel_bias)
    c_far = (rel_bias.astype(F32)[N_BUCKETS - 1] * LOG2_E).reshape(N_ATT_HEADS, 1, 1)
    w_in_bf = w_in.astype(BF16)
    hd = DIFF_HEAD_DIM
    group_of = jnp.arange(PROJ_NORM_LANES) // hd
    gmat = jnp.where(group_of[:, None] == group_of[None, :], 1.0 / hd, 0.0).astype(BF16)
    n_rep = ATT_WIDTH // hd

    ng3 = norm_g.reshape(DEPTH * 3, 1, d)

    for i in range(DEPTH):
        x2 = _ffn_call(i, 0, x2, ng3, mod, ffn1_w_gate, ffn1_w_up, ffn1_w_down)

        qg = jnp.tile(q_norm_g[i].astype(F32), n_rep)[None] * (hd ** -0.5 * LOG2_E)
        kg = jnp.tile(k_norm_g[i].astype(F32), n_rep)[None]
        q, k, vt, u = _proj_call(i, x2, ng3, mod, w_in_bf, gmat, qg, kg)
        bound = (DIFF_HEAD_DIM * jnp.max(jnp.abs(qg[0, :hd] * kg[0, :hd]))
                 + jnp.max(rel_bias.astype(F32), axis=0) * LOG2_E).reshape(N_ATT_HEADS, 1, 1)
        att = _attn_call(i, q, k, vt, near_bias, c_far, bound, lambda_q1[i][None], lambda_k1[i][None],
                         lambda_q2[i][None], lambda_k2[i][None], subln_g[i].reshape(V_HEAD_DIM, 1))
        bmat, cmat, lam2, lamr2, d3 = _ssm_params(
            ssm_lambda_re[i], ssm_lambda_im[i], ssm_log_step[i], ssm_b_re[i], ssm_b_im[i],
            ssm_c_re[i], ssm_c_im[i], ssm_d[i])
        y = _chunk_deinterleave(_ssm_call(_chunk_interleave(u), bmat, cmat, lam2, lamr2, d3))
        x2 = _out_call(i, att, y, ssm_glu_w, ssm_glu_b[i][None], w_out, x2, mod)

        x2 = _ffn_call(i, 2, x2, ng3, mod, ffn2_w_gate, ffn2_w_up, ffn2_w_down)
    return x2.reshape(b, L, d)
```

```python
import functools
import math

import jax
import jax.numpy as jnp
from jax import lax
from jax.experimental import pallas as pl
from jax.experimental.pallas import tpu as pltpu

D_MODEL = 2048
SEQ = 8192
DEPTH = 2
ATT_WIDTH = 1024
SSM_WIDTH = 1024
DIFF_HEAD_DIM = 64
V_HEAD_DIM = 128
N_ATT_HEADS = 8
SSM_GROUP = 16
N_SSM_GROUPS = 64
SSM_STATE = 64
IN_WIDTH = 4096
D_FF = 5632
N_BUCKETS = 32
MAX_DISTANCE = 128
N_COND = 9
NORM_EPS = 1e-6
SUBLN_EPS = 1e-5
NEG_INF = -1e30
LOG2_E = math.log2(math.e)

F32 = jnp.float32
BF16 = jnp.bfloat16

VMEM_LIMIT_BYTES = 60 * 1024 * 1024

ADA_TN = 2048
FFN_TM = 1024
FFN_TF = 512
FFN_NORM_ROWS = 512
PROJ_TM = 512
PROJ_NORM_LANES = 128
ATT_T = 256
ATT_TQ = 1024
ATT_MIN_DENOM = 2.0 ** -40
SSM_TT = 1024
SSM_SLAB_GROUPS = 8
SSM_STRIP_ROWS = 128
OUT_TM = 512


def _cparams(sem):
    return pltpu.CompilerParams(dimension_semantics=sem, vmem_limit_bytes=VMEM_LIMIT_BYTES)


def _ada_kernel(c_ref, w_ref, b_ref, o_ref):
    c = c_ref[...]
    cs = c * jax.nn.sigmoid(c)
    o_ref[0] = jnp.sum(w_ref[0] * cs, axis=0, keepdims=True) + b_ref[0]


def _ada_call(c_col, ada_w, ada_b3):
    depth, d, n = ada_w.shape
    return pl.pallas_call(
        _ada_kernel,
        grid=(depth, n // ADA_TN),
        in_specs=[
            pl.BlockSpec((d, 1), lambda l, j: (0, 0)),
            pl.BlockSpec((1, d, ADA_TN), lambda l, j: (l, 0, j)),
            pl.BlockSpec((1, 1, ADA_TN), lambda l, j: (l, 0, j)),
        ],
        out_specs=pl.BlockSpec((1, 1, ADA_TN), lambda l, j: (l, 0, j)),
        out_shape=jax.ShapeDtypeStruct((depth, 1, n), F32),
        compiler_params=_cparams(("arbitrary", "arbitrary")),
        name="adaln",
    )(c_col, ada_w, ada_b3)


def _cond_spec(layer, n, d):
    return pl.BlockSpec((None, 1, d), lambda *_: (layer, 0, n))


def _gain_spec(layer, sub, d):
    return pl.BlockSpec((None, 1, d), lambda *_: (layer * 3 + sub, 0, 0))


def _norm_mod(x, g, sc, sh):
    ms = jnp.mean(x * x, axis=-1, keepdims=True)
    return (x * lax.rsqrt(ms + NORM_EPS) * g) * (1.0 + sc) + sh


def _ffn_kernel(x_hbm_ref, ng_ref, sh_ref, sc_ref, gt_ref, wg_ref, wu_ref, wd_ref, o_ref, h_ref,
                x_ref, x_sem):
    i = pl.program_id(0)
    j = pl.program_id(1)

    def x_copy(tile):
        rows = pl.ds(pl.multiple_of(tile * FFN_TM, FFN_TM), FFN_TM)
        return pltpu.make_async_copy(x_hbm_ref.at[rows, :], x_ref, x_sem)

    @pl.when((i == 0) & (j == 0))
    def _():
        x_copy(0).start()

    @pl.when((j == 1) & (i + 1 < pl.num_programs(0)))
    def _():
        x_copy(i + 1).start()

    def weights():
        return (wg_ref[...].astype(BF16), wu_ref[...].astype(BF16), wd_ref[...].astype(BF16))

    def hidden_update(h, wg, wu, wd):
        g = jnp.dot(h, wg, preferred_element_type=F32)
        u = jnp.dot(h, wu, preferred_element_type=F32)
        a = (g * jax.nn.sigmoid(g) * u).astype(BF16)
        return jnp.dot(a, wd, preferred_element_type=F32)

    half_gate = 0.5 * gt_ref[...]

    @pl.when(j == 0)
    def _():
        x_copy(i).wait()
        w = weights()
        for r in range(0, FFN_TM, FFN_NORM_ROWS):
            rows = pl.ds(r, FFN_NORM_ROWS)
            x = x_ref[rows, :]
            h = _norm_mod(x, ng_ref[...], sc_ref[...], sh_ref[...]).astype(BF16)
            h_ref[rows, :] = h
            o_ref[rows, :] = x + half_gate * hidden_update(h, *w)

    @pl.when(j > 0)
    def _():
        o_ref[...] += half_gate * hidden_update(h_ref[...], *weights())


def _ffn_call(layer, sub, x, ng3, mod, wg, wu, wd):
    L, d = x.shape
    dff = wg.shape[-1]
    return pl.pallas_call(
        _ffn_kernel,
        grid=(L // FFN_TM, dff // FFN_TF),
        in_specs=[
            pl.BlockSpec(memory_space=pl.ANY),
            _gain_spec(layer, sub, d), _cond_spec(layer, 3 * sub, d),
            _cond_spec(layer, 3 * sub + 1, d), _cond_spec(layer, 3 * sub + 2, d),
            pl.BlockSpec((None, d, FFN_TF), lambda i, j: (layer, 0, j)),
            pl.BlockSpec((None, d, FFN_TF), lambda i, j: (layer, 0, j)),
            pl.BlockSpec((None, FFN_TF, d), lambda i, j: (layer, j, 0)),
        ],
        out_specs=pl.BlockSpec((FFN_TM, d), lambda i, j: (i, 0)),
        out_shape=jax.ShapeDtypeStruct((L, d), F32),
        scratch_shapes=[pltpu.VMEM((FFN_TM, d), BF16), pltpu.VMEM((FFN_TM, d), F32),
                        pltpu.SemaphoreType.DMA],
        compiler_params=_cparams(("arbitrary", "arbitrary")),
        name="ffn",
    )(x, ng3, mod, mod, mod, wg, wu, wd)


def _proj_kernel(x_ref, ng_ref, sh_ref, sc_ref, w_ref, gm_ref, qg_ref, kg_ref,
                 q_ref, k_ref, vt_ref, u_ref):
    h = _norm_mod(x_ref[...], ng_ref[...], sc_ref[...], sh_ref[...]).astype(BF16)
    aw = ATT_WIDTH

    def head_norm(z, g):
        sq = (z * z).astype(BF16)
        lanes = gm_ref.shape[0]
        ms = jnp.concatenate(
            [jnp.dot(sq[:, c:c + lanes], gm_ref[...], preferred_element_type=F32)
             for c in range(0, aw, lanes)], axis=1)
        return z * lax.rsqrt(ms + NORM_EPS) * g

    q = jnp.dot(h, w_ref[:, 0:aw], preferred_element_type=F32)
    q_ref[...] = head_norm(q, qg_ref[...]).astype(BF16)
    k = jnp.dot(h, w_ref[:, aw:2 * aw], preferred_element_type=F32)
    k_ref[...] = head_norm(k, kg_ref[...]).astype(BF16)
    v = jnp.dot(h, w_ref[:, 2 * aw:3 * aw], preferred_element_type=F32)
    vt_ref[0] = v.T.astype(BF16)
    u_ref[...] = jnp.dot(h, w_ref[:, 3 * aw:4 * aw], preferred_element_type=F32)


def _proj_call(layer, x, ng3, mod, w_in_bf, gmat, qg, kg):
    L, d = x.shape
    aw = ATT_WIDTH
    unit = ATT_TQ
    per_unit = unit // PROJ_TM
    vec_a = pl.BlockSpec((1, aw), lambda i: (0, 0))
    return pl.pallas_call(
        _proj_kernel,
        grid=(L // PROJ_TM,),
        in_specs=[
            pl.BlockSpec((PROJ_TM, d), lambda i: (i, 0)),
            _gain_spec(layer, 1, d), _cond_spec(layer, 3, d), _cond_spec(layer, 4, d),
            pl.BlockSpec((None, d, IN_WIDTH), lambda i: (layer, 0, 0),
                         pipeline_mode=pl.Buffered(1)),
            pl.BlockSpec((PROJ_NORM_LANES, PROJ_NORM_LANES), lambda i: (0, 0)),
            vec_a, vec_a,
        ],
        out_specs=[
            pl.BlockSpec((PROJ_TM, aw), lambda i: (i, 0)),
            pl.BlockSpec((PROJ_TM, aw), lambda i: (i, 0)),
            pl.BlockSpec((1, aw, PROJ_TM), lambda i: (i // per_unit, 0, i % per_unit)),
            pl.BlockSpec((PROJ_TM, SSM_WIDTH), lambda i: (i, 0)),
        ],
        out_shape=[
            jax.ShapeDtypeStruct((L, aw), BF16),
            jax.ShapeDtypeStruct((L, aw), BF16),
            jax.ShapeDtypeStruct((L // unit, aw, unit), BF16),
            jax.ShapeDtypeStruct((L, SSM_WIDTH), F32),
        ],
        compiler_params=_cparams(("arbitrary",)),
        name="in_proj",
    )(x, ng3, mod, mod, w_in_bf, gmat, qg, kg)


def _attn_kernel(q_ref, k_ref, vt_ref, nbias_ref, cfar_ref, bound_ref, lq1_ref, lk1_ref, lq2_ref,
                 lk2_ref, sg_ref, o_ref, acc_ref, m_ref, l_ref, bias_ref, s_ref, p_ref, *, lam_init):
    t = ATT_T
    tq = ATT_TQ
    qb = tq // t
    n_tiles = q_ref.shape[0] // tq

    bound = bound_ref[0]
    far_bias = cfar_ref[0] - bound
    tiles = {2: jnp.broadcast_to(far_bias, (t, t)), 1: nbias_ref[0, 1] - bound,
             0: nbias_ref[0, 0] - bound, -1: jnp.full((t, t), NEG_INF, F32)}
    for d in range(-qb, 3):
        for a in range(qb):
            bias_ref[d + qb, :, a * t:(a + 1) * t] = tiles[max(min(d + a, 2), -1)]

    lam = (jnp.exp(jnp.sum(lq1_ref[...] * lk1_ref[...], keepdims=True))
           - jnp.exp(jnp.sum(lq2_ref[...] * lk2_ref[...], keepdims=True)) + lam_init)
    out_gain = sg_ref[...] * (1.0 - lam_init)

    def load_qcat(qi):
        q = q_ref[pl.ds(pl.multiple_of(qi * tq, tq), tq), :]
        lane = lax.broadcasted_iota(jnp.int32, q.shape, 1)
        zero = jnp.zeros_like(q)
        return jnp.concatenate([jnp.where(lane < DIFF_HEAD_DIM, q, zero),
                                jnp.where(lane >= DIFF_HEAD_DIM, q, zero)], axis=0)

    def scores(blk, qmat):
        off = pl.multiple_of(blk * t, t)
        return lax.dot_general(k_ref[pl.ds(off, t), :], qmat, (((1,), (1,)), ((), ())),
                               preferred_element_type=F32)

    def start_tile(qcat):
        acc_ref[...] = jnp.zeros(acc_ref.shape, F32)
        l_ref[...] = jnp.zeros(l_ref.shape, F32)
        p_ref[...] = jnp.zeros(p_ref.shape, BF16)
        s_ref[...] = scores(0, qcat)

    start_tile(load_qcat(0))

    def tile_body(qi, carry):
        qcat = load_qcat(qi)

        def biased(s, blk):
            bias = bias_ref[jnp.clip(qb * qi - blk, -qb, 2) + qb]
            return s[:, :tq] + bias, s[:, tq:] + bias

        def fast_unit(u, c):
            s_cur = s_ref[...]
            pv = jnp.dot(vt_ref[jnp.maximum(u - 1, 0), :, (qb - 1) * t:qb * t], p_ref[...],
                         preferred_element_type=F32)
            lsum = jnp.zeros((1, 2 * tq), F32)
            for b in range(qb):
                blk = u * qb + b
                s_next = scores(blk + 1, qcat)
                if b < qb - 1:
                    s0, s1 = s_cur[:, :tq] + far_bias, s_cur[:, tq:] + far_bias
                else:
                    s0, s1 = biased(s_cur, blk)
                p = jnp.concatenate([jnp.exp2(s0), jnp.exp2(s1)], axis=1)
                lsum = lsum + jnp.sum(p, axis=0, keepdims=True)
                if b < qb - 1:
                    pv = pv + jnp.dot(vt_ref[u, :, b * t:(b + 1) * t], p.astype(BF16),
                                      preferred_element_type=F32)
                else:
                    p_ref[...] = p.astype(BF16)
                s_cur = s_next
            s_ref[...] = s_cur
            acc_ref[...] += pv
            l_ref[...] += lsum
            return c

        lax.fori_loop(0, qi, fast_unit, 0)

        acc_ref[...] += jnp.dot(vt_ref[jnp.maximum(qi - 1, 0), :, (qb - 1) * t:qb * t], p_ref[...],
                                preferred_element_type=F32)
        s_cur = s_ref[...]
        for b in range(qb):
            w = (qb - b) * t
            if b < qb - 1:
                qn = jnp.concatenate([qcat[(b + 1) * t:tq], qcat[tq + (b + 1) * t:2 * tq]], axis=0)
                s_next = scores(qb * qi + b + 1, qn)
            bias = bias_ref[qb - b, :, b * t:tq]
            p0 = jnp.exp2(s_cur[:, :w] + bias)
            p1 = jnp.exp2(s_cur[:, w:] + bias)
            l_ref[:, b * t:tq] += jnp.sum(p0, axis=0, keepdims=True)
            l_ref[:, tq + b * t:2 * tq] += jnp.sum(p1, axis=0, keepdims=True)
            pv = jnp.dot(vt_ref[qi, :, b * t:(b + 1) * t],
                         jnp.concatenate([p0, p1], axis=1).astype(BF16),
                         preferred_element_type=F32)
            acc_ref[:, b * t:tq] += pv[:, :w]
            acc_ref[:, tq + b * t:2 * tq] += pv[:, w:]
            if b < qb - 1:
                s_cur = s_next

        l_min = jnp.min(l_ref[...], keepdims=True)
        underflow = jnp.logical_not(l_min[0, 0] > ATT_MIN_DENOM)

        @pl.when(underflow)
        def _():
            m_ref[...] = jnp.full(m_ref.shape, NEG_INF, F32)
            l_ref[...] = jnp.zeros(l_ref.shape, F32)
            acc_ref[...] = jnp.zeros(acc_ref.shape, F32)

            def exact_unit(u, c):
                for b in range(qb):
                    blk = u * qb + b
                    s = jnp.concatenate(biased(scores(blk, qcat), blk), axis=1)
                    m_old = m_ref[...]
                    m_new = jnp.maximum(m_old, jnp.max(s, axis=0, keepdims=True))
                    alpha = jnp.exp2(m_old - m_new)
                    p = jnp.exp2(s - m_new)
                    l_ref[...] = alpha * l_ref[...] + jnp.sum(p, axis=0, keepdims=True)
                    acc_ref[...] = alpha * acc_ref[...] + jnp.dot(
                        vt_ref[u, :, b * t:(b + 1) * t], p.astype(BF16),
                        preferred_element_type=F32)
                    m_ref[...] = m_new
                return c

            lax.fori_loop(0, qi + 1, exact_unit, 0)

        on = acc_ref[...] / l_ref[...]
        o = on[:, :tq] - lam * on[:, tq:]
        ms = jnp.mean(o * o, axis=0, keepdims=True)
        o = o * lax.rsqrt(ms + SUBLN_EPS) * out_gain
        o_ref[pl.ds(pl.multiple_of(qi * tq, tq), tq), :] = o.T.astype(BF16)
        start_tile(load_qcat(jnp.minimum(qi + 1, n_tiles - 1)))
        return carry

    lax.fori_loop(0, n_tiles, tile_body, 0)


def _attn_call(layer_idx, q, k, vt, near_bias, c_far, bound, lq1, lk1, lq2, lk2, sg_col):
    L = q.shape[0]
    t, tq = ATT_T, ATT_TQ
    assert tq % t == 0 and L % tq == 0
    lam_init = 0.8 - 0.6 * math.exp(-0.3 * layer_idx)
    lvec = pl.BlockSpec((1, DIFF_HEAD_DIM), lambda h: (0, 0))
    scalar = pl.BlockSpec((1, 1, 1), lambda h: (h, 0, 0))
    return pl.pallas_call(
        functools.partial(_attn_kernel, lam_init=lam_init),
        grid=(N_ATT_HEADS,),
        in_specs=[
            pl.BlockSpec((L, V_HEAD_DIM), lambda h: (0, h)),
            pl.BlockSpec((L, V_HEAD_DIM), lambda h: (0, h)),
            pl.BlockSpec((L // tq, V_HEAD_DIM, tq), lambda h: (0, h, 0)),
            pl.BlockSpec((1, 2, t, t), lambda h: (h, 0, 0, 0)),
            scalar, scalar,
            lvec, lvec, lvec, lvec,
            pl.BlockSpec((V_HEAD_DIM, 1), lambda h: (0, 0)),
        ],
        out_specs=pl.BlockSpec((L, V_HEAD_DIM), lambda h: (0, h)),
        out_shape=jax.ShapeDtypeStruct((L, ATT_WIDTH), BF16),
        scratch_shapes=[
            pltpu.VMEM((V_HEAD_DIM, 2 * tq), F32),
            pltpu.VMEM((1, 2 * tq), F32),
            pltpu.VMEM((1, 2 * tq), F32),
            pltpu.VMEM((tq // t + 3, t, tq), F32),
            pltpu.VMEM((t, 2 * tq), F32),
            pltpu.VMEM((t, 2 * tq), BF16),
        ],
        compiler_params=_cparams(("arbitrary",)),
        name="diff_attn",
    )(q, k, vt, near_bias, c_far, bound, lq1, lk1, lq2, lk2, sg_col)


def _ssm_kernel(u_ref, un_ref, bm_ref, cm_ref, lam_ref, lamr_ref, d_ref, y_ref,
                bu_a_ref, bu_b_ref, carry_ref):
    ns = SSM_SLAB_GROUPS * SSM_STATE
    tt = SSM_TT
    strip = SSM_STRIP_ROWS
    n_strips = tt // strip
    bm = bm_ref[0]

    @pl.when(pl.program_id(1) == 0)
    def _():
        carry_ref[...] = jnp.zeros_like(carry_ref)
        bu_a_ref[...] = jnp.dot(u_ref[0:tt, :].astype(BF16), bm, preferred_element_type=F32)

    lr = jnp.broadcast_to(lam_ref[0, 0:1, :], (8, ns))
    li = jnp.broadcast_to(lam_ref[0, 1:2, :], (8, ns))
    pr = lamr_ref[0, 0:1, :]
    pi = lamr_ref[0, 1:2, :]
    row_id = lax.broadcasted_iota(jnp.int32, (8, ns), 0)
    z = jnp.zeros((8, ns), F32)

    def tile(cur_ref, nxt_ref, row0, next_u):
        def advance(r, sr, si, store):
            rows = pl.ds(8 * r, 8)
            nsr = lr * sr - li * si + cur_ref[rows, 0:ns]
            nsi = lr * si + li * sr + cur_ref[rows, ns:2 * ns]
            if store:
                cur_ref[rows, 0:ns] = nsr
                cur_ref[rows, ns:2 * ns] = nsi
            return nsr, nsi

        sr, si = z, z
        for k in range(n_strips):
            nxt_ref[pl.ds(k * strip, strip), :] = jnp.dot(
                next_u(k * strip, strip).astype(BF16), bm, preferred_element_type=F32)
            for r in range(k * strip // 8, (k + 1) * strip // 8):
                sr, si = advance(r, sr, si, False)
        er, ei = sr, si

        cr = carry_ref[0:1, 0:ns]
        ci = carry_ref[0:1, ns:2 * ns]
        sr, si = z, z
        for c in range(8):
            sr = jnp.where(row_id == c, cr, sr)
            si = jnp.where(row_id == c, ci, si)
            cr, ci = (pr * cr - pi * ci + er[c:c + 1, :], pr * ci + pi * cr + ei[c:c + 1, :])
        carry_ref[0:1, 0:ns] = cr
        carry_ref[0:1, ns:2 * ns] = ci

        def emit(k):
            rows = pl.ds(k * strip, strip)
            out_rows = pl.ds(row0 + k * strip, strip)
            y = (jnp.dot(cur_ref[rows, :].astype(BF16), cm_ref[0], preferred_element_type=F32)
                 + d_ref[0] * u_ref[out_rows, :])
            y_ref[out_rows, :] = y

        for k in range(n_strips):
            if k > 0:
                emit(k - 1)
            for r in range(k * strip // 8, (k + 1) * strip // 8):
                sr, si = advance(r, sr, si, True)
        emit(n_strips - 1)

    tile(bu_a_ref, bu_b_ref, 0, lambda r, n: u_ref[pl.ds(tt + r, n), :])
    tile(bu_b_ref, bu_a_ref, tt, lambda r, n: un_ref[pl.ds(r, n), :])


def _ssm_call(u_perm, bmat, cmat, lam2, lamr2, d3):
    L = u_perm.shape[0]
    n_slab = N_SSM_GROUPS // SSM_SLAB_GROUPS
    n_tiles = L // SSM_TT
    assert n_tiles % 2 == 0
    ns = SSM_SLAB_GROUPS * SSM_STATE
    cw = SSM_SLAB_GROUPS * SSM_GROUP
    return pl.pallas_call(
        _ssm_kernel,
        grid=(n_slab, n_tiles // 2),
        in_specs=[
            pl.BlockSpec((2 * SSM_TT, cw), lambda s, p: (p, s)),
            pl.BlockSpec((SSM_TT, cw), lambda s, p: (jnp.minimum(2 * p + 2, n_tiles - 1), s)),
            pl.BlockSpec((1, cw, 2 * ns), lambda s, p: (s, 0, 0)),
            pl.BlockSpec((1, 2 * ns, cw), lambda s, p: (s, 0, 0)),
            pl.BlockSpec((1, 2, ns), lambda s, p: (s, 0, 0)),
            pl.BlockSpec((1, 2, ns), lambda s, p: (s, 0, 0)),
            pl.BlockSpec((1, 1, cw), lambda s, p: (s, 0, 0)),
        ],
        out_specs=pl.BlockSpec((2 * SSM_TT, cw), lambda s, p: (p, s)),
        out_shape=jax.ShapeDtypeStruct((L, SSM_WIDTH), F32),
        scratch_shapes=[
            pltpu.VMEM((SSM_TT, 2 * ns), F32),
            pltpu.VMEM((SSM_TT, 2 * ns), F32),
            pltpu.VMEM((1, 2 * ns), F32),
        ],
        compiler_params=_cparams(("arbitrary", "arbitrary")),
        name="s5_scan",
    )(u_perm, u_perm, bmat, cmat, lam2, lamr2, d3)


def _ssm_params(lam_re, lam_im, log_step, b_re, b_im, c_re, c_im, d):
    g, p, hc = N_SSM_GROUPS, SSM_STATE, SSM_GROUP
    sg = SSM_SLAB_GROUPS
    n_slab = g // sg
    lam = lax.complex(jnp.minimum(lam_re.astype(F32), -1e-4), lam_im.astype(F32))
    step = jnp.exp(log_step.astype(F32))[:, None]
    lam_bar = jnp.exp(lam * step)
    lam_bar_r = jnp.exp(lam * (step * (SSM_TT // 8)))
    b_bar = ((lam_bar - 1.0) / lam)[:, :, None] * lax.complex(b_re.astype(F32), b_im.astype(F32))
    eye = jnp.eye(sg, dtype=F32)

    def b_block(part):
        z = part.reshape(n_slab, sg, p, hc).transpose(0, 1, 3, 2)
        return (z[:, :, :, None, :] * eye[None, :, None, :, None]).reshape(n_slab, sg * hc, sg * p)

    def c_block(part):
        z = part.reshape(n_slab, sg, hc, p).transpose(0, 1, 3, 2)
        return (z[:, :, :, None, :] * eye[None, :, None, :, None]).reshape(n_slab, sg * p, sg * hc)

    bmat = jnp.concatenate([b_block(jnp.real(b_bar)), b_block(jnp.imag(b_bar))], axis=-1)
    cmat = jnp.concatenate([c_block(c_re.astype(F32)), c_block(-c_im.astype(F32))], axis=1)

    def rows(zc):
        return jnp.stack([jnp.real(zc).reshape(n_slab, sg * p),
                          jnp.imag(zc).reshape(n_slab, sg * p)], axis=1)

    d3 = d.astype(F32).reshape(n_slab, 1, sg * hc)
    return bmat.astype(BF16), cmat.astype(BF16), rows(lam_bar), rows(lam_bar_r), d3


def _out_kernel(att_ref, y_ref, gw_ref, gb_ref, woa_ref, wob_ref, x_ref, gt_ref, o_ref,
                gw_bf_ref, wo_bf_ref):
    aw = ATT_WIDTH

    @pl.when(pl.program_id(0) == 0)
    def _():
        gw_bf_ref[...] = gw_ref[...].astype(BF16)
        wo_bf_ref[0:aw, :] = woa_ref[...].astype(BF16)
        wo_bf_ref[aw:, :] = wob_ref[...].astype(BF16)

    y = jax.nn.gelu(y_ref[...])
    z = jnp.dot(y.astype(BF16), gw_bf_ref[...], preferred_element_type=F32) + gb_ref[...]
    yg = (y * jax.nn.sigmoid(z)).astype(BF16)
    m = (jnp.dot(att_ref[...], wo_bf_ref[0:aw, :], preferred_element_type=F32)
         + jnp.dot(yg, wo_bf_ref[aw:, :], preferred_element_type=F32))
    o_ref[...] = x_ref[...] + gt_ref[...] * m


def _out_call(layer, att, y, glu_w, glu_b, w_out, x, mod):
    L, d = x.shape
    aw, sw = ATT_WIDTH, SSM_WIDTH
    one = pl.Buffered(1)
    return pl.pallas_call(
        _out_kernel,
        grid=(L // OUT_TM,),
        in_specs=[
            pl.BlockSpec((OUT_TM, aw), lambda i: (i, 0)),
            pl.BlockSpec((OUT_TM, sw), lambda i: (i, 0)),
            pl.BlockSpec((None, sw, sw), lambda i: (layer, 0, 0), pipeline_mode=one),
            pl.BlockSpec((1, sw), lambda i: (0, 0)),
            pl.BlockSpec((None, aw, d), lambda i: (layer, 0, 0), pipeline_mode=one),
            pl.BlockSpec((None, sw, d), lambda i: (layer, 1, 0), pipeline_mode=one),
            pl.BlockSpec((OUT_TM, d), lambda i: (i, 0)),
            _cond_spec(layer, 5, d),
        ],
        out_specs=pl.BlockSpec((OUT_TM, d), lambda i: (i, 0)),
        out_shape=jax.ShapeDtypeStruct((L, d), F32),
        scratch_shapes=[pltpu.VMEM((sw, sw), BF16), pltpu.VMEM((aw + sw, d), BF16)],
        compiler_params=_cparams(("arbitrary",)),
        name="out_proj",
    )(att, y, glu_w, glu_b, w_out, w_out, x, mod)


def _t5_causal_buckets(dist):
    max_exact = N_BUCKETS // 2
    d = jnp.maximum(dist, 1).astype(F32)
    large = max_exact + (jnp.log(d / max_exact) / math.log(MAX_DISTANCE / max_exact)
                         * (N_BUCKETS - max_exact)).astype(jnp.int32)
    large = jnp.minimum(large, N_BUCKETS - 1)
    return jnp.where(dist < max_exact, dist, large)


def _near_bias(rel_bias):
    t = ATT_T
    assert t >= MAX_DISTANCE
    kk = jnp.arange(t, dtype=jnp.int32)[:, None]
    qq = jnp.arange(t, dtype=jnp.int32)[None, :]
    dist = jnp.stack([qq - kk, qq - kk + t], axis=0)
    bucket = _t5_causal_buckets(jnp.maximum(dist, 0))
    rb = rel_bias.astype(F32) * LOG2_E
    val = jnp.zeros((N_ATT_HEADS,) + dist.shape, F32)
    for b in range(N_BUCKETS):
        val = jnp.where((bucket == b)[None], rb[b][:, None, None, None], val)
    return jnp.where((dist >= 0)[None], val, NEG_INF)


def _chunk_interleave(a):
    L, w = a.shape
    return a.reshape(L // SSM_TT, 8, SSM_TT // 8, w).transpose(0, 2, 1, 3).reshape(L, w)


def _chunk_deinterleave(a):
    L, w = a.shape
    return a.reshape(L // SSM_TT, SSM_TT // 8, 8, w).transpose(0, 2, 1, 3).reshape(L, w)


def kernel(x, c, rel_bias, ada_w, ada_b, norm_g, ffn1_w_gate, ffn1_w_up, ffn1_w_down, ffn2_w_gate, ffn2_w_up, ffn2_w_down, w_in, w_out, q_norm_g, k_norm_g, lambda_q1, lambda_k1, lambda_q2, lambda_k2, subln_g, ssm_lambda_re, ssm_lambda_im, ssm_log_step, ssm_b_re, ssm_b_im, ssm_c_re, ssm_c_im, ssm_d, ssm_glu_w, ssm_glu_b):
    b, L, d = x.shape
    assert b == 1 and c.shape == (1, d)
    x2 = x.reshape(L, d)

    mod = _ada_call(c.reshape(d, 1), ada_w, ada_b.reshape(DEPTH, 1, N_COND * d))
    near_bias = _near_bias(rel_bias)
    c_far = (rel_bias.astype(F32)[N_BUCKETS - 1] * LOG2_E).reshape(N_ATT_HEADS, 1, 1)
    w_in_bf = w_in.astype(BF16)
    hd = DIFF_HEAD_DIM
    group_of = jnp.arange(PROJ_NORM_LANES) // hd
    gmat = jnp.where(group_of[:, None] == group_of[None, :], 1.0 / hd, 0.0).astype(BF16)
    n_rep = ATT_WIDTH // hd

    ng3 = norm_g.reshape(DEPTH * 3, 1, d)

    for i in range(DEPTH):
        x2 = _ffn_call(i, 0, x2, ng3, mod, ffn1_w_gate, ffn1_w_up, ffn1_w_down)

        qg = jnp.tile(q_norm_g[i].astype(F32), n_rep)[None] * (hd ** -0.5 * LOG2_E)
        kg = jnp.tile(k_norm_g[i].astype(F32), n_rep)[None]
        q, k, vt, u = _proj_call(i, x2, ng3, mod, w_in_bf, gmat, qg, kg)
        bound = (DIFF_HEAD_DIM * jnp.max(jnp.abs(qg[0, :hd] * kg[0, :hd]))
                 + jnp.max(rel_bias.astype(F32), axis=0) * LOG2_E).reshape(N_ATT_HEADS, 1, 1)
        att = _attn_call(i, q, k, vt, near_bias, c_far, bound, lambda_q1[i][None], lambda_k1[i][None],
                         lambda_q2[i][None], lambda_k2[i][None], subln_g[i].reshape(V_HEAD_DIM, 1))
        bmat, cmat, lam2, lamr2, d3 = _ssm_params(
            ssm_lambda_re[i], ssm_lambda_im[i], ssm_log_step[i], ssm_b_re[i], ssm_b_im[i],
            ssm_c_re[i], ssm_c_im[i], ssm_d[i])
        y = _chunk_deinterleave(_ssm_call(_chunk_interleave(u), bmat, cmat, lam2, lamr2, d3))
        x2 = _out_call(i, att, y, ssm_glu_w, ssm_glu_b[i][None], w_out, x2, mod)

        x2 = _ffn_call(i, 2, x2, ng3, mod, ffn2_w_gate, ffn2_w_up, ffn2_w_down)
    return x2.reshape(b, L, d)
```

```python
import functools
import math

import jax
import jax.numpy as jnp
from jax import lax
from jax.experimental import pallas as pl
from jax.experimental.pallas import tpu as pltpu

D_MODEL = 2048
SEQ = 8192
DEPTH = 2
ATT_WIDTH = 1024
SSM_WIDTH = 1024
DIFF_HEAD_DIM = 64
V_HEAD_DIM = 128
N_ATT_HEADS = 8
SSM_GROUP = 16
N_SSM_GROUPS = 64
SSM_STATE = 64
IN_WIDTH = 4096
D_FF = 5632
N_BUCKETS = 32
MAX_DISTANCE = 128
N_COND = 9
NORM_EPS = 1e-6
SUBLN_EPS = 1e-5
NEG_INF = -1e30
LOG2_E = math.log2(math.e)

F32 = jnp.float32
BF16 = jnp.bfloat16

VMEM_LIMIT_BYTES = 62 * 1024 * 1024

ADA_TN = 2048
FFN_TM = 1024
FFN_TF = 512
FFN_NORM_ROWS = 512
PROJ_TM = 512
PROJ_NORM_LANES = 128
ATT_T = 256
ATT_TQ = 1024
ATT_MIN_DENOM = 2.0 ** -40
SSM_TT = 1024
SSM_SLAB_GROUPS = 8
SSM_STRIP_ROWS = 128
OUT_TM = 512


def _cparams(sem):
    return pltpu.CompilerParams(dimension_semantics=sem, vmem_limit_bytes=VMEM_LIMIT_BYTES)


def _ada_kernel(c_ref, w_ref, b_ref, o_ref):
    c = c_ref[...]
    cs = c * jax.nn.sigmoid(c)
    o_ref[0] = jnp.sum(w_ref[0] * cs, axis=0, keepdims=True) + b_ref[0]


def _ada_call(c_col, ada_w, ada_b3):
    depth, d, n = ada_w.shape
    return pl.pallas_call(
        _ada_kernel,
        grid=(depth, n // ADA_TN),
        in_specs=[
            pl.BlockSpec((d, 1), lambda l, j: (0, 0)),
            pl.BlockSpec((1, d, ADA_TN), lambda l, j: (l, 0, j)),
            pl.BlockSpec((1, 1, ADA_TN), lambda l, j: (l, 0, j)),
        ],
        out_specs=pl.BlockSpec((1, 1, ADA_TN), lambda l, j: (l, 0, j)),
        out_shape=jax.ShapeDtypeStruct((depth, 1, n), F32),
        compiler_params=_cparams(("arbitrary", "arbitrary")),
        name="adaln",
    )(c_col, ada_w, ada_b3)


def _cond_spec(layer, n, d):
    return pl.BlockSpec((None, 1, d), lambda *_: (layer, 0, n))


def _gain_spec(layer, sub, d):
    return pl.BlockSpec((None, 1, d), lambda *_: (layer * 3 + sub, 0, 0))


def _norm_mod(x, g, sc, sh):
    ms = jnp.mean(x * x, axis=-1, keepdims=True)
    return (x * lax.rsqrt(ms + NORM_EPS) * g) * (1.0 + sc) + sh


def _ffn_kernel(x_hbm_ref, ng_ref, sh_ref, sc_ref, gt_ref, wg_hbm_ref, wu_hbm_ref, wd_hbm_ref,
                o_ref, h_ref, x_ref, wg_ref, wu_ref, wd_ref, x_sem, w_sem, *, layer):
    i = pl.program_id(0)
    n_i = pl.num_programs(0)
    nj = wg_hbm_ref.shape[2] // FFN_TF
    base = (i * nj) % 2

    def x_copy(tile):
        rows = pl.ds(pl.multiple_of(tile * FFN_TM, FFN_TM), FFN_TM)
        return pltpu.make_async_copy(x_hbm_ref.at[rows, :], x_ref, x_sem)

    def w_copies(jt, slot):
        cols = pl.ds(pl.multiple_of(jt * FFN_TF, FFN_TF), FFN_TF)
        return (pltpu.make_async_copy(wg_hbm_ref.at[layer, :, cols], wg_ref.at[slot], w_sem.at[0, slot]),
                pltpu.make_async_copy(wu_hbm_ref.at[layer, :, cols], wu_ref.at[slot], w_sem.at[1, slot]),
                pltpu.make_async_copy(wd_hbm_ref.at[layer, cols, :], wd_ref.at[slot], w_sem.at[2, slot]))

    def start_w(jt, slot):
        for cp in w_copies(jt, slot):
            cp.start()

    def wait_w(jt, slot):
        for cp in w_copies(jt, slot):
            cp.wait()

    def hidden_update(h, slot):
        wg = wg_ref[slot].astype(BF16)
        wu = wu_ref[slot].astype(BF16)
        wd = wd_ref[slot].astype(BF16)
        g = jnp.dot(h, wg, preferred_element_type=F32)
        u = jnp.dot(h, wu, preferred_element_type=F32)
        a = (g * jax.nn.sigmoid(g) * u).astype(BF16)
        return jnp.dot(a, wd, preferred_element_type=F32)

    half_gate = 0.5 * gt_ref[...]

    @pl.when(i == 0)
    def _():
        x_copy(0).start()
        start_w(0, 0)

    start_w(1, 1 - base)
    x_copy(i).wait()
    wait_w(0, base)
    for r in range(0, FFN_TM, FFN_NORM_ROWS):
        rows = pl.ds(r, FFN_NORM_ROWS)
        x = x_ref[rows, :]
        h = _norm_mod(x, ng_ref[...], sc_ref[...], sh_ref[...]).astype(BF16)
        h_ref[rows, :] = h
        o_ref[rows, :] = x + half_gate * hidden_update(h, base)

    @pl.when(i + 1 < n_i)
    def _():
        x_copy(i + 1).start()

    def hidden_tile(jt, carry):
        slot = (base + jt) % 2
        more_here = jt + 1 < nj

        @pl.when(more_here | (i + 1 < n_i))
        def _():
            start_w(jnp.where(more_here, jt + 1, 0), 1 - slot)

        wait_w(jt, slot)
        o_ref[...] += half_gate * hidden_update(h_ref[...], slot)
        return carry

    lax.fori_loop(1, nj, hidden_tile, 0)


def _ffn_call(layer, sub, x, ng3, mod, wg, wu, wd):
    L, d = x.shape
    dff = wg.shape[-1]
    assert dff % FFN_TF == 0 and dff // FFN_TF >= 2 and L % FFN_TM == 0
    hbm = pl.BlockSpec(memory_space=pl.ANY)
    return pl.pallas_call(
        functools.partial(_ffn_kernel, layer=layer),
        grid=(L // FFN_TM,),
        in_specs=[
            hbm,
            _gain_spec(layer, sub, d), _cond_spec(layer, 3 * sub, d),
            _cond_spec(layer, 3 * sub + 1, d), _cond_spec(layer, 3 * sub + 2, d),
            hbm, hbm, hbm,
        ],
        out_specs=pl.BlockSpec((FFN_TM, d), lambda i: (i, 0)),
        out_shape=jax.ShapeDtypeStruct((L, d), F32),
        scratch_shapes=[
            pltpu.VMEM((FFN_TM, d), BF16),
            pltpu.VMEM((FFN_TM, d), F32),
            pltpu.VMEM((2, d, FFN_TF), F32),
            pltpu.VMEM((2, d, FFN_TF), F32),
            pltpu.VMEM((2, FFN_TF, d), F32),
            pltpu.SemaphoreType.DMA,
            pltpu.SemaphoreType.DMA((3, 2)),
        ],
        compiler_params=_cparams(("arbitrary",)),
        name="ffn",
    )(x, ng3, mod, mod, mod, wg, wu, wd)


def _proj_kernel(x_ref, ng_ref, sh_ref, sc_ref, w_ref, gm_ref, qg_ref, kg_ref,
                 q_ref, k_ref, vt_ref, u_ref):
    h = _norm_mod(x_ref[...], ng_ref[...], sc_ref[...], sh_ref[...]).astype(BF16)
    aw = ATT_WIDTH

    def head_norm(z, g):
        sq = (z * z).astype(BF16)
        lanes = gm_ref.shape[0]
        ms = jnp.concatenate(
            [jnp.dot(sq[:, c:c + lanes], gm_ref[...], preferred_element_type=F32)
             for c in range(0, aw, lanes)], axis=1)
        return z * lax.rsqrt(ms + NORM_EPS) * g

    q = jnp.dot(h, w_ref[:, 0:aw], preferred_element_type=F32)
    q_ref[...] = head_norm(q, qg_ref[...]).astype(BF16)
    k = jnp.dot(h, w_ref[:, aw:2 * aw], preferred_element_type=F32)
    k_ref[...] = head_norm(k, kg_ref[...]).astype(BF16)
    v = jnp.dot(h, w_ref[:, 2 * aw:3 * aw], preferred_element_type=F32)
    vt_ref[0] = v.T.astype(BF16)
    u_ref[...] = jnp.dot(h, w_ref[:, 3 * aw:4 * aw], preferred_element_type=F32)


def _proj_call(layer, x, ng3, mod, w_in_bf, gmat, qg, kg):
    L, d = x.shape
    aw = ATT_WIDTH
    unit = ATT_TQ
    per_unit = unit // PROJ_TM
    vec_a = pl.BlockSpec((1, aw), lambda i: (0, 0))
    return pl.pallas_call(
        _proj_kernel,
        grid=(L // PROJ_TM,),
        in_specs=[
            pl.BlockSpec((PROJ_TM, d), lambda i: (i, 0)),
            _gain_spec(layer, 1, d), _cond_spec(layer, 3, d), _cond_spec(layer, 4, d),
            pl.BlockSpec((None, d, IN_WIDTH), lambda i: (layer, 0, 0),
                         pipeline_mode=pl.Buffered(1)),
            pl.BlockSpec((PROJ_NORM_LANES, PROJ_NORM_LANES), lambda i: (0, 0)),
            vec_a, vec_a,
        ],
        out_specs=[
            pl.BlockSpec((PROJ_TM, aw), lambda i: (i, 0)),
            pl.BlockSpec((PROJ_TM, aw), lambda i: (i, 0)),
            pl.BlockSpec((1, aw, PROJ_TM), lambda i: (i // per_unit, 0, i % per_unit)),
            pl.BlockSpec((PROJ_TM, SSM_WIDTH), lambda i: (i, 0)),
        ],
        out_shape=[
            jax.ShapeDtypeStruct((L, aw), BF16),
            jax.ShapeDtypeStruct((L, aw), BF16),
            jax.ShapeDtypeStruct((L // unit, aw, unit), BF16),
            jax.ShapeDtypeStruct((L, SSM_WIDTH), F32),
        ],
        compiler_params=_cparams(("arbitrary",)),
        name="in_proj",
    )(x, ng3, mod, mod, w_in_bf, gmat, qg, kg)


def _attn_kernel(q_ref, k_ref, vt_ref, nbias_ref, cfar_ref, bound_ref, lq1_ref, lk1_ref, lq2_ref,
                 lk2_ref, sg_ref, o_ref, acc_ref, m_ref, l_ref, bias_ref, s_ref, p_ref, *, lam_init):
    t = ATT_T
    tq = ATT_TQ
    qb = tq // t
    n_tiles = q_ref.shape[0] // tq

    bound = bound_ref[0]
    far_bias = cfar_ref[0] - bound
    tiles = {2: jnp.broadcast_to(far_bias, (t, t)), 1: nbias_ref[0, 1] - bound,
             0: nbias_ref[0, 0] - bound, -1: jnp.full((t, t), NEG_INF, F32)}
    for d in range(-qb, 3):
        for a in range(qb):
            bias_ref[d + qb, :, a * t:(a + 1) * t] = tiles[max(min(d + a, 2), -1)]

    lam = (jnp.exp(jnp.sum(lq1_ref[...] * lk1_ref[...], keepdims=True))
           - jnp.exp(jnp.sum(lq2_ref[...] * lk2_ref[...], keepdims=True)) + lam_init)
    out_gain = sg_ref[...] * (1.0 - lam_init)

    def load_qcat(qi):
        q = q_ref[pl.ds(pl.multiple_of(qi * tq, tq), tq), :]
        lane = lax.broadcasted_iota(jnp.int32, q.shape, 1)
        zero = jnp.zeros_like(q)
        return jnp.concatenate([jnp.where(lane < DIFF_HEAD_DIM, q, zero),
                                jnp.where(lane >= DIFF_HEAD_DIM, q, zero)], axis=0)

    def scores(blk, qmat):
        off = pl.multiple_of(blk * t, t)
        return lax.dot_general(k_ref[pl.ds(off, t), :], qmat, (((1,), (1,)), ((), ())),
                               preferred_element_type=F32)

    def start_tile(qcat):
        acc_ref[...] = jnp.zeros(acc_ref.shape, F32)
        l_ref[...] = jnp.zeros(l_ref.shape, F32)
        p_ref[...] = jnp.zeros(p_ref.shape, BF16)
        s_ref[...] = scores(0, qcat)

    start_tile(load_qcat(0))

    def tile_body(qi, carry):
        qcat = load_qcat(qi)

        def biased(s, blk):
            bias = bias_ref[jnp.clip(qb * qi - blk, -qb, 2) + qb]
            return s[:, :tq] + bias, s[:, tq:] + bias

        def fast_unit(u, c):
            s_cur = s_ref[...]
            pv = jnp.dot(vt_ref[jnp.maximum(u - 1, 0), :, (qb - 1) * t:qb * t], p_ref[...],
                         preferred_element_type=F32)
            lsum = jnp.zeros((1, 2 * tq), F32)
            for b in range(qb):
                blk = u * qb + b
                s_next = scores(blk + 1, qcat)
                if b < qb - 1:
                    s0, s1 = s_cur[:, :tq] + far_bias, s_cur[:, tq:] + far_bias
                else:
                    s0, s1 = biased(s_cur, blk)
                p = jnp.concatenate([jnp.exp2(s0), jnp.exp2(s1)], axis=1)
                lsum = lsum + jnp.sum(p, axis=0, keepdims=True)
                if b < qb - 1:
                    pv = pv + jnp.dot(vt_ref[u, :, b * t:(b + 1) * t], p.astype(BF16),
                                      preferred_element_type=F32)
                else:
                    p_ref[...] = p.astype(BF16)
                s_cur = s_next
            s_ref[...] = s_cur
            acc_ref[...] += pv
            l_ref[...] += lsum
            return c

        lax.fori_loop(0, qi, fast_unit, 0)

        acc_ref[...] += jnp.dot(vt_ref[jnp.maximum(qi - 1, 0), :, (qb - 1) * t:qb * t], p_ref[...],
                                preferred_element_type=F32)
        s_cur = s_ref[...]
        for b in range(qb):
            w = (qb - b) * t
            if b < qb - 1:
                qn = jnp.concatenate([qcat[(b + 1) * t:tq], qcat[tq + (b + 1) * t:2 * tq]], axis=0)
                s_next = scores(qb * qi + b + 1, qn)
            bias = bias_ref[qb - b, :, b * t:tq]
            p0 = jnp.exp2(s_cur[:, :w] + bias)
            p1 = jnp.exp2(s_cur[:, w:] + bias)
            l_ref[:, b * t:tq] += jnp.sum(p0, axis=0, keepdims=True)
            l_ref[:, tq + b * t:2 * tq] += jnp.sum(p1, axis=0, keepdims=True)
            pv = jnp.dot(vt_ref[qi, :, b * t:(b + 1) * t],
                         jnp.concatenate([p0, p1], axis=1).astype(BF16),
                         preferred_element_type=F32)
            acc_ref[:, b * t:tq] += pv[:, :w]
            acc_ref[:, tq + b * t:2 * tq] += pv[:, w:]
            if b < qb - 1:
                s_cur = s_next

        l_min = jnp.min(l_ref[...], keepdims=True)
        underflow = jnp.logical_not(l_min[0, 0] > ATT_MIN_DENOM)

        @pl.when(underflow)
        def _():
            m_ref[...] = jnp.full(m_ref.shape, NEG_INF, F32)
            l_ref[...] = jnp.zeros(l_ref.shape, F32)
            acc_ref[...] = jnp.zeros(acc_ref.shape, F32)

            def exact_unit(u, c):
                for b in range(qb):
                    blk = u * qb + b
                    s = jnp.concatenate(biased(scores(blk, qcat), blk), axis=1)
                    m_old = m_ref[...]
                    m_new = jnp.maximum(m_old, jnp.max(s, axis=0, keepdims=True))
                    alpha = jnp.exp2(m_old - m_new)
                    p = jnp.exp2(s - m_new)
                    l_ref[...] = alpha * l_ref[...] + jnp.sum(p, axis=0, keepdims=True)
                    acc_ref[...] = alpha * acc_ref[...] + jnp.dot(
                        vt_ref[u, :, b * t:(b + 1) * t], p.astype(BF16),
                        preferred_element_type=F32)
                    m_ref[...] = m_new
                return c

            lax.fori_loop(0, qi + 1, exact_unit, 0)

        on = acc_ref[...] / l_ref[...]
        o = on[:, :tq] - lam * on[:, tq:]
        ms = jnp.mean(o * o, axis=0, keepdims=True)
        o = o * lax.rsqrt(ms + SUBLN_EPS) * out_gain
        o_ref[pl.ds(pl.multiple_of(qi * tq, tq), tq), :] = o.T.astype(BF16)
        start_tile(load_qcat(jnp.minimum(qi + 1, n_tiles - 1)))
        return carry

    lax.fori_loop(0, n_tiles, tile_body, 0)


def _attn_call(layer_idx, q, k, vt, near_bias, c_far, bound, lq1, lk1, lq2, lk2, sg_col):
    L = q.shape[0]
    t, tq = ATT_T, ATT_TQ
    assert tq % t == 0 and L % tq == 0
    lam_init = 0.8 - 0.6 * math.exp(-0.3 * layer_idx)
    lvec = pl.BlockSpec((1, DIFF_HEAD_DIM), lambda h: (0, 0))
    scalar = pl.BlockSpec((1, 1, 1), lambda h: (h, 0, 0))
    return pl.pallas_call(
        functools.partial(_attn_kernel, lam_init=lam_init),
        grid=(N_ATT_HEADS,),
        in_specs=[
            pl.BlockSpec((L, V_HEAD_DIM), lambda h: (0, h)),
            pl.BlockSpec((L, V_HEAD_DIM), lambda h: (0, h)),
            pl.BlockSpec((L // tq, V_HEAD_DIM, tq), lambda h: (0, h, 0)),
            pl.BlockSpec((1, 2, t, t), lambda h: (h, 0, 0, 0)),
            scalar, scalar,
            lvec, lvec, lvec, lvec,
            pl.BlockSpec((V_HEAD_DIM, 1), lambda h: (0, 0)),
        ],
        out_specs=pl.BlockSpec((L, V_HEAD_DIM), lambda h: (0, h)),
        out_shape=jax.ShapeDtypeStruct((L, ATT_WIDTH), BF16),
        scratch_shapes=[
            pltpu.VMEM((V_HEAD_DIM, 2 * tq), F32),
            pltpu.VMEM((1, 2 * tq), F32),
            pltpu.VMEM((1, 2 * tq), F32),
            pltpu.VMEM((tq // t + 3, t, tq), F32),
            pltpu.VMEM((t, 2 * tq), F32),
            pltpu.VMEM((t, 2 * tq), BF16),
        ],
        compiler_params=_cparams(("arbitrary",)),
        name="diff_attn",
    )(q, k, vt, near_bias, c_far, bound, lq1, lk1, lq2, lk2, sg_col)


def _ssm_kernel(u_ref, un_ref, bm_ref, cm_ref, lam_ref, lamr_ref, d_ref, y_ref,
                bu_a_ref, bu_b_ref, carry_ref):
    ns = SSM_SLAB_GROUPS * SSM_STATE
    tt = SSM_TT
    strip = SSM_STRIP_ROWS
    n_strips = tt // strip
    bm = bm_ref[0]

    @pl.when(pl.program_id(1) == 0)
    def _():
        carry_ref[...] = jnp.zeros_like(carry_ref)
        bu_a_ref[...] = jnp.dot(u_ref[0:tt, :].astype(BF16), bm, preferred_element_type=F32)

    lr = jnp.broadcast_to(lam_ref[0, 0:1, :], (8, ns))
    li = jnp.broadcast_to(lam_ref[0, 1:2, :], (8, ns))
    pr = lamr_ref[0, 0:1, :]
    pi = lamr_ref[0, 1:2, :]
    row_id = lax.broadcasted_iota(jnp.int32, (8, ns), 0)
    z = jnp.zeros((8, ns), F32)

    def tile(cur_ref, nxt_ref, row0, next_u):
        def advance(r, sr, si, store):
            rows = pl.ds(8 * r, 8)
            nsr = lr * sr - li * si + cur_ref[rows, 0:ns]
            nsi = lr * si + li * sr + cur_ref[rows, ns:2 * ns]
            if store:
                cur_ref[rows, 0:ns] = nsr
                cur_ref[rows, ns:2 * ns] = nsi
            return nsr, nsi

        sr, si = z, z
        for k in range(n_strips):
            nxt_ref[pl.ds(k * strip, strip), :] = jnp.dot(
                next_u(k * strip, strip).astype(BF16), bm, preferred_element_type=F32)
            for r in range(k * strip // 8, (k + 1) * strip // 8):
                sr, si = advance(r, sr, si, False)
        er, ei = sr, si

        cr = carry_ref[0:1, 0:ns]
        ci = carry_ref[0:1, ns:2 * ns]
        sr, si = z, z
        for c in range(8):
            sr = jnp.where(row_id == c, cr, sr)
            si = jnp.where(row_id == c, ci, si)
            cr, ci = (pr * cr - pi * ci + er[c:c + 1, :], pr * ci + pi * cr + ei[c:c + 1, :])
        carry_ref[0:1, 0:ns] = cr
        carry_ref[0:1, ns:2 * ns] = ci

        def emit(k):
            rows = pl.ds(k * strip, strip)
            out_rows = pl.ds(row0 + k * strip, strip)
            y = (jnp.dot(cur_ref[rows, :].astype(BF16), cm_ref[0], preferred_element_type=F32)
                 + d_ref[0] * u_ref[out_rows, :])
            y_ref[out_rows, :] = y

        for k in range(n_strips):
            if k > 0:
                emit(k - 1)
            for r in range(k * strip // 8, (k + 1) * strip // 8):
                sr, si = advance(r, sr, si, True)
        emit(n_strips - 1)

    tile(bu_a_ref, bu_b_ref, 0, lambda r, n: u_ref[pl.ds(tt + r, n), :])
    tile(bu_b_ref, bu_a_ref, tt, lambda r, n: un_ref[pl.ds(r, n), :])


def _ssm_call(u_perm, bmat, cmat, lam2, lamr2, d3):
    L = u_perm.shape[0]
    n_slab = N_SSM_GROUPS // SSM_SLAB_GROUPS
    n_tiles = L // SSM_TT
    assert n_tiles % 2 == 0
    ns = SSM_SLAB_GROUPS * SSM_STATE
    cw = SSM_SLAB_GROUPS * SSM_GROUP
    return pl.pallas_call(
        _ssm_kernel,
        grid=(n_slab, n_tiles // 2),
        in_specs=[
            pl.BlockSpec((2 * SSM_TT, cw), lambda s, p: (p, s)),
            pl.BlockSpec((SSM_TT, cw), lambda s, p: (jnp.minimum(2 * p + 2, n_tiles - 1), s)),
            pl.BlockSpec((1, cw, 2 * ns), lambda s, p: (s, 0, 0)),
            pl.BlockSpec((1, 2 * ns, cw), lambda s, p: (s, 0, 0)),
            pl.BlockSpec((1, 2, ns), lambda s, p: (s, 0, 0)),
            pl.BlockSpec((1, 2, ns), lambda s, p: (s, 0, 0)),
            pl.BlockSpec((1, 1, cw), lambda s, p: (s, 0, 0)),
        ],
        out_specs=pl.BlockSpec((2 * SSM_TT, cw), lambda s, p: (p, s)),
        out_shape=jax.ShapeDtypeStruct((L, SSM_WIDTH), F32),
        scratch_shapes=[
            pltpu.VMEM((SSM_TT, 2 * ns), F32),
            pltpu.VMEM((SSM_TT, 2 * ns), F32),
            pltpu.VMEM((1, 2 * ns), F32),
        ],
        compiler_params=_cparams(("arbitrary", "arbitrary")),
        name="s5_scan",
    )(u_perm, u_perm, bmat, cmat, lam2, lamr2, d3)


def _ssm_params(lam_re, lam_im, log_step, b_re, b_im, c_re, c_im, d):
    g, p, hc = N_SSM_GROUPS, SSM_STATE, SSM_GROUP
    sg = SSM_SLAB_GROUPS
    n_slab = g // sg
    lam = lax.complex(jnp.minimum(lam_re.astype(F32), -1e-4), lam_im.astype(F32))
    step = jnp.exp(log_step.astype(F32))[:, None]
    lam_bar = jnp.exp(lam * step)
    lam_bar_r = jnp.exp(lam * (step * (SSM_TT // 8)))
    b_bar = ((lam_bar - 1.0) / lam)[:, :, None] * lax.complex(b_re.astype(F32), b_im.astype(F32))
    eye = jnp.eye(sg, dtype=F32)

    def b_block(part):
        z = part.reshape(n_slab, sg, p, hc).transpose(0, 1, 3, 2)
        return (z[:, :, :, None, :] * eye[None, :, None, :, None]).reshape(n_slab, sg * hc, sg * p)

    def c_block(part):
        z = part.reshape(n_slab, sg, hc, p).transpose(0, 1, 3, 2)
        return (z[:, :, :, None, :] * eye[None, :, None, :, None]).reshape(n_slab, sg * p, sg * hc)

    bmat = jnp.concatenate([b_block(jnp.real(b_bar)), b_block(jnp.imag(b_bar))], axis=-1)
    cmat = jnp.concatenate([c_block(c_re.astype(F32)), c_block(-c_im.astype(F32))], axis=1)

    def rows(zc):
        return jnp.stack([jnp.real(zc).reshape(n_slab, sg * p),
                          jnp.imag(zc).reshape(n_slab, sg * p)], axis=1)

    d3 = d.astype(F32).reshape(n_slab, 1, sg * hc)
    return bmat.astype(BF16), cmat.astype(BF16), rows(lam_bar), rows(lam_bar_r), d3


def _out_kernel(att_ref, y_ref, gw_ref, gb_ref, woa_ref, wob_ref, x_ref, gt_ref, o_ref,
                gw_bf_ref, wo_bf_ref):
    aw = ATT_WIDTH

    @pl.when(pl.program_id(0) == 0)
    def _():
        gw_bf_ref[...] = gw_ref[...].astype(BF16)
        wo_bf_ref[0:aw, :] = woa_ref[...].astype(BF16)
        wo_bf_ref[aw:, :] = wob_ref[...].astype(BF16)

    y = jax.nn.gelu(y_ref[...])
    z = jnp.dot(y.astype(BF16), gw_bf_ref[...], preferred_element_type=F32) + gb_ref[...]
    yg = (y * jax.nn.sigmoid(z)).astype(BF16)
    m = (jnp.dot(att_ref[...], wo_bf_ref[0:aw, :], preferred_element_type=F32)
         + jnp.dot(yg, wo_bf_ref[aw:, :], preferred_element_type=F32))
    o_ref[...] = x_ref[...] + gt_ref[...] * m


def _out_call(layer, att, y, glu_w, glu_b, w_out, x, mod):
    L, d = x.shape
    aw, sw = ATT_WIDTH, SSM_WIDTH
    one = pl.Buffered(1)
    return pl.pallas_call(
        _out_kernel,
        grid=(L // OUT_TM,),
        in_specs=[
            pl.BlockSpec((OUT_TM, aw), lambda i: (i, 0)),
            pl.BlockSpec((OUT_TM, sw), lambda i: (i, 0)),
            pl.BlockSpec((None, sw, sw), lambda i: (layer, 0, 0), pipeline_mode=one),
            pl.BlockSpec((1, sw), lambda i: (0, 0)),
            pl.BlockSpec((None, aw, d), lambda i: (layer, 0, 0), pipeline_mode=one),
            pl.BlockSpec((None, sw, d), lambda i: (layer, 1, 0), pipeline_mode=one),
            pl.BlockSpec((OUT_TM, d), lambda i: (i, 0)),
            _cond_spec(layer, 5, d),
        ],
        out_specs=pl.BlockSpec((OUT_TM, d), lambda i: (i, 0)),
        out_shape=jax.ShapeDtypeStruct((L, d), F32),
        scratch_shapes=[pltpu.VMEM((sw, sw), BF16), pltpu.VMEM((aw + sw, d), BF16)],
        compiler_params=_cparams(("arbitrary",)),
        name="out_proj",
    )(att, y, glu_w, glu_b, w_out, w_out, x, mod)


def _t5_causal_buckets(dist):
    max_exact = N_BUCKETS // 2
    d = jnp.maximum(dist, 1).astype(F32)
    large = max_exact + (jnp.log(d / max_exact) / math.log(MAX_DISTANCE / max_exact)
                         * (N_BUCKETS - max_exact)).astype(jnp.int32)
    large = jnp.minimum(large, N_BUCKETS - 1)
    return jnp.where(dist < max_exact, dist, large)


def _near_bias(rel_bias):
    t = ATT_T
    assert t >= MAX_DISTANCE
    kk = jnp.arange(t, dtype=jnp.int32)[:, None]
    qq = jnp.arange(t, dtype=jnp.int32)[None, :]
    dist = jnp.stack([qq - kk, qq - kk + t], axis=0)
    bucket = _t5_causal_buckets(jnp.maximum(dist, 0))
    rb = rel_bias.astype(F32) * LOG2_E
    val = jnp.zeros((N_ATT_HEADS,) + dist.shape, F32)
    for b in range(N_BUCKETS):
        val = jnp.where((bucket == b)[None], rb[b][:, None, None, None], val)
    return jnp.where((dist >= 0)[None], val, NEG_INF)


def _chunk_interleave(a):
    L, w = a.shape
    return a.reshape(L // SSM_TT, 8, SSM_TT // 8, w).transpose(0, 2, 1, 3).reshape(L, w)


def _chunk_deinterleave(a):
    L, w = a.shape
    return a.reshape(L // SSM_TT, SSM_TT // 8, 8, w).transpose(0, 2, 1, 3).reshape(L, w)


def kernel(x, c, rel_bias, ada_w, ada_b, norm_g, ffn1_w_gate, ffn1_w_up, ffn1_w_down, ffn2_w_gate, ffn2_w_up, ffn2_w_down, w_in, w_out, q_norm_g, k_norm_g, lambda_q1, lambda_k1, lambda_q2, lambda_k2, subln_g, ssm_lambda_re, ssm_lambda_im, ssm_log_step, ssm_b_re, ssm_b_im, ssm_c_re, ssm_c_im, ssm_d, ssm_glu_w, ssm_glu_b):
    b, L, d = x.shape
    assert b == 1 and c.shape == (1, d)
    x2 = x.reshape(L, d)

    mod = _ada_call(c.reshape(d, 1), ada_w, ada_b.reshape(DEPTH, 1, N_COND * d))
    near_bias = _near_bias(rel_bias)
    c_far = (rel_bias.astype(F32)[N_BUCKETS - 1] * LOG2_E).reshape(N_ATT_HEADS, 1, 1)
    w_in_bf = w_in.astype(BF16)
    hd = DIFF_HEAD_DIM
    group_of = jnp.arange(PROJ_NORM_LANES) // hd
    gmat = jnp.where(group_of[:, None] == group_of[None, :], 1.0 / hd, 0.0).astype(BF16)
    n_rep = ATT_WIDTH // hd

    ng3 = norm_g.reshape(DEPTH * 3, 1, d)

    for i in range(DEPTH):
        x2 = _ffn_call(i, 0, x2, ng3, mod, ffn1_w_gate, ffn1_w_up, ffn1_w_down)

        qg = jnp.tile(q_norm_g[i].astype(F32), n_rep)[None] * (hd ** -0.5 * LOG2_E)
        kg = jnp.tile(k_norm_g[i].astype(F32), n_rep)[None]
        q, k, vt, u = _proj_call(i, x2, ng3, mod, w_in_bf, gmat, qg, kg)
        bound = (DIFF_HEAD_DIM * jnp.max(jnp.abs(qg[0, :hd] * kg[0, :hd]))
                 + jnp.max(rel_bias.astype(F32), axis=0) * LOG2_E).reshape(N_ATT_HEADS, 1, 1)
        att = _attn_call(i, q, k, vt, near_bias, c_far, bound, lambda_q1[i][None], lambda_k1[i][None],
                         lambda_q2[i][None], lambda_k2[i][None], subln_g[i].reshape(V_HEAD_DIM, 1))
        bmat, cmat, lam2, lamr2, d3 = _ssm_params(
            ssm_lambda_re[i], ssm_lambda_im[i], ssm_log_step[i], ssm_b_re[i], ssm_b_im[i],
            ssm_c_re[i], ssm_c_im[i], ssm_d[i])
        y = _chunk_deinterleave(_ssm_call(_chunk_interleave(u), bmat, cmat, lam2, lamr2, d3))
        x2 = _out_call(i, att, y, ssm_glu_w, ssm_glu_b[i][None], w_out, x2, mod)

        x2 = _ffn_call(i, 2, x2, ng3, mod, ffn2_w_gate, ffn2_w_up, ffn2_w_down)
    return x2.reshape(b, L, d)
```

```python
import functools
import math

import jax
import jax.numpy as jnp
from jax import lax
from jax.experimental import pallas as pl
from jax.experimental.pallas import tpu as pltpu

D_MODEL = 2048
SEQ = 8192
DEPTH = 2
ATT_WIDTH = 1024
SSM_WIDTH = 1024
DIFF_HEAD_DIM = 64
V_HEAD_DIM = 128
N_ATT_HEADS = 8
SSM_GROUP = 16
N_SSM_GROUPS = 64
SSM_STATE = 64
IN_WIDTH = 4096
D_FF = 5632
N_BUCKETS = 32
MAX_DISTANCE = 128
N_COND = 9
NORM_EPS = 1e-6
SUBLN_EPS = 1e-5
NEG_INF = -1e30
LOG2_E = math.log2(math.e)

F32 = jnp.float32
BF16 = jnp.bfloat16

VMEM_LIMIT_BYTES = 62 * 1024 * 1024

ADA_TN = 2048
FFN_TM = 1024
FFN_TF = 512
FFN_NORM_ROWS = 512
PROJ_TM = 512
PROJ_NORM_LANES = 128
PROJ_STAGE_COLS = 512
ATT_T = 256
ATT_TQ = 1024
ATT_MIN_DENOM = 2.0 ** -40
SSM_TT = 1024
SSM_SLAB_GROUPS = 8
SSM_STRIP_ROWS = 128
OUT_TM = 512


def _cparams(sem):
    return pltpu.CompilerParams(dimension_semantics=sem, vmem_limit_bytes=VMEM_LIMIT_BYTES)


def _ada_kernel(c_ref, w_ref, b_ref, o_ref):
    c = c_ref[...]
    cs = c * jax.nn.sigmoid(c)
    o_ref[0] = jnp.sum(w_ref[0] * cs, axis=0, keepdims=True) + b_ref[0]


def _ada_call(c_col, ada_w, ada_b3):
    depth, d, n = ada_w.shape
    return pl.pallas_call(
        _ada_kernel,
        grid=(depth, n // ADA_TN),
        in_specs=[
            pl.BlockSpec((d, 1), lambda l, j: (0, 0)),
            pl.BlockSpec((1, d, ADA_TN), lambda l, j: (l, 0, j)),
            pl.BlockSpec((1, 1, ADA_TN), lambda l, j: (l, 0, j)),
        ],
        out_specs=pl.BlockSpec((1, 1, ADA_TN), lambda l, j: (l, 0, j)),
        out_shape=jax.ShapeDtypeStruct((depth, 1, n), F32),
        compiler_params=_cparams(("arbitrary", "arbitrary")),
        name="adaln",
    )(c_col, ada_w, ada_b3)


def _cond_spec(layer, n, d):
    return pl.BlockSpec((None, 1, d), lambda *_: (layer, 0, n))


def _gain_spec(layer, sub, d):
    return pl.BlockSpec((None, 1, d), lambda *_: (layer * 3 + sub, 0, 0))


def _norm_mod(x, g, sc, sh):
    ms = jnp.mean(x * x, axis=-1, keepdims=True)
    return (x * lax.rsqrt(ms + NORM_EPS) * g) * (1.0 + sc) + sh


def _ffn_kernel(x_hbm_ref, ng_ref, sh_ref, sc_ref, gt_ref, wg_hbm_ref, wu_hbm_ref, wd_hbm_ref,
                o_ref, h_ref, x_ref, wg_ref, wu_ref, wd_ref, x_sem, w_sem, *, layer):
    i = pl.program_id(0)
    n_i = pl.num_programs(0)
    nj = wg_hbm_ref.shape[2] // FFN_TF
    base = (i * nj) % 2

    def x_copy(tile):
        rows = pl.ds(pl.multiple_of(tile * FFN_TM, FFN_TM), FFN_TM)
        return pltpu.make_async_copy(x_hbm_ref.at[rows, :], x_ref, x_sem)

    def w_copies(jt, slot):
        cols = pl.ds(pl.multiple_of(jt * FFN_TF, FFN_TF), FFN_TF)
        return (pltpu.make_async_copy(wg_hbm_ref.at[layer, :, cols], wg_ref.at[slot], w_sem.at[0, slot]),
                pltpu.make_async_copy(wu_hbm_ref.at[layer, :, cols], wu_ref.at[slot], w_sem.at[1, slot]),
                pltpu.make_async_copy(wd_hbm_ref.at[layer, cols, :], wd_ref.at[slot], w_sem.at[2, slot]))

    def start_w(jt, slot):
        for cp in w_copies(jt, slot):
            cp.start()

    def wait_w(jt, slot):
        for cp in w_copies(jt, slot):
            cp.wait()

    def hidden_update(h, slot):
        wg = wg_ref[slot].astype(BF16)
        wu = wu_ref[slot].astype(BF16)
        wd = wd_ref[slot].astype(BF16)
        g = jnp.dot(h, wg, preferred_element_type=F32)
        u = jnp.dot(h, wu, preferred_element_type=F32)
        a = (g * jax.nn.sigmoid(g) * u).astype(BF16)
        return jnp.dot(a, wd, preferred_element_type=F32)

    half_gate = 0.5 * gt_ref[...]

    @pl.when(i == 0)
    def _():
        x_copy(0).start()
        start_w(0, 0)

    start_w(1, 1 - base)
    x_copy(i).wait()
    wait_w(0, base)
    for r in range(0, FFN_TM, FFN_NORM_ROWS):
        rows = pl.ds(r, FFN_NORM_ROWS)
        x = x_ref[rows, :]
        h = _norm_mod(x, ng_ref[...], sc_ref[...], sh_ref[...]).astype(BF16)
        h_ref[rows, :] = h
        o_ref[rows, :] = x + half_gate * hidden_update(h, base)

    @pl.when(i + 1 < n_i)
    def _():
        x_copy(i + 1).start()

    def hidden_tile(jt, carry):
        slot = (base + jt) % 2
        more_here = jt + 1 < nj

        @pl.when(more_here | (i + 1 < n_i))
        def _():
            start_w(jnp.where(more_here, jt + 1, 0), 1 - slot)

        wait_w(jt, slot)
        o_ref[...] += half_gate * hidden_update(h_ref[...], slot)
        return carry

    lax.fori_loop(1, nj, hidden_tile, 0)


def _ffn_call(layer, sub, x, ng3, mod, wg, wu, wd):
    L, d = x.shape
    dff = wg.shape[-1]
    assert dff % FFN_TF == 0 and dff // FFN_TF >= 2 and L % FFN_TM == 0
    hbm = pl.BlockSpec(memory_space=pl.ANY)
    return pl.pallas_call(
        functools.partial(_ffn_kernel, layer=layer),
        grid=(L // FFN_TM,),
        in_specs=[
            hbm,
            _gain_spec(layer, sub, d), _cond_spec(layer, 3 * sub, d),
            _cond_spec(layer, 3 * sub + 1, d), _cond_spec(layer, 3 * sub + 2, d),
            hbm, hbm, hbm,
        ],
        out_specs=pl.BlockSpec((FFN_TM, d), lambda i: (i, 0)),
        out_shape=jax.ShapeDtypeStruct((L, d), F32),
        scratch_shapes=[
            pltpu.VMEM((FFN_TM, d), BF16),
            pltpu.VMEM((FFN_TM, d), F32),
            pltpu.VMEM((2, d, FFN_TF), F32),
            pltpu.VMEM((2, d, FFN_TF), F32),
            pltpu.VMEM((2, FFN_TF, d), F32),
            pltpu.SemaphoreType.DMA,
            pltpu.SemaphoreType.DMA((3, 2)),
        ],
        compiler_params=_cparams(("arbitrary",)),
        name="ffn",
    )(x, ng3, mod, mod, mod, wg, wu, wd)


def _proj_kernel(x_ref, ng_ref, sh_ref, sc_ref, w_hbm_ref, gm_ref, qg_ref, kg_ref,
                 q_ref, k_ref, vt_ref, u_ref, w_ref, stage_ref, w_sem, *, layer):
    aw = ATT_WIDTH

    @pl.when(pl.program_id(0) == 0)
    def _():
        cw = stage_ref.shape[2]
        n_chunks = w_ref.shape[1] // cw

        def chunk_copy(c):
            return pltpu.make_async_copy(w_hbm_ref.at[layer, :, c * cw:(c + 1) * cw],
                                         stage_ref.at[c % 2], w_sem.at[c % 2])

        chunk_copy(0).start()
        for c in range(n_chunks):
            if c + 1 < n_chunks:
                chunk_copy(c + 1).start()
            chunk_copy(c).wait()
            w_ref[:, c * cw:(c + 1) * cw] = stage_ref[c % 2].astype(BF16)

    h = _norm_mod(x_ref[...], ng_ref[...], sc_ref[...], sh_ref[...]).astype(BF16)

    def head_norm(z, g):
        sq = (z * z).astype(BF16)
        lanes = gm_ref.shape[0]
        ms = jnp.concatenate(
            [jnp.dot(sq[:, c:c + lanes], gm_ref[...], preferred_element_type=F32)
             for c in range(0, aw, lanes)], axis=1)
        return z * lax.rsqrt(ms + NORM_EPS) * g

    q = jnp.dot(h, w_ref[:, 0:aw], preferred_element_type=F32)
    q_ref[...] = head_norm(q, qg_ref[...]).astype(BF16)
    k = jnp.dot(h, w_ref[:, aw:2 * aw], preferred_element_type=F32)
    k_ref[...] = head_norm(k, kg_ref[...]).astype(BF16)
    v = jnp.dot(h, w_ref[:, 2 * aw:3 * aw], preferred_element_type=F32)
    vt_ref[0] = v.T.astype(BF16)
    u_ref[...] = jnp.dot(h, w_ref[:, 3 * aw:4 * aw], preferred_element_type=F32)


def _proj_call(layer, x, ng3, mod, w_in, gmat, qg, kg):
    L, d = x.shape
    aw = ATT_WIDTH
    unit = ATT_TQ
    per_unit = unit // PROJ_TM
    vec_a = pl.BlockSpec((1, aw), lambda i: (0, 0))
    return pl.pallas_call(
        functools.partial(_proj_kernel, layer=layer),
        grid=(L // PROJ_TM,),
        in_specs=[
            pl.BlockSpec((PROJ_TM, d), lambda i: (i, 0)),
            _gain_spec(layer, 1, d), _cond_spec(layer, 3, d), _cond_spec(layer, 4, d),
            pl.BlockSpec(memory_space=pl.ANY),
            pl.BlockSpec((PROJ_NORM_LANES, PROJ_NORM_LANES), lambda i: (0, 0)),
            vec_a, vec_a,
        ],
        out_specs=[
            pl.BlockSpec((PROJ_TM, aw), lambda i: (i, 0)),
            pl.BlockSpec((PROJ_TM, aw), lambda i: (i, 0)),
            pl.BlockSpec((1, aw, PROJ_TM), lambda i: (i // per_unit, 0, i % per_unit)),
            pl.BlockSpec((PROJ_TM, SSM_WIDTH), lambda i: (i, 0)),
        ],
        out_shape=[
            jax.ShapeDtypeStruct((L, aw), BF16),
            jax.ShapeDtypeStruct((L, aw), BF16),
            jax.ShapeDtypeStruct((L // unit, aw, unit), BF16),
            jax.ShapeDtypeStruct((L, SSM_WIDTH), F32),
        ],
        scratch_shapes=[
            pltpu.VMEM((d, IN_WIDTH), BF16),
            pltpu.VMEM((2, d, PROJ_STAGE_COLS), F32),
            pltpu.SemaphoreType.DMA((2,)),
        ],
        compiler_params=_cparams(("arbitrary",)),
        name="in_proj",
    )(x, ng3, mod, mod, w_in, gmat, qg, kg)


def _attn_kernel(q_ref, k_ref, vt_ref, nbias_ref, cfar_ref, bound_ref, lq1_ref, lk1_ref, lq2_ref,
                 lk2_ref, sg_ref, o_ref, acc_ref, m_ref, l_ref, bias_ref, s_ref, p_ref, *, lam_init):
    t = ATT_T
    tq = ATT_TQ
    qb = tq // t
    n_tiles = q_ref.shape[0] // tq

    bound = bound_ref[0]
    far_bias = cfar_ref[0] - bound
    tiles = {2: jnp.broadcast_to(far_bias, (t, t)), 1: nbias_ref[0, 1] - bound,
             0: nbias_ref[0, 0] - bound, -1: jnp.full((t, t), NEG_INF, F32)}
    for d in range(-qb, 3):
        for a in range(qb):
            bias_ref[d + qb, :, a * t:(a + 1) * t] = tiles[max(min(d + a, 2), -1)]

    lam = (jnp.exp(jnp.sum(lq1_ref[...] * lk1_ref[...], keepdims=True))
           - jnp.exp(jnp.sum(lq2_ref[...] * lk2_ref[...], keepdims=True)) + lam_init)
    out_gain = sg_ref[...] * (1.0 - lam_init)

    def load_qcat(qi):
        q = q_ref[pl.ds(pl.multiple_of(qi * tq, tq), tq), :]
        lane = lax.broadcasted_iota(jnp.int32, q.shape, 1)
        zero = jnp.zeros_like(q)
        return jnp.concatenate([jnp.where(lane < DIFF_HEAD_DIM, q, zero),
                                jnp.where(lane >= DIFF_HEAD_DIM, q, zero)], axis=0)

    def scores(blk, qmat):
        off = pl.multiple_of(blk * t, t)
        return lax.dot_general(k_ref[pl.ds(off, t), :], qmat, (((1,), (1,)), ((), ())),
                               preferred_element_type=F32)

    def start_tile(qcat):
        acc_ref[...] = jnp.zeros(acc_ref.shape, F32)
        l_ref[...] = jnp.zeros(l_ref.shape, F32)
        p_ref[...] = jnp.zeros(p_ref.shape, BF16)
        s_ref[...] = scores(0, qcat)

    start_tile(load_qcat(0))

    def tile_body(qi, carry):
        qcat = load_qcat(qi)

        def biased(s, blk):
            bias = bias_ref[jnp.clip(qb * qi - blk, -qb, 2) + qb]
            return s[:, :tq] + bias, s[:, tq:] + bias

        def fast_unit(u, c):
            s_cur = s_ref[...]
            pv = jnp.dot(vt_ref[jnp.maximum(u - 1, 0), :, (qb - 1) * t:qb * t], p_ref[...],
                         preferred_element_type=F32)
            lsum = jnp.zeros((1, 2 * tq), F32)
            for b in range(qb):
                blk = u * qb + b
                s_next = scores(blk + 1, qcat)
                if b < qb - 1:
                    s0, s1 = s_cur[:, :tq] + far_bias, s_cur[:, tq:] + far_bias
                else:
                    s0, s1 = biased(s_cur, blk)
                p = jnp.concatenate([jnp.exp2(s0), jnp.exp2(s1)], axis=1)
                lsum = lsum + jnp.sum(p, axis=0, keepdims=True)
                if b < qb - 1:
                    pv = pv + jnp.dot(vt_ref[u, :, b * t:(b + 1) * t], p.astype(BF16),
                                      preferred_element_type=F32)
                else:
                    p_ref[...] = p.astype(BF16)
                s_cur = s_next
            s_ref[...] = s_cur
            acc_ref[...] += pv
            l_ref[...] += lsum
            return c

        lax.fori_loop(0, qi, fast_unit, 0)

        acc_ref[...] += jnp.dot(vt_ref[jnp.maximum(qi - 1, 0), :, (qb - 1) * t:qb * t], p_ref[...],
                                preferred_element_type=F32)
        s_cur = s_ref[...]
        for b in range(qb):
            w = (qb - b) * t
            if b < qb - 1:
                qn = jnp.concatenate([qcat[(b + 1) * t:tq], qcat[tq + (b + 1) * t:2 * tq]], axis=0)
                s_next = scores(qb * qi + b + 1, qn)
            bias = bias_ref[qb - b, :, b * t:tq]
            p0 = jnp.exp2(s_cur[:, :w] + bias)
            p1 = jnp.exp2(s_cur[:, w:] + bias)
            l_ref[:, b * t:tq] += jnp.sum(p0, axis=0, keepdims=True)
            l_ref[:, tq + b * t:2 * tq] += jnp.sum(p1, axis=0, keepdims=True)
            pv = jnp.dot(vt_ref[qi, :, b * t:(b + 1) * t],
                         jnp.concatenate([p0, p1], axis=1).astype(BF16),
                         preferred_element_type=F32)
            acc_ref[:, b * t:tq] += pv[:, :w]
            acc_ref[:, tq + b * t:2 * tq] += pv[:, w:]
            if b < qb - 1:
                s_cur = s_next

        l_min = jnp.min(l_ref[...], keepdims=True)
        underflow = jnp.logical_not(l_min[0, 0] > ATT_MIN_DENOM)

        @pl.when(underflow)
        def _():
            m_ref[...] = jnp.full(m_ref.shape, NEG_INF, F32)
            l_ref[...] = jnp.zeros(l_ref.shape, F32)
            acc_ref[...] = jnp.zeros(acc_ref.shape, F32)

            def exact_unit(u, c):
                for b in range(qb):
                    blk = u * qb + b
                    s = jnp.concatenate(biased(scores(blk, qcat), blk), axis=1)
                    m_old = m_ref[...]
                    m_new = jnp.maximum(m_old, jnp.max(s, axis=0, keepdims=True))
                    alpha = jnp.exp2(m_old - m_new)
                    p = jnp.exp2(s - m_new)
                    l_ref[...] = alpha * l_ref[...] + jnp.sum(p, axis=0, keepdims=True)
                    acc_ref[...] = alpha * acc_ref[...] + jnp.dot(
                        vt_ref[u, :, b * t:(b + 1) * t], p.astype(BF16),
                        preferred_element_type=F32)
                    m_ref[...] = m_new
                return c

            lax.fori_loop(0, qi + 1, exact_unit, 0)

        on = acc_ref[...] / l_ref[...]
        o = on[:, :tq] - lam * on[:, tq:]
        ms = jnp.mean(o * o, axis=0, keepdims=True)
        o = o * lax.rsqrt(ms + SUBLN_EPS) * out_gain
        o_ref[pl.ds(pl.multiple_of(qi * tq, tq), tq), :] = o.T.astype(BF16)
        start_tile(load_qcat(jnp.minimum(qi + 1, n_tiles - 1)))
        return carry

    lax.fori_loop(0, n_tiles, tile_body, 0)


def _attn_call(layer_idx, q, k, vt, near_bias, c_far, bound, lq1, lk1, lq2, lk2, sg_col):
    L = q.shape[0]
    t, tq = ATT_T, ATT_TQ
    assert tq % t == 0 and L % tq == 0
    lam_init = 0.8 - 0.6 * math.exp(-0.3 * layer_idx)
    lvec = pl.BlockSpec((1, DIFF_HEAD_DIM), lambda h: (0, 0))
    scalar = pl.BlockSpec((1, 1, 1), lambda h: (h, 0, 0))
    return pl.pallas_call(
        functools.partial(_attn_kernel, lam_init=lam_init),
        grid=(N_ATT_HEADS,),
        in_specs=[
            pl.BlockSpec((L, V_HEAD_DIM), lambda h: (0, h)),
            pl.BlockSpec((L, V_HEAD_DIM), lambda h: (0, h)),
            pl.BlockSpec((L // tq, V_HEAD_DIM, tq), lambda h: (0, h, 0)),
            pl.BlockSpec((1, 2, t, t), lambda h: (h, 0, 0, 0)),
            scalar, scalar,
            lvec, lvec, lvec, lvec,
            pl.BlockSpec((V_HEAD_DIM, 1), lambda h: (0, 0)),
        ],
        out_specs=pl.BlockSpec((L, V_HEAD_DIM), lambda h: (0, h)),
        out_shape=jax.ShapeDtypeStruct((L, ATT_WIDTH), BF16),
        scratch_shapes=[
            pltpu.VMEM((V_HEAD_DIM, 2 * tq), F32),
            pltpu.VMEM((1, 2 * tq), F32),
            pltpu.VMEM((1, 2 * tq), F32),
            pltpu.VMEM((tq // t + 3, t, tq), F32),
            pltpu.VMEM((t, 2 * tq), F32),
            pltpu.VMEM((t, 2 * tq), BF16),
        ],
        compiler_params=_cparams(("arbitrary",)),
        name="diff_attn",
    )(q, k, vt, near_bias, c_far, bound, lq1, lk1, lq2, lk2, sg_col)


def _ssm_kernel(u_ref, un_ref, bm_ref, cm_ref, lam_ref, lamr_ref, d_ref, y_ref,
                bu_a_ref, bu_b_ref, e_ref, carry_ref):
    ns = SSM_SLAB_GROUPS * SSM_STATE
    tt = SSM_TT
    strip = SSM_STRIP_ROWS
    n_strips = tt // strip
    steps = strip // 8
    bm = bm_ref[0]

    lr = jnp.broadcast_to(lam_ref[0, 0:1, :], (8, ns))
    li = jnp.broadcast_to(lam_ref[0, 1:2, :], (8, ns))
    pr = lamr_ref[0, 0:1, :]
    pi = lamr_ref[0, 1:2, :]
    row_id = lax.broadcasted_iota(jnp.int32, (8, ns), 0)
    z = jnp.zeros((8, ns), F32)

    def advance(ref, r, sr, si, store):
        rows = pl.ds(8 * r, 8)
        nsr = lr * sr - li * si + ref[rows, 0:ns]
        nsi = lr * si + li * sr + ref[rows, ns:2 * ns]
        if store:
            ref[rows, 0:ns] = nsr
            ref[rows, ns:2 * ns] = nsi
        return nsr, nsi

    def input_strip(dst_ref, src, k):
        dst_ref[pl.ds(k * strip, strip), :] = jnp.dot(
            src(k * strip, strip).astype(BF16), bm, preferred_element_type=F32)

    def carry_in(er, ei):
        cr = carry_ref[0:1, 0:ns]
        ci = carry_ref[0:1, ns:2 * ns]
        sr, si = z, z
        for c in range(8):
            sr = jnp.where(row_id == c, cr, sr)
            si = jnp.where(row_id == c, ci, si)
            cr, ci = (pr * cr - pi * ci + er[c:c + 1, :], pr * ci + pi * cr + ei[c:c + 1, :])
        carry_ref[0:1, 0:ns] = cr
        carry_ref[0:1, ns:2 * ns] = ci
        return sr, si

    def emit(ref, row0, k):
        rows = pl.ds(k * strip, strip)
        out_rows = pl.ds(row0 + k * strip, strip)
        y = (jnp.dot(ref[rows, :].astype(BF16), cm_ref[0], preferred_element_type=F32)
             + d_ref[0] * u_ref[out_rows, :])
        y_ref[out_rows, :] = y

    @pl.when(pl.program_id(1) == 0)
    def _():
        carry_ref[...] = jnp.zeros_like(carry_ref)
        bu_a_ref[...] = jnp.dot(u_ref[0:tt, :].astype(BF16), bm, preferred_element_type=F32)
        er, ei = z, z
        for r in range(tt // 8):
            er, ei = advance(bu_a_ref, r, er, ei, False)
        e_ref[:, 0:ns] = er
        e_ref[:, ns:2 * ns] = ei

    def phase(cur_ref, row0, cur_er, cur_ei, nxt_ref, nxt_src):
        sr, si = carry_in(cur_er, cur_ei)
        er, ei = z, z
        input_strip(nxt_ref, nxt_src, 0)
        for k in range(n_strips):
            if k + 1 < n_strips:
                input_strip(nxt_ref, nxt_src, k + 1)
            if k > 0:
                emit(cur_ref, row0, k - 1)
            for r in range(k * steps, (k + 1) * steps):
                sr, si = advance(cur_ref, r, sr, si, True)
                er, ei = advance(nxt_ref, r, er, ei, False)
        emit(cur_ref, row0, n_strips - 1)
        return er, ei

    er, ei = phase(bu_a_ref, 0, e_ref[:, 0:ns], e_ref[:, ns:2 * ns], bu_b_ref,
                   lambda r, n: u_ref[pl.ds(tt + r, n), :])
    er, ei = phase(bu_b_ref, tt, er, ei, bu_a_ref, lambda r, n: un_ref[pl.ds(r, n), :])
    e_ref[:, 0:ns] = er
    e_ref[:, ns:2 * ns] = ei


def _ssm_call(u_perm, bmat, cmat, lam2, lamr2, d3):
    L = u_perm.shape[0]
    n_slab = N_SSM_GROUPS // SSM_SLAB_GROUPS
    n_tiles = L // SSM_TT
    assert n_tiles % 2 == 0
    ns = SSM_SLAB_GROUPS * SSM_STATE
    cw = SSM_SLAB_GROUPS * SSM_GROUP
    return pl.pallas_call(
        _ssm_kernel,
        grid=(n_slab, n_tiles // 2),
        in_specs=[
            pl.BlockSpec((2 * SSM_TT, cw), lambda s, p: (p, s)),
            pl.BlockSpec((SSM_TT, cw), lambda s, p: (jnp.minimum(2 * p + 2, n_tiles - 1), s)),
            pl.BlockSpec((1, cw, 2 * ns), lambda s, p: (s, 0, 0)),
            pl.BlockSpec((1, 2 * ns, cw), lambda s, p: (s, 0, 0)),
            pl.BlockSpec((1, 2, ns), lambda s, p: (s, 0, 0)),
            pl.BlockSpec((1, 2, ns), lambda s, p: (s, 0, 0)),
            pl.BlockSpec((1, 1, cw), lambda s, p: (s, 0, 0)),
        ],
        out_specs=pl.BlockSpec((2 * SSM_TT, cw), lambda s, p: (p, s)),
        out_shape=jax.ShapeDtypeStruct((L, SSM_WIDTH), F32),
        scratch_shapes=[
            pltpu.VMEM((SSM_TT, 2 * ns), F32),
            pltpu.VMEM((SSM_TT, 2 * ns), F32),
            pltpu.VMEM((8, 2 * ns), F32),
            pltpu.VMEM((1, 2 * ns), F32),
        ],
        compiler_params=_cparams(("arbitrary", "arbitrary")),
        name="s5_scan",
    )(u_perm, u_perm, bmat, cmat, lam2, lamr2, d3)


def _ssm_params(lam_re, lam_im, log_step, b_re, b_im, c_re, c_im, d):
    g, p, hc = N_SSM_GROUPS, SSM_STATE, SSM_GROUP
    sg = SSM_SLAB_GROUPS
    n_slab = g // sg
    lam = lax.complex(jnp.minimum(lam_re.astype(F32), -1e-4), lam_im.astype(F32))
    step = jnp.exp(log_step.astype(F32))[:, None]
    lam_bar = jnp.exp(lam * step)
    lam_bar_r = jnp.exp(lam * (step * (SSM_TT // 8)))
    b_bar = ((lam_bar - 1.0) / lam)[:, :, None] * lax.complex(b_re.astype(F32), b_im.astype(F32))
    eye = jnp.eye(sg, dtype=F32)

    def b_block(part):
        z = part.reshape(n_slab, sg, p, hc).transpose(0, 1, 3, 2)
        return (z[:, :, :, None, :] * eye[None, :, None, :, None]).reshape(n_slab, sg * hc, sg * p)

    def c_block(part):
        z = part.reshape(n_slab, sg, hc, p).transpose(0, 1, 3, 2)
        return (z[:, :, :, None, :] * eye[None, :, None, :, None]).reshape(n_slab, sg * p, sg * hc)

    bmat = jnp.concatenate([b_block(jnp.real(b_bar)), b_block(jnp.imag(b_bar))], axis=-1)
    cmat = jnp.concatenate([c_block(c_re.astype(F32)), c_block(-c_im.astype(F32))], axis=1)

    def rows(zc):
        return jnp.stack([jnp.real(zc).reshape(n_slab, sg * p),
                          jnp.imag(zc).reshape(n_slab, sg * p)], axis=1)

    d3 = d.astype(F32).reshape(n_slab, 1, sg * hc)
    return bmat.astype(BF16), cmat.astype(BF16), rows(lam_bar), rows(lam_bar_r), d3


def _out_kernel(att_ref, y_ref, gw_ref, gb_ref, woa_ref, wob_ref, x_ref, gt_ref, o_ref,
                gw_bf_ref, wo_bf_ref):
    aw = ATT_WIDTH

    @pl.when(pl.program_id(0) == 0)
    def _():
        gw_bf_ref[...] = gw_ref[...].astype(BF16)
        wo_bf_ref[0:aw, :] = woa_ref[...].astype(BF16)
        wo_bf_ref[aw:, :] = wob_ref[...].astype(BF16)

    y = jax.nn.gelu(y_ref[...])
    z = jnp.dot(y.astype(BF16), gw_bf_ref[...], preferred_element_type=F32) + gb_ref[...]
    yg = (y * jax.nn.sigmoid(z)).astype(BF16)
    m = (jnp.dot(att_ref[...], wo_bf_ref[0:aw, :], preferred_element_type=F32)
         + jnp.dot(yg, wo_bf_ref[aw:, :], preferred_element_type=F32))
    o_ref[...] = x_ref[...] + gt_ref[...] * m


def _out_call(layer, att, y, glu_w, glu_b, w_out, x, mod):
    L, d = x.shape
    aw, sw = ATT_WIDTH, SSM_WIDTH
    one = pl.Buffered(1)
    return pl.pallas_call(
        _out_kernel,
        grid=(L // OUT_TM,),
        in_specs=[
            pl.BlockSpec((OUT_TM, aw), lambda i: (i, 0)),
            pl.BlockSpec((OUT_TM, sw), lambda i: (i, 0)),
            pl.BlockSpec((None, sw, sw), lambda i: (layer, 0, 0), pipeline_mode=one),
            pl.BlockSpec((1, sw), lambda i: (0, 0)),
            pl.BlockSpec((None, aw, d), lambda i: (layer, 0, 0), pipeline_mode=one),
            pl.BlockSpec((None, sw, d), lambda i: (layer, 1, 0), pipeline_mode=one),
            pl.BlockSpec((OUT_TM, d), lambda i: (i, 0)),
            _cond_spec(layer, 5, d),
        ],
        out_specs=pl.BlockSpec((OUT_TM, d), lambda i: (i, 0)),
        out_shape=jax.ShapeDtypeStruct((L, d), F32),
        scratch_shapes=[pltpu.VMEM((sw, sw), BF16), pltpu.VMEM((aw + sw, d), BF16)],
        compiler_params=_cparams(("arbitrary",)),
        name="out_proj",
    )(att, y, glu_w, glu_b, w_out, w_out, x, mod)


def _t5_causal_buckets(dist):
    max_exact = N_BUCKETS // 2
    d = jnp.maximum(dist, 1).astype(F32)
    large = max_exact + (jnp.log(d / max_exact) / math.log(MAX_DISTANCE / max_exact)
                         * (N_BUCKETS - max_exact)).astype(jnp.int32)
    large = jnp.minimum(large, N_BUCKETS - 1)
    return jnp.where(dist < max_exact, dist, large)


def _near_bias(rel_bias):
    t = ATT_T
    assert t >= MAX_DISTANCE
    kk = jnp.arange(t, dtype=jnp.int32)[:, None]
    qq = jnp.arange(t, dtype=jnp.int32)[None, :]
    dist = jnp.stack([qq - kk, qq - kk + t], axis=0)
    bucket = _t5_causal_buckets(jnp.maximum(dist, 0))
    rb = rel_bias.astype(F32) * LOG2_E
    val = jnp.zeros((N_ATT_HEADS,) + dist.shape, F32)
    for b in range(N_BUCKETS):
        val = jnp.where((bucket == b)[None], rb[b][:, None, None, None], val)
    return jnp.where((dist >= 0)[None], val, NEG_INF)


def _chunk_interleave(a):
    L, w = a.shape
    return a.reshape(L // SSM_TT, 8, SSM_TT // 8, w).transpose(0, 2, 1, 3).reshape(L, w)


def _chunk_deinterleave(a):
    L, w = a.shape
    return a.reshape(L // SSM_TT, SSM_TT // 8, 8, w).transpose(0, 2, 1, 3).reshape(L, w)


def kernel(x, c, rel_bias, ada_w, ada_b, norm_g, ffn1_w_gate, ffn1_w_up, ffn1_w_down, ffn2_w_gate, ffn2_w_up, ffn2_w_down, w_in, w_out, q_norm_g, k_norm_g, lambda_q1, lambda_k1, lambda_q2, lambda_k2, subln_g, ssm_lambda_re, ssm_lambda_im, ssm_log_step, ssm_b_re, ssm_b_im, ssm_c_re, ssm_c_im, ssm_d, ssm_glu_w, ssm_glu_b):
    b, L, d = x.shape
    assert b == 1 and c.shape == (1, d)
    x2 = x.reshape(L, d)

    mod = _ada_call(c.reshape(d, 1), ada_w, ada_b.reshape(DEPTH, 1, N_COND * d))
    near_bias = _near_bias(rel_bias)
    c_far = (rel_bias.astype(F32)[N_BUCKETS - 1] * LOG2_E).reshape(N_ATT_HEADS, 1, 1)
    hd = DIFF_HEAD_DIM
    group_of = jnp.arange(PROJ_NORM_LANES) // hd
    gmat = jnp.where(group_of[:, None] == group_of[None, :], 1.0 / hd, 0.0).astype(BF16)
    n_rep = ATT_WIDTH // hd

    ng3 = norm_g.reshape(DEPTH * 3, 1, d)

    for i in range(DEPTH):
        x2 = _ffn_call(i, 0, x2, ng3, mod, ffn1_w_gate, ffn1_w_up, ffn1_w_down)

        qg = jnp.tile(q_norm_g[i].astype(F32), n_rep)[None] * (hd ** -0.5 * LOG2_E)
        kg = jnp.tile(k_norm_g[i].astype(F32), n_rep)[None]
        q, k, vt, u = _proj_call(i, x2, ng3, mod, w_in, gmat, qg, kg)
        bound = (DIFF_HEAD_DIM * jnp.max(jnp.abs(qg[0, :hd] * kg[0, :hd]))
                 + jnp.max(rel_bias.astype(F32), axis=0) * LOG2_E).reshape(N_ATT_HEADS, 1, 1)
        att = _attn_call(i, q, k, vt, near_bias, c_far, bound, lambda_q1[i][None], lambda_k1[i][None],
                         lambda_q2[i][None], lambda_k2[i][None], subln_g[i].reshape(V_HEAD_DIM, 1))
        bmat, cmat, lam2, lamr2, d3 = _ssm_params(
            ssm_lambda_re[i], ssm_lambda_im[i], ssm_log_step[i], ssm_b_re[i], ssm_b_im[i],
            ssm_c_re[i], ssm_c_im[i], ssm_d[i])
        y = _chunk_deinterleave(_ssm_call(_chunk_interleave(u), bmat, cmat, lam2, lamr2, d3))
        x2 = _out_call(i, att, y, ssm_glu_w, ssm_glu_b[i][None], w_out, x2, mod)

        x2 = _ffn_call(i, 2, x2, ng3, mod, ffn2_w_gate, ffn2_w_up, ffn2_w_down)
    return x2.reshape(b, L, d)
```

```python
import functools
import math

import jax
import jax.numpy as jnp
from jax import lax
from jax.experimental import pallas as pl
from jax.experimental.pallas import tpu as pltpu

D_MODEL = 2048
SEQ = 8192
DEPTH = 2
ATT_WIDTH = 1024
SSM_WIDTH = 1024
DIFF_HEAD_DIM = 64
V_HEAD_DIM = 128
N_ATT_HEADS = 8
SSM_GROUP = 16
N_SSM_GROUPS = 64
SSM_STATE = 64
IN_WIDTH = 4096
D_FF = 5632
N_BUCKETS = 32
MAX_DISTANCE = 128
N_COND = 9
NORM_EPS = 1e-6
SUBLN_EPS = 1e-5
NEG_INF = -1e30
LOG2_E = math.log2(math.e)

F32 = jnp.float32
BF16 = jnp.bfloat16

VMEM_LIMIT_BYTES = 62 * 1024 * 1024

ADA_TN = 2048
FFN_TM = 1024
FFN_TF = 512
FFN_NORM_ROWS = 512
PROJ_TM = 512
PROJ_NORM_LANES = 128
PROJ_STAGE_COLS = 512
ATT_T = 256
ATT_TQ = 1024
ATT_MIN_DENOM = 2.0 ** -40
SSM_CHUNKS = 8
SSM_TT = 1024
SSM_SLAB_GROUPS = 8
SSM_STRIP_ROWS = 128
OUT_TM = 512


def _cparams(sem):
    return pltpu.CompilerParams(dimension_semantics=sem, vmem_limit_bytes=VMEM_LIMIT_BYTES)


def _ada_kernel(c_ref, w_ref, b_ref, o_ref):
    c = c_ref[...]
    cs = c * jax.nn.sigmoid(c)
    o_ref[0] = jnp.sum(w_ref[0] * cs, axis=0, keepdims=True) + b_ref[0]


def _ada_call(c_col, ada_w, ada_b3):
    depth, d, n = ada_w.shape
    return pl.pallas_call(
        _ada_kernel,
        grid=(depth, n // ADA_TN),
        in_specs=[
            pl.BlockSpec((d, 1), lambda l, j: (0, 0)),
            pl.BlockSpec((1, d, ADA_TN), lambda l, j: (l, 0, j)),
            pl.BlockSpec((1, 1, ADA_TN), lambda l, j: (l, 0, j)),
        ],
        out_specs=pl.BlockSpec((1, 1, ADA_TN), lambda l, j: (l, 0, j)),
        out_shape=jax.ShapeDtypeStruct((depth, 1, n), F32),
        compiler_params=_cparams(("arbitrary", "arbitrary")),
        name="adaln",
    )(c_col, ada_w, ada_b3)


def _cond_spec(layer, n, d):
    return pl.BlockSpec((None, 1, d), lambda *_: (layer, 0, n))


def _gain_spec(layer, sub, d):
    return pl.BlockSpec((None, 1, d), lambda *_: (layer * 3 + sub, 0, 0))


def _norm_mod(x, g, sc, sh):
    ms = jnp.mean(x * x, axis=-1, keepdims=True)
    return (x * lax.rsqrt(ms + NORM_EPS) * g) * (1.0 + sc) + sh


def _ffn_kernel(x_hbm_ref, ng_ref, sh_ref, sc_ref, gt_ref, wg_hbm_ref, wu_hbm_ref, wd_hbm_ref,
                o_ref, h_ref, x_ref, wg_ref, wu_ref, wd_ref, x_sem, w_sem, *, layer):
    i = pl.program_id(0)
    n_i = pl.num_programs(0)
    nj = wg_hbm_ref.shape[2] // FFN_TF
    base = (i * nj) % 2

    def x_copy(tile):
        rows = pl.ds(pl.multiple_of(tile * FFN_TM, FFN_TM), FFN_TM)
        return pltpu.make_async_copy(x_hbm_ref.at[rows, :], x_ref, x_sem)

    def w_copies(jt, slot):
        cols = pl.ds(pl.multiple_of(jt * FFN_TF, FFN_TF), FFN_TF)
        return (pltpu.make_async_copy(wg_hbm_ref.at[layer, :, cols], wg_ref.at[slot], w_sem.at[0, slot]),
                pltpu.make_async_copy(wu_hbm_ref.at[layer, :, cols], wu_ref.at[slot], w_sem.at[1, slot]),
                pltpu.make_async_copy(wd_hbm_ref.at[layer, cols, :], wd_ref.at[slot], w_sem.at[2, slot]))

    def start_w(jt, slot):
        for cp in w_copies(jt, slot):
            cp.start()

    def wait_w(jt, slot):
        for cp in w_copies(jt, slot):
            cp.wait()

    def hidden_update(h, slot):
        wg = wg_ref[slot].astype(BF16)
        wu = wu_ref[slot].astype(BF16)
        wd = wd_ref[slot].astype(BF16)
        g = jnp.dot(h, wg, preferred_element_type=F32)
        u = jnp.dot(h, wu, preferred_element_type=F32)
        a = (g * jax.nn.sigmoid(g) * u).astype(BF16)
        return jnp.dot(a, wd, preferred_element_type=F32)

    half_gate = 0.5 * gt_ref[...]

    @pl.when(i == 0)
    def _():
        x_copy(0).start()
        start_w(0, 0)

    start_w(1, 1 - base)
    x_copy(i).wait()
    wait_w(0, base)
    for r in range(0, FFN_TM, FFN_NORM_ROWS):
        rows = pl.ds(r, FFN_NORM_ROWS)
        x = x_ref[rows, :]
        h = _norm_mod(x, ng_ref[...], sc_ref[...], sh_ref[...]).astype(BF16)
        h_ref[rows, :] = h
        o_ref[rows, :] = x + half_gate * hidden_update(h, base)

    @pl.when(i + 1 < n_i)
    def _():
        x_copy(i + 1).start()

    def hidden_tile(jt, carry):
        slot = (base + jt) % 2
        more_here = jt + 1 < nj

        @pl.when(more_here | (i + 1 < n_i))
        def _():
            start_w(jnp.where(more_here, jt + 1, 0), 1 - slot)

        wait_w(jt, slot)
        o_ref[...] += half_gate * hidden_update(h_ref[...], slot)
        return carry

    lax.fori_loop(1, nj, hidden_tile, 0)


def _ffn_call(layer, sub, x, ng3, mod, wg, wu, wd):
    L, d = x.shape
    dff = wg.shape[-1]
    assert dff % FFN_TF == 0 and dff // FFN_TF >= 2 and L % FFN_TM == 0
    hbm = pl.BlockSpec(memory_space=pl.ANY)
    return pl.pallas_call(
        functools.partial(_ffn_kernel, layer=layer),
        grid=(L // FFN_TM,),
        in_specs=[
            hbm,
            _gain_spec(layer, sub, d), _cond_spec(layer, 3 * sub, d),
            _cond_spec(layer, 3 * sub + 1, d), _cond_spec(layer, 3 * sub + 2, d),
            hbm, hbm, hbm,
        ],
        out_specs=pl.BlockSpec((FFN_TM, d), lambda i: (i, 0)),
        out_shape=jax.ShapeDtypeStruct((L, d), F32),
        scratch_shapes=[
            pltpu.VMEM((FFN_TM, d), BF16),
            pltpu.VMEM((FFN_TM, d), F32),
            pltpu.VMEM((2, d, FFN_TF), F32),
            pltpu.VMEM((2, d, FFN_TF), F32),
            pltpu.VMEM((2, FFN_TF, d), F32),
            pltpu.SemaphoreType.DMA,
            pltpu.SemaphoreType.DMA((3, 2)),
        ],
        compiler_params=_cparams(("arbitrary",)),
        name="ffn",
    )(x, ng3, mod, mod, mod, wg, wu, wd)


def _proj_kernel(x_ref, ng_ref, sh_ref, sc_ref, w_hbm_ref, gm_ref, qg_ref, kg_ref,
                 q_ref, k_ref, vt_ref, u_ref, w_ref, stage_ref, w_sem, *, layer):
    aw = ATT_WIDTH

    @pl.when(pl.program_id(0) == 0)
    def _():
        cw = stage_ref.shape[2]
        n_chunks = w_ref.shape[1] // cw

        def chunk_copy(c):
            return pltpu.make_async_copy(w_hbm_ref.at[layer, :, c * cw:(c + 1) * cw],
                                         stage_ref.at[c % 2], w_sem.at[c % 2])

        chunk_copy(0).start()
        for c in range(n_chunks):
            if c + 1 < n_chunks:
                chunk_copy(c + 1).start()
            chunk_copy(c).wait()
            w_ref[:, c * cw:(c + 1) * cw] = stage_ref[c % 2].astype(BF16)

    h = _norm_mod(x_ref[...], ng_ref[...], sc_ref[...], sh_ref[...]).astype(BF16)

    def head_norm(z, g):
        sq = (z * z).astype(BF16)
        lanes = gm_ref.shape[0]
        ms = jnp.concatenate(
            [jnp.dot(sq[:, c:c + lanes], gm_ref[...], preferred_element_type=F32)
             for c in range(0, aw, lanes)], axis=1)
        return z * lax.rsqrt(ms + NORM_EPS) * g

    q = jnp.dot(h, w_ref[:, 0:aw], preferred_element_type=F32)
    q_ref[...] = head_norm(q, qg_ref[...]).astype(BF16)
    k = jnp.dot(h, w_ref[:, aw:2 * aw], preferred_element_type=F32)
    k_ref[...] = head_norm(k, kg_ref[...]).astype(BF16)
    v = jnp.dot(h, w_ref[:, 2 * aw:3 * aw], preferred_element_type=F32)
    vt_ref[0] = v.T.astype(BF16)
    u_ref[...] = jnp.dot(h, w_ref[:, 3 * aw:4 * aw], preferred_element_type=F32)


def _proj_call(layer, x, ng3, mod, w_in, gmat, qg, kg):
    L, d = x.shape
    aw = ATT_WIDTH
    unit = ATT_TQ
    per_unit = unit // PROJ_TM
    vec_a = pl.BlockSpec((1, aw), lambda i: (0, 0))
    return pl.pallas_call(
        functools.partial(_proj_kernel, layer=layer),
        grid=(L // PROJ_TM,),
        in_specs=[
            pl.BlockSpec((PROJ_TM, d), lambda i: (i, 0)),
            _gain_spec(layer, 1, d), _cond_spec(layer, 3, d), _cond_spec(layer, 4, d),
            pl.BlockSpec(memory_space=pl.ANY),
            pl.BlockSpec((PROJ_NORM_LANES, PROJ_NORM_LANES), lambda i: (0, 0)),
            vec_a, vec_a,
        ],
        out_specs=[
            pl.BlockSpec((PROJ_TM, aw), lambda i: (i, 0)),
            pl.BlockSpec((PROJ_TM, aw), lambda i: (i, 0)),
            pl.BlockSpec((1, aw, PROJ_TM), lambda i: (i // per_unit, 0, i % per_unit)),
            pl.BlockSpec((PROJ_TM, SSM_WIDTH), lambda i: (i, 0)),
        ],
        out_shape=[
            jax.ShapeDtypeStruct((L, aw), BF16),
            jax.ShapeDtypeStruct((L, aw), BF16),
            jax.ShapeDtypeStruct((L // unit, aw, unit), BF16),
            jax.ShapeDtypeStruct((L, SSM_WIDTH), F32),
        ],
        scratch_shapes=[
            pltpu.VMEM((d, IN_WIDTH), BF16),
            pltpu.VMEM((2, d, PROJ_STAGE_COLS), F32),
            pltpu.SemaphoreType.DMA((2,)),
        ],
        compiler_params=_cparams(("arbitrary",)),
        name="in_proj",
    )(x, ng3, mod, mod, w_in, gmat, qg, kg)


def _attn_kernel(q_ref, k_ref, vt_ref, nbias_ref, cfar_ref, bound_ref, lq1_ref, lk1_ref, lq2_ref,
                 lk2_ref, sg_ref, o_ref, acc_ref, m_ref, l_ref, bias_ref, s_ref, p_ref, *, lam_init):
    t = ATT_T
    tq = ATT_TQ
    qb = tq // t
    n_tiles = q_ref.shape[0] // tq

    bound = bound_ref[0]
    far_bias = cfar_ref[0] - bound
    tiles = {2: jnp.broadcast_to(far_bias, (t, t)), 1: nbias_ref[0, 1] - bound,
             0: nbias_ref[0, 0] - bound, -1: jnp.full((t, t), NEG_INF, F32)}
    for d in range(-qb, 3):
        for a in range(qb):
            bias_ref[d + qb, :, a * t:(a + 1) * t] = tiles[max(min(d + a, 2), -1)]

    lam = (jnp.exp(jnp.sum(lq1_ref[...] * lk1_ref[...], keepdims=True))
           - jnp.exp(jnp.sum(lq2_ref[...] * lk2_ref[...], keepdims=True)) + lam_init)
    out_gain = sg_ref[...] * (1.0 - lam_init)

    def load_qcat(qi):
        q = q_ref[pl.ds(pl.multiple_of(qi * tq, tq), tq), :]
        lane = lax.broadcasted_iota(jnp.int32, q.shape, 1)
        zero = jnp.zeros_like(q)
        return jnp.concatenate([jnp.where(lane < DIFF_HEAD_DIM, q, zero),
                                jnp.where(lane >= DIFF_HEAD_DIM, q, zero)], axis=0)

    def scores(blk, qmat):
        off = pl.multiple_of(blk * t, t)
        return lax.dot_general(k_ref[pl.ds(off, t), :], qmat, (((1,), (1,)), ((), ())),
                               preferred_element_type=F32)

    def start_tile(qcat):
        acc_ref[...] = jnp.zeros(acc_ref.shape, F32)
        l_ref[...] = jnp.zeros(l_ref.shape, F32)
        p_ref[...] = jnp.zeros(p_ref.shape, BF16)
        s_ref[...] = scores(0, qcat)

    start_tile(load_qcat(0))

    def tile_body(qi, carry):
        qcat = load_qcat(qi)

        def biased(s, blk):
            bias = bias_ref[jnp.clip(qb * qi - blk, -qb, 2) + qb]
            return s[:, :tq] + bias, s[:, tq:] + bias

        def fast_unit(u, c):
            s_cur = s_ref[...]
            pv = jnp.dot(vt_ref[jnp.maximum(u - 1, 0), :, (qb - 1) * t:qb * t], p_ref[...],
                         preferred_element_type=F32)
            lsum = jnp.zeros((1, 2 * tq), F32)
            for b in range(qb):
                blk = u * qb + b
                s_next = scores(blk + 1, qcat)
                if b < qb - 1:
                    s0, s1 = s_cur[:, :tq] + far_bias, s_cur[:, tq:] + far_bias
                else:
                    s0, s1 = biased(s_cur, blk)
                p = jnp.concatenate([jnp.exp2(s0), jnp.exp2(s1)], axis=1)
                lsum = lsum + jnp.sum(p, axis=0, keepdims=True)
                if b < qb - 1:
                    pv = pv + jnp.dot(vt_ref[u, :, b * t:(b + 1) * t], p.astype(BF16),
                                      preferred_element_type=F32)
                else:
                    p_ref[...] = p.astype(BF16)
                s_cur = s_next
            s_ref[...] = s_cur
            acc_ref[...] += pv
            l_ref[...] += lsum
            return c

        lax.fori_loop(0, qi, fast_unit, 0)

        acc_ref[...] += jnp.dot(vt_ref[jnp.maximum(qi - 1, 0), :, (qb - 1) * t:qb * t], p_ref[...],
                                preferred_element_type=F32)
        s_cur = s_ref[...]
        for b in range(qb):
            w = (qb - b) * t
            if b < qb - 1:
                qn = jnp.concatenate([qcat[(b + 1) * t:tq], qcat[tq + (b + 1) * t:2 * tq]], axis=0)
                s_next = scores(qb * qi + b + 1, qn)
            bias = bias_ref[qb - b, :, b * t:tq]
            p0 = jnp.exp2(s_cur[:, :w] + bias)
            p1 = jnp.exp2(s_cur[:, w:] + bias)
            l_ref[:, b * t:tq] += jnp.sum(p0, axis=0, keepdims=True)
            l_ref[:, tq + b * t:2 * tq] += jnp.sum(p1, axis=0, keepdims=True)
            pv = jnp.dot(vt_ref[qi, :, b * t:(b + 1) * t],
                         jnp.concatenate([p0, p1], axis=1).astype(BF16),
                         preferred_element_type=F32)
            acc_ref[:, b * t:tq] += pv[:, :w]
            acc_ref[:, tq + b * t:2 * tq] += pv[:, w:]
            if b < qb - 1:
                s_cur = s_next

        l_min = jnp.min(l_ref[...], keepdims=True)
        underflow = jnp.logical_not(l_min[0, 0] > ATT_MIN_DENOM)

        @pl.when(underflow)
        def _():
            m_ref[...] = jnp.full(m_ref.shape, NEG_INF, F32)
            l_ref[...] = jnp.zeros(l_ref.shape, F32)
            acc_ref[...] = jnp.zeros(acc_ref.shape, F32)

            def exact_unit(u, c):
                for b in range(qb):
                    blk = u * qb + b
                    s = jnp.concatenate(biased(scores(blk, qcat), blk), axis=1)
                    m_old = m_ref[...]
                    m_new = jnp.maximum(m_old, jnp.max(s, axis=0, keepdims=True))
                    alpha = jnp.exp2(m_old - m_new)
                    p = jnp.exp2(s - m_new)
                    l_ref[...] = alpha * l_ref[...] + jnp.sum(p, axis=0, keepdims=True)
                    acc_ref[...] = alpha * acc_ref[...] + jnp.dot(
                        vt_ref[u, :, b * t:(b + 1) * t], p.astype(BF16),
                        preferred_element_type=F32)
                    m_ref[...] = m_new
                return c

            lax.fori_loop(0, qi + 1, exact_unit, 0)

        on = acc_ref[...] / l_ref[...]
        o = on[:, :tq] - lam * on[:, tq:]
        ms = jnp.mean(o * o, axis=0, keepdims=True)
        o = o * lax.rsqrt(ms + SUBLN_EPS) * out_gain
        o_ref[pl.ds(pl.multiple_of(qi * tq, tq), tq), :] = o.T.astype(BF16)
        start_tile(load_qcat(jnp.minimum(qi + 1, n_tiles - 1)))
        return carry

    lax.fori_loop(0, n_tiles, tile_body, 0)


def _attn_call(layer_idx, q, k, vt, near_bias, c_far, bound, lq1, lk1, lq2, lk2, sg_col):
    L = q.shape[0]
    t, tq = ATT_T, ATT_TQ
    assert tq % t == 0 and L % tq == 0
    lam_init = 0.8 - 0.6 * math.exp(-0.3 * layer_idx)
    lvec = pl.BlockSpec((1, DIFF_HEAD_DIM), lambda h: (0, 0))
    scalar = pl.BlockSpec((1, 1, 1), lambda h: (h, 0, 0))
    return pl.pallas_call(
        functools.partial(_attn_kernel, lam_init=lam_init),
        grid=(N_ATT_HEADS,),
        in_specs=[
            pl.BlockSpec((L, V_HEAD_DIM), lambda h: (0, h)),
            pl.BlockSpec((L, V_HEAD_DIM), lambda h: (0, h)),
            pl.BlockSpec((L // tq, V_HEAD_DIM, tq), lambda h: (0, h, 0)),
            pl.BlockSpec((1, 2, t, t), lambda h: (h, 0, 0, 0)),
            scalar, scalar,
            lvec, lvec, lvec, lvec,
            pl.BlockSpec((V_HEAD_DIM, 1), lambda h: (0, 0)),
        ],
        out_specs=pl.BlockSpec((L, V_HEAD_DIM), lambda h: (0, h)),
        out_shape=jax.ShapeDtypeStruct((L, ATT_WIDTH), BF16),
        scratch_shapes=[
            pltpu.VMEM((V_HEAD_DIM, 2 * tq), F32),
            pltpu.VMEM((1, 2 * tq), F32),
            pltpu.VMEM((1, 2 * tq), F32),
            pltpu.VMEM((tq // t + 3, t, tq), F32),
            pltpu.VMEM((t, 2 * tq), F32),
            pltpu.VMEM((t, 2 * tq), BF16),
        ],
        compiler_params=_cparams(("arbitrary",)),
        name="diff_attn",
    )(q, k, vt, near_bias, c_far, bound, lq1, lk1, lq2, lk2, sg_col)


def _ssm_kernel(u_ref, un_ref, bm_ref, cm_ref, lam_ref, lamr_ref, d_ref, y_ref,
                bu_a_ref, bu_b_ref, e_ref, carry_ref):
    ns = SSM_SLAB_GROUPS * SSM_STATE
    tt = SSM_TT
    strip = SSM_STRIP_ROWS
    n_strips = tt // strip
    nc = SSM_CHUNKS
    steps = strip // nc
    bm = bm_ref[0]

    lr = jnp.broadcast_to(lam_ref[0, 0:1, :], (nc, ns))
    li = jnp.broadcast_to(lam_ref[0, 1:2, :], (nc, ns))
    pr = lamr_ref[0, 0:1, :]
    pi = lamr_ref[0, 1:2, :]
    row_id = lax.broadcasted_iota(jnp.int32, (nc, ns), 0)
    z = jnp.zeros((nc, ns), F32)

    def advance(ref, r, sr, si, store):
        rows = pl.ds(nc * r, nc)
        nsr = lr * sr - li * si + ref[rows, 0:ns]
        nsi = lr * si + li * sr + ref[rows, ns:2 * ns]
        if store:
            ref[rows, 0:ns] = nsr
            ref[rows, ns:2 * ns] = nsi
        return nsr, nsi

    def input_strip(dst_ref, src, k):
        dst_ref[pl.ds(k * strip, strip), :] = jnp.dot(
            src(k * strip, strip).astype(BF16), bm, preferred_element_type=F32)

    def carry_in(er, ei):
        cr = carry_ref[0:1, 0:ns]
        ci = carry_ref[0:1, ns:2 * ns]
        sr, si = z, z
        for c in range(nc):
            sr = jnp.where(row_id == c, cr, sr)
            si = jnp.where(row_id == c, ci, si)
            cr, ci = (pr * cr - pi * ci + er[c:c + 1, :], pr * ci + pi * cr + ei[c:c + 1, :])
        carry_ref[0:1, 0:ns] = cr
        carry_ref[0:1, ns:2 * ns] = ci
        return sr, si

    def emit(ref, row0, k):
        rows = pl.ds(k * strip, strip)
        out_rows = pl.ds(row0 + k * strip, strip)
        y = (jnp.dot(ref[rows, :].astype(BF16), cm_ref[0], preferred_element_type=F32)
             + d_ref[0] * u_ref[out_rows, :])
        y_ref[out_rows, :] = y

    @pl.when(pl.program_id(1) == 0)
    def _():
        carry_ref[...] = jnp.zeros_like(carry_ref)
        bu_a_ref[...] = jnp.dot(u_ref[0:tt, :].astype(BF16), bm, preferred_element_type=F32)
        er, ei = z, z
        for r in range(tt // nc):
            er, ei = advance(bu_a_ref, r, er, ei, False)
        e_ref[:, 0:ns] = er
        e_ref[:, ns:2 * ns] = ei

    def phase(cur_ref, row0, cur_er, cur_ei, nxt_ref, nxt_src):
        sr, si = carry_in(cur_er, cur_ei)
        er, ei = z, z
        input_strip(nxt_ref, nxt_src, 0)
        for k in range(n_strips):
            if k + 1 < n_strips:
                input_strip(nxt_ref, nxt_src, k + 1)
            if k > 0:
                emit(cur_ref, row0, k - 1)
            for r in range(k * steps, (k + 1) * steps):
                sr, si = advance(cur_ref, r, sr, si, True)
                er, ei = advance(nxt_ref, r, er, ei, False)
        emit(cur_ref, row0, n_strips - 1)
        return er, ei

    er, ei = phase(bu_a_ref, 0, e_ref[:, 0:ns], e_ref[:, ns:2 * ns], bu_b_ref,
                   lambda r, n: u_ref[pl.ds(tt + r, n), :])
    er, ei = phase(bu_b_ref, tt, er, ei, bu_a_ref, lambda r, n: un_ref[pl.ds(r, n), :])
    e_ref[:, 0:ns] = er
    e_ref[:, ns:2 * ns] = ei


def _ssm_call(u_perm, bmat, cmat, lam2, lamr2, d3):
    L = u_perm.shape[0]
    n_slab = N_SSM_GROUPS // SSM_SLAB_GROUPS
    n_tiles = L // SSM_TT
    assert n_tiles % 2 == 0
    ns = SSM_SLAB_GROUPS * SSM_STATE
    cw = SSM_SLAB_GROUPS * SSM_GROUP
    return pl.pallas_call(
        _ssm_kernel,
        grid=(n_slab, n_tiles // 2),
        in_specs=[
            pl.BlockSpec((2 * SSM_TT, cw), lambda s, p: (p, s)),
            pl.BlockSpec((SSM_TT, cw), lambda s, p: (jnp.minimum(2 * p + 2, n_tiles - 1), s)),
            pl.BlockSpec((1, cw, 2 * ns), lambda s, p: (s, 0, 0)),
            pl.BlockSpec((1, 2 * ns, cw), lambda s, p: (s, 0, 0)),
            pl.BlockSpec((1, 2, ns), lambda s, p: (s, 0, 0)),
            pl.BlockSpec((1, 2, ns), lambda s, p: (s, 0, 0)),
            pl.BlockSpec((1, 1, cw), lambda s, p: (s, 0, 0)),
        ],
        out_specs=pl.BlockSpec((2 * SSM_TT, cw), lambda s, p: (p, s)),
        out_shape=jax.ShapeDtypeStruct((L, SSM_WIDTH), F32),
        scratch_shapes=[
            pltpu.VMEM((SSM_TT, 2 * ns), F32),
            pltpu.VMEM((SSM_TT, 2 * ns), F32),
            pltpu.VMEM((SSM_CHUNKS, 2 * ns), F32),
            pltpu.VMEM((1, 2 * ns), F32),
        ],
        compiler_params=_cparams(("arbitrary", "arbitrary")),
        name="s5_scan",
    )(u_perm, u_perm, bmat, cmat, lam2, lamr2, d3)


def _ssm_params(lam_re, lam_im, log_step, b_re, b_im, c_re, c_im, d):
    g, p, hc = N_SSM_GROUPS, SSM_STATE, SSM_GROUP
    sg = SSM_SLAB_GROUPS
    n_slab = g // sg
    lam = lax.complex(jnp.minimum(lam_re.astype(F32), -1e-4), lam_im.astype(F32))
    step = jnp.exp(log_step.astype(F32))[:, None]
    lam_bar = jnp.exp(lam * step)
    lam_bar_r = jnp.exp(lam * (step * (SSM_TT // SSM_CHUNKS)))
    b_bar = ((lam_bar - 1.0) / lam)[:, :, None] * lax.complex(b_re.astype(F32), b_im.astype(F32))
    eye = jnp.eye(sg, dtype=F32)

    def b_block(part):
        z = part.reshape(n_slab, sg, p, hc).transpose(0, 1, 3, 2)
        return (z[:, :, :, None, :] * eye[None, :, None, :, None]).reshape(n_slab, sg * hc, sg * p)

    def c_block(part):
        z = part.reshape(n_slab, sg, hc, p).transpose(0, 1, 3, 2)
        return (z[:, :, :, None, :] * eye[None, :, None, :, None]).reshape(n_slab, sg * p, sg * hc)

    bmat = jnp.concatenate([b_block(jnp.real(b_bar)), b_block(jnp.imag(b_bar))], axis=-1)
    cmat = jnp.concatenate([c_block(c_re.astype(F32)), c_block(-c_im.astype(F32))], axis=1)

    def rows(zc):
        return jnp.stack([jnp.real(zc).reshape(n_slab, sg * p),
                          jnp.imag(zc).reshape(n_slab, sg * p)], axis=1)

    d3 = d.astype(F32).reshape(n_slab, 1, sg * hc)
    return bmat.astype(BF16), cmat.astype(BF16), rows(lam_bar), rows(lam_bar_r), d3


def _out_kernel(att_ref, y_ref, gw_ref, gb_ref, woa_ref, wob_ref, x_ref, gt_ref, o_ref,
                gw_bf_ref, wo_bf_ref):
    aw = ATT_WIDTH

    @pl.when(pl.program_id(0) == 0)
    def _():
        gw_bf_ref[...] = gw_ref[...].astype(BF16)
        wo_bf_ref[0:aw, :] = woa_ref[...].astype(BF16)
        wo_bf_ref[aw:, :] = wob_ref[...].astype(BF16)

    y = jax.nn.gelu(y_ref[...])
    z = jnp.dot(y.astype(BF16), gw_bf_ref[...], preferred_element_type=F32) + gb_ref[...]
    yg = (y * jax.nn.sigmoid(z)).astype(BF16)
    m = (jnp.dot(att_ref[...], wo_bf_ref[0:aw, :], preferred_element_type=F32)
         + jnp.dot(yg, wo_bf_ref[aw:, :], preferred_element_type=F32))
    o_ref[...] = x_ref[...] + gt_ref[...] * m


def _out_call(layer, att, y, glu_w, glu_b, w_out, x, mod):
    L, d = x.shape
    aw, sw = ATT_WIDTH, SSM_WIDTH
    one = pl.Buffered(1)
    return pl.pallas_call(
        _out_kernel,
        grid=(L // OUT_TM,),
        in_specs=[
            pl.BlockSpec((OUT_TM, aw), lambda i: (i, 0)),
            pl.BlockSpec((OUT_TM, sw), lambda i: (i, 0)),
            pl.BlockSpec((None, sw, sw), lambda i: (layer, 0, 0), pipeline_mode=one),
            pl.BlockSpec((1, sw), lambda i: (0, 0)),
            pl.BlockSpec((None, aw, d), lambda i: (layer, 0, 0), pipeline_mode=one),
            pl.BlockSpec((None, sw, d), lambda i: (layer, 1, 0), pipeline_mode=one),
            pl.BlockSpec((OUT_TM, d), lambda i: (i, 0)),
            _cond_spec(layer, 5, d),
        ],
        out_specs=pl.BlockSpec((OUT_TM, d), lambda i: (i, 0)),
        out_shape=jax.ShapeDtypeStruct((L, d), F32),
        scratch_shapes=[pltpu.VMEM((sw, sw), BF16), pltpu.VMEM((aw + sw, d), BF16)],
        compiler_params=_cparams(("arbitrary",)),
        name="out_proj",
    )(att, y, glu_w, glu_b, w_out, w_out, x, mod)


def _t5_causal_buckets(dist):
    max_exact = N_BUCKETS // 2
    d = jnp.maximum(dist, 1).astype(F32)
    large = max_exact + (jnp.log(d / max_exact) / math.log(MAX_DISTANCE / max_exact)
                         * (N_BUCKETS - max_exact)).astype(jnp.int32)
    large = jnp.minimum(large, N_BUCKETS - 1)
    return jnp.where(dist < max_exact, dist, large)


def _near_bias(rel_bias):
    t = ATT_T
    assert t >= MAX_DISTANCE
    kk = jnp.arange(t, dtype=jnp.int32)[:, None]
    qq = jnp.arange(t, dtype=jnp.int32)[None, :]
    dist = jnp.stack([qq - kk, qq - kk + t], axis=0)
    bucket = _t5_causal_buckets(jnp.maximum(dist, 0))
    rb = rel_bias.astype(F32) * LOG2_E
    val = jnp.zeros((N_ATT_HEADS,) + dist.shape, F32)
    for b in range(N_BUCKETS):
        val = jnp.where((bucket == b)[None], rb[b][:, None, None, None], val)
    return jnp.where((dist >= 0)[None], val, NEG_INF)


def _chunk_interleave(a):
    L, w = a.shape
    return a.reshape(L // SSM_TT, SSM_CHUNKS, SSM_TT // SSM_CHUNKS, w).transpose(0, 2, 1, 3).reshape(L, w)


def _chunk_deinterleave(a):
    L, w = a.shape
    return a.reshape(L // SSM_TT, SSM_TT // SSM_CHUNKS, SSM_CHUNKS, w).transpose(0, 2, 1, 3).reshape(L, w)


def kernel(x, c, rel_bias, ada_w, ada_b, norm_g, ffn1_w_gate, ffn1_w_up, ffn1_w_down, ffn2_w_gate, ffn2_w_up, ffn2_w_down, w_in, w_out, q_norm_g, k_norm_g, lambda_q1, lambda_k1, lambda_q2, lambda_k2, subln_g, ssm_lambda_re, ssm_lambda_im, ssm_log_step, ssm_b_re, ssm_b_im, ssm_c_re, ssm_c_im, ssm_d, ssm_glu_w, ssm_glu_b):
    b, L, d = x.shape
    assert b == 1 and c.shape == (1, d)
    x2 = x.reshape(L, d)

    mod = _ada_call(c.reshape(d, 1), ada_w, ada_b.reshape(DEPTH, 1, N_COND * d))
    near_bias = _near_bias(rel_bias)
    c_far = (rel_bias.astype(F32)[N_BUCKETS - 1] * LOG2_E).reshape(N_ATT_HEADS, 1, 1)
    hd = DIFF_HEAD_DIM
    group_of = jnp.arange(PROJ_NORM_LANES) // hd
    gmat = jnp.where(group_of[:, None] == group_of[None, :], 1.0 / hd, 0.0).astype(BF16)
    n_rep = ATT_WIDTH // hd

    ng3 = norm_g.reshape(DEPTH * 3, 1, d)

    for i in range(DEPTH):
        x2 = _ffn_call(i, 0, x2, ng3, mod, ffn1_w_gate, ffn1_w_up, ffn1_w_down)

        qg = jnp.tile(q_norm_g[i].astype(F32), n_rep)[None] * (hd ** -0.5 * LOG2_E)
        kg = jnp.tile(k_norm_g[i].astype(F32), n_rep)[None]
        q, k, vt, u = _proj_call(i, x2, ng3, mod, w_in, gmat, qg, kg)
        bound = (DIFF_HEAD_DIM * jnp.max(jnp.abs(qg[0, :hd] * kg[0, :hd]))
                 + jnp.max(rel_bias.astype(F32), axis=0) * LOG2_E).reshape(N_ATT_HEADS, 1, 1)
        att = _attn_call(i, q, k, vt, near_bias, c_far, bound, lambda_q1[i][None], lambda_k1[i][None],
                         lambda_q2[i][None], lambda_k2[i][None], subln_g[i].reshape(V_HEAD_DIM, 1))
        bmat, cmat, lam2, lamr2, d3 = _ssm_params(
            ssm_lambda_re[i], ssm_lambda_im[i], ssm_log_step[i], ssm_b_re[i], ssm_b_im[i],
            ssm_c_re[i], ssm_c_im[i], ssm_d[i])
        y = _chunk_deinterleave(_ssm_call(_chunk_interleave(u), bmat, cmat, lam2, lamr2, d3))
        x2 = _out_call(i, att, y, ssm_glu_w, ssm_glu_b[i][None], w_out, x2, mod)

        x2 = _ffn_call(i, 2, x2, ng3, mod, ffn2_w_gate, ffn2_w_up, ffn2_w_down)
    return x2.reshape(b, L, d)
```

```python
import functools
import math

import jax
import jax.numpy as jnp
from jax import lax
from jax.experimental import pallas as pl
from jax.experimental.pallas import tpu as pltpu

D_MODEL = 2048
SEQ = 8192
DEPTH = 2
ATT_WIDTH = 1024
SSM_WIDTH = 1024
DIFF_HEAD_DIM = 64
V_HEAD_DIM = 128
N_ATT_HEADS = 8
SSM_GROUP = 16
N_SSM_GROUPS = 64
SSM_STATE = 64
IN_WIDTH = 4096
D_FF = 5632
N_BUCKETS = 32
MAX_DISTANCE = 128
N_COND = 9
NORM_EPS = 1e-6
SUBLN_EPS = 1e-5
NEG_INF = -1e30
LOG2_E = math.log2(math.e)

F32 = jnp.float32
BF16 = jnp.bfloat16

VMEM_LIMIT_BYTES = 62 * 1024 * 1024

ADA_TN = 2048
FFN_TM = 1024
FFN_TF = 512
FFN_NORM_ROWS = 512
PROJ_TM = 512
PROJ_NORM_LANES = 128
PROJ_STAGE_COLS = 512
ATT_T = 256
ATT_TQ = 1024
ATT_MIN_DENOM = 2.0 ** -40
SSM_CHUNKS = 8
SSM_TT = 1024
SSM_SLAB_GROUPS = 8
SSM_STRIP_ROWS = 128
OUT_TM = 512


def _cparams(sem):
    return pltpu.CompilerParams(dimension_semantics=sem, vmem_limit_bytes=VMEM_LIMIT_BYTES)


def _ada_kernel(c_ref, w_ref, b_ref, o_ref):
    c = c_ref[...]
    cs = c * jax.nn.sigmoid(c)
    o_ref[0] = jnp.sum(w_ref[0] * cs, axis=0, keepdims=True) + b_ref[0]


def _ada_call(c_col, ada_w, ada_b3):
    depth, d, n = ada_w.shape
    return pl.pallas_call(
        _ada_kernel,
        grid=(depth, n // ADA_TN),
        in_specs=[
            pl.BlockSpec((d, 1), lambda l, j: (0, 0)),
            pl.BlockSpec((1, d, ADA_TN), lambda l, j: (l, 0, j)),
            pl.BlockSpec((1, 1, ADA_TN), lambda l, j: (l, 0, j)),
        ],
        out_specs=pl.BlockSpec((1, 1, ADA_TN), lambda l, j: (l, 0, j)),
        out_shape=jax.ShapeDtypeStruct((depth, 1, n), F32),
        compiler_params=_cparams(("arbitrary", "arbitrary")),
        name="adaln",
    )(c_col, ada_w, ada_b3)


def _cond_spec(layer, n, d):
    return pl.BlockSpec((None, 1, d), lambda *_: (layer, 0, n))


def _gain_spec(layer, sub, d):
    return pl.BlockSpec((None, 1, d), lambda *_: (layer * 3 + sub, 0, 0))


def _norm_mod(x, g, sc, sh):
    ms = jnp.mean(x * x, axis=-1, keepdims=True)
    return (x * lax.rsqrt(ms + NORM_EPS) * g) * (1.0 + sc) + sh


def _ffn_kernel(x_hbm_ref, ng_ref, sh_ref, sc_ref, gt_ref, wg_hbm_ref, wu_hbm_ref, wd_hbm_ref,
                o_ref, h_ref, x_ref, wg_ref, wu_ref, wd_ref, x_sem, w_sem, *, layer):
    i = pl.program_id(0)
    n_i = pl.num_programs(0)
    nj = wg_hbm_ref.shape[2] // FFN_TF
    base = (i * nj) % 2

    def x_copy(tile):
        rows = pl.ds(pl.multiple_of(tile * FFN_TM, FFN_TM), FFN_TM)
        return pltpu.make_async_copy(x_hbm_ref.at[rows, :], x_ref, x_sem)

    def w_copies(jt, slot):
        cols = pl.ds(pl.multiple_of(jt * FFN_TF, FFN_TF), FFN_TF)
        return (pltpu.make_async_copy(wg_hbm_ref.at[layer, :, cols], wg_ref.at[slot], w_sem.at[0, slot]),
                pltpu.make_async_copy(wu_hbm_ref.at[layer, :, cols], wu_ref.at[slot], w_sem.at[1, slot]),
                pltpu.make_async_copy(wd_hbm_ref.at[layer, cols, :], wd_ref.at[slot], w_sem.at[2, slot]))

    def start_w(jt, slot):
        for cp in w_copies(jt, slot):
            cp.start()

    def wait_w(jt, slot):
        for cp in w_copies(jt, slot):
            cp.wait()

    def hidden_update(h, slot):
        wg = wg_ref[slot].astype(BF16)
        wu = wu_ref[slot].astype(BF16)
        wd = wd_ref[slot].astype(BF16)
        g = jnp.dot(h, wg, preferred_element_type=F32)
        u = jnp.dot(h, wu, preferred_element_type=F32)
        a = (g * jax.nn.sigmoid(g) * u).astype(BF16)
        return jnp.dot(a, wd, preferred_element_type=F32)

    half_gate = 0.5 * gt_ref[...]

    @pl.when(i == 0)
    def _():
        x_copy(0).start()
        start_w(0, 0)

    start_w(1, 1 - base)
    x_copy(i).wait()
    wait_w(0, base)
    for r in range(0, FFN_TM, FFN_NORM_ROWS):
        rows = pl.ds(r, FFN_NORM_ROWS)
        x = x_ref[rows, :]
        h = _norm_mod(x, ng_ref[...], sc_ref[...], sh_ref[...]).astype(BF16)
        h_ref[rows, :] = h
        o_ref[rows, :] = x + half_gate * hidden_update(h, base)

    @pl.when(i + 1 < n_i)
    def _():
        x_copy(i + 1).start()

    def hidden_tile(jt, carry):
        slot = (base + jt) % 2
        more_here = jt + 1 < nj

        @pl.when(more_here | (i + 1 < n_i))
        def _():
            start_w(jnp.where(more_here, jt + 1, 0), 1 - slot)

        wait_w(jt, slot)
        o_ref[...] += half_gate * hidden_update(h_ref[...], slot)
        return carry

    lax.fori_loop(1, nj, hidden_tile, 0)


def _ffn_call(layer, sub, x, ng3, mod, wg, wu, wd):
    L, d = x.shape
    dff = wg.shape[-1]
    assert dff % FFN_TF == 0 and dff // FFN_TF >= 2 and L % FFN_TM == 0
    hbm = pl.BlockSpec(memory_space=pl.ANY)
    return pl.pallas_call(
        functools.partial(_ffn_kernel, layer=layer),
        grid=(L // FFN_TM,),
        in_specs=[
            hbm,
            _gain_spec(layer, sub, d), _cond_spec(layer, 3 * sub, d),
            _cond_spec(layer, 3 * sub + 1, d), _cond_spec(layer, 3 * sub + 2, d),
            hbm, hbm, hbm,
        ],
        out_specs=pl.BlockSpec((FFN_TM, d), lambda i: (i, 0)),
        out_shape=jax.ShapeDtypeStruct((L, d), F32),
        scratch_shapes=[
            pltpu.VMEM((FFN_TM, d), BF16),
            pltpu.VMEM((FFN_TM, d), F32),
            pltpu.VMEM((2, d, FFN_TF), F32),
            pltpu.VMEM((2, d, FFN_TF), F32),
            pltpu.VMEM((2, FFN_TF, d), F32),
            pltpu.SemaphoreType.DMA,
            pltpu.SemaphoreType.DMA((3, 2)),
        ],
        compiler_params=_cparams(("arbitrary",)),
        name="ffn",
    )(x, ng3, mod, mod, mod, wg, wu, wd)


def _proj_kernel(x_ref, ng_ref, sh_ref, sc_ref, w_hbm_ref, gm_ref, qg_ref, kg_ref,
                 q_ref, k_ref, vt_ref, u_ref, w_ref, stage_ref, w_sem, *, layer):
    aw = ATT_WIDTH

    @pl.when(pl.program_id(0) == 0)
    def _():
        cw = stage_ref.shape[2]
        n_chunks = w_ref.shape[1] // cw

        def chunk_copy(c):
            return pltpu.make_async_copy(w_hbm_ref.at[layer, :, c * cw:(c + 1) * cw],
                                         stage_ref.at[c % 2], w_sem.at[c % 2])

        chunk_copy(0).start()
        for c in range(n_chunks):
            if c + 1 < n_chunks:
                chunk_copy(c + 1).start()
            chunk_copy(c).wait()
            w_ref[:, c * cw:(c + 1) * cw] = stage_ref[c % 2].astype(BF16)

    h = _norm_mod(x_ref[...], ng_ref[...], sc_ref[...], sh_ref[...]).astype(BF16)

    def head_norm(z, g):
        sq = (z * z).astype(BF16)
        lanes = gm_ref.shape[0]
        ms = jnp.concatenate(
            [jnp.dot(sq[:, c:c + lanes], gm_ref[...], preferred_element_type=F32)
             for c in range(0, aw, lanes)], axis=1)
        return z * lax.rsqrt(ms + NORM_EPS) * g

    q = jnp.dot(h, w_ref[:, 0:aw], preferred_element_type=F32)
    q_ref[...] = head_norm(q, qg_ref[...]).astype(BF16)
    k = jnp.dot(h, w_ref[:, aw:2 * aw], preferred_element_type=F32)
    k_ref[...] = head_norm(k, kg_ref[...]).astype(BF16)
    v = jnp.dot(h, w_ref[:, 2 * aw:3 * aw], preferred_element_type=F32)
    vt_ref[0] = v.T.astype(BF16)
    u_ref[...] = jnp.dot(h, w_ref[:, 3 * aw:4 * aw], preferred_element_type=F32)


def _proj_call(layer, x, ng3, mod, w_in, gmat, qg, kg):
    L, d = x.shape
    aw = ATT_WIDTH
    unit = ATT_TQ
    per_unit = unit // PROJ_TM
    vec_a = pl.BlockSpec((1, aw), lambda i: (0, 0))
    return pl.pallas_call(
        functools.partial(_proj_kernel, layer=layer),
        grid=(L // PROJ_TM,),
        in_specs=[
            pl.BlockSpec((PROJ_TM, d), lambda i: (i, 0)),
            _gain_spec(layer, 1, d), _cond_spec(layer, 3, d), _cond_spec(layer, 4, d),
            pl.BlockSpec(memory_space=pl.ANY),
            pl.BlockSpec((PROJ_NORM_LANES, PROJ_NORM_LANES), lambda i: (0, 0)),
            vec_a, vec_a,
        ],
        out_specs=[
            pl.BlockSpec((PROJ_TM, aw), lambda i: (i, 0)),
            pl.BlockSpec((PROJ_TM, aw), lambda i: (i, 0)),
            pl.BlockSpec((1, aw, PROJ_TM), lambda i: (i // per_unit, 0, i % per_unit)),
            pl.BlockSpec((PROJ_TM, SSM_WIDTH), lambda i: (i, 0)),
        ],
        out_shape=[
            jax.ShapeDtypeStruct((L, aw), BF16),
            jax.ShapeDtypeStruct((L, aw), BF16),
            jax.ShapeDtypeStruct((L // unit, aw, unit), BF16),
            jax.ShapeDtypeStruct((L, SSM_WIDTH), F32),
        ],
        scratch_shapes=[
            pltpu.VMEM((d, IN_WIDTH), BF16),
            pltpu.VMEM((2, d, PROJ_STAGE_COLS), F32),
            pltpu.SemaphoreType.DMA((2,)),
        ],
        compiler_params=_cparams(("arbitrary",)),
        name="in_proj",
    )(x, ng3, mod, mod, w_in, gmat, qg, kg)


def _attn_kernel(q_ref, k_ref, vt_ref, nbias_ref, cfar_ref, bound_ref, lq1_ref, lk1_ref, lq2_ref,
                 lk2_ref, sg_ref, o_ref, acc_ref, m_ref, l_ref, bias_ref, s_ref, p_ref, *, lam_init):
    t = ATT_T
    tq = ATT_TQ
    qb = tq // t
    n_tiles = q_ref.shape[0] // tq

    bound = bound_ref[0]
    far_bias = cfar_ref[0] - bound
    tiles = {2: jnp.broadcast_to(far_bias, (t, t)), 1: nbias_ref[0, 1] - bound,
             0: nbias_ref[0, 0] - bound, -1: jnp.full((t, t), NEG_INF, F32)}
    for d in range(-qb, 3):
        for a in range(qb):
            bias_ref[d + qb, :, a * t:(a + 1) * t] = tiles[max(min(d + a, 2), -1)]

    lam = (jnp.exp(jnp.sum(lq1_ref[...] * lk1_ref[...], keepdims=True))
           - jnp.exp(jnp.sum(lq2_ref[...] * lk2_ref[...], keepdims=True)) + lam_init)
    out_gain = sg_ref[...] * (1.0 - lam_init)

    def load_qcat(qi):
        q = q_ref[pl.ds(pl.multiple_of(qi * tq, tq), tq), :]
        lane = lax.broadcasted_iota(jnp.int32, q.shape, 1)
        zero = jnp.zeros_like(q)
        return jnp.concatenate([jnp.where(lane < DIFF_HEAD_DIM, q, zero),
                                jnp.where(lane >= DIFF_HEAD_DIM, q, zero)], axis=0)

    def scores(blk, qmat):
        off = pl.multiple_of(blk * t, t)
        return lax.dot_general(k_ref[pl.ds(off, t), :], qmat, (((1,), (1,)), ((), ())),
                               preferred_element_type=F32)

    def start_tile(qcat):
        acc_ref[...] = jnp.zeros(acc_ref.shape, F32)
        l_ref[...] = jnp.zeros(l_ref.shape, F32)
        p_ref[...] = jnp.zeros(p_ref.shape, BF16)
        s_ref[...] = scores(0, qcat)

    start_tile(load_qcat(0))

    def tile_body(qi, carry):
        qcat = load_qcat(qi)

        def biased(s, blk):
            bias = bias_ref[jnp.clip(qb * qi - blk, -qb, 2) + qb]
            return s[:, :tq] + bias, s[:, tq:] + bias

        def fast_unit(u, c):
            s_cur = s_ref[...]
            v_prev = vt_ref[jnp.maximum(u - 1, 0), :, (qb - 1) * t:qb * t]
            pv = [jnp.dot(v_prev, p_ref[:, m * tq:(m + 1) * tq], preferred_element_type=F32)
                  for m in range(2)]
            lsum = [jnp.zeros((1, tq), F32), jnp.zeros((1, tq), F32)]
            for b in range(qb):
                blk = u * qb + b
                s_next = scores(blk + 1, qcat)
                if b < qb - 1:
                    sb = (s_cur[:, :tq] + far_bias, s_cur[:, tq:] + far_bias)
                else:
                    sb = biased(s_cur, blk)
                for m in range(2):
                    p = jnp.exp2(sb[m])
                    lsum[m] = lsum[m] + jnp.sum(p, axis=0, keepdims=True)
                    if b < qb - 1:
                        pv[m] = pv[m] + jnp.dot(vt_ref[u, :, b * t:(b + 1) * t], p.astype(BF16),
                                                preferred_element_type=F32)
                    else:
                        p_ref[:, m * tq:(m + 1) * tq] = p.astype(BF16)
                s_cur = s_next
            s_ref[...] = s_cur
            for m in range(2):
                acc_ref[:, m * tq:(m + 1) * tq] += pv[m]
                l_ref[:, m * tq:(m + 1) * tq] += lsum[m]
            return c

        lax.fori_loop(0, qi, fast_unit, 0)

        acc_ref[...] += jnp.dot(vt_ref[jnp.maximum(qi - 1, 0), :, (qb - 1) * t:qb * t], p_ref[...],
                                preferred_element_type=F32)
        s_cur = s_ref[...]
        for b in range(qb):
            w = (qb - b) * t
            if b < qb - 1:
                qn = jnp.concatenate([qcat[(b + 1) * t:tq], qcat[tq + (b + 1) * t:2 * tq]], axis=0)
                s_next = scores(qb * qi + b + 1, qn)
            bias = bias_ref[qb - b, :, b * t:tq]
            p0 = jnp.exp2(s_cur[:, :w] + bias)
            p1 = jnp.exp2(s_cur[:, w:] + bias)
            l_ref[:, b * t:tq] += jnp.sum(p0, axis=0, keepdims=True)
            l_ref[:, tq + b * t:2 * tq] += jnp.sum(p1, axis=0, keepdims=True)
            pv = jnp.dot(vt_ref[qi, :, b * t:(b + 1) * t],
                         jnp.concatenate([p0, p1], axis=1).astype(BF16),
                         preferred_element_type=F32)
            acc_ref[:, b * t:tq] += pv[:, :w]
            acc_ref[:, tq + b * t:2 * tq] += pv[:, w:]
            if b < qb - 1:
                s_cur = s_next

        l_min = jnp.min(l_ref[...], keepdims=True)
        underflow = jnp.logical_not(l_min[0, 0] > ATT_MIN_DENOM)

        @pl.when(underflow)
        def _():
            m_ref[...] = jnp.full(m_ref.shape, NEG_INF, F32)
            l_ref[...] = jnp.zeros(l_ref.shape, F32)
            acc_ref[...] = jnp.zeros(acc_ref.shape, F32)

            def exact_unit(u, c):
                for b in range(qb):
                    blk = u * qb + b
                    s = jnp.concatenate(biased(scores(blk, qcat), blk), axis=1)
                    m_old = m_ref[...]
                    m_new = jnp.maximum(m_old, jnp.max(s, axis=0, keepdims=True))
                    alpha = jnp.exp2(m_old - m_new)
                    p = jnp.exp2(s - m_new)
                    l_ref[...] = alpha * l_ref[...] + jnp.sum(p, axis=0, keepdims=True)
                    acc_ref[...] = alpha * acc_ref[...] + jnp.dot(
                        vt_ref[u, :, b * t:(b + 1) * t], p.astype(BF16),
                        preferred_element_type=F32)
                    m_ref[...] = m_new
                return c

            lax.fori_loop(0, qi + 1, exact_unit, 0)

        on = acc_ref[...] / l_ref[...]
        o = on[:, :tq] - lam * on[:, tq:]
        ms = jnp.mean(o * o, axis=0, keepdims=True)
        o = o * lax.rsqrt(ms + SUBLN_EPS) * out_gain
        o_ref[pl.ds(pl.multiple_of(qi * tq, tq), tq), :] = o.T.astype(BF16)
        start_tile(load_qcat(jnp.minimum(qi + 1, n_tiles - 1)))
        return carry

    lax.fori_loop(0, n_tiles, tile_body, 0)


def _attn_call(layer_idx, q, k, vt, near_bias, c_far, bound, lq1, lk1, lq2, lk2, sg_col):
    L = q.shape[0]
    t, tq = ATT_T, ATT_TQ
    assert tq % t == 0 and L % tq == 0
    lam_init = 0.8 - 0.6 * math.exp(-0.3 * layer_idx)
    lvec = pl.BlockSpec((1, DIFF_HEAD_DIM), lambda h: (0, 0))
    scalar = pl.BlockSpec((1, 1, 1), lambda h: (h, 0, 0))
    return pl.pallas_call(
        functools.partial(_attn_kernel, lam_init=lam_init),
        grid=(N_ATT_HEADS,),
        in_specs=[
            pl.BlockSpec((L, V_HEAD_DIM), lambda h: (0, h)),
            pl.BlockSpec((L, V_HEAD_DIM), lambda h: (0, h)),
            pl.BlockSpec((L // tq, V_HEAD_DIM, tq), lambda h: (0, h, 0)),
            pl.BlockSpec((1, 2, t, t), lambda h: (h, 0, 0, 0)),
            scalar, scalar,
            lvec, lvec, lvec, lvec,
            pl.BlockSpec((V_HEAD_DIM, 1), lambda h: (0, 0)),
        ],
        out_specs=pl.BlockSpec((L, V_HEAD_DIM), lambda h: (0, h)),
        out_shape=jax.ShapeDtypeStruct((L, ATT_WIDTH), BF16),
        scratch_shapes=[
            pltpu.VMEM((V_HEAD_DIM, 2 * tq), F32),
            pltpu.VMEM((1, 2 * tq), F32),
            pltpu.VMEM((1, 2 * tq), F32),
            pltpu.VMEM((tq // t + 3, t, tq), F32),
            pltpu.VMEM((t, 2 * tq), F32),
            pltpu.VMEM((t, 2 * tq), BF16),
        ],
        compiler_params=_cparams(("arbitrary",)),
        name="diff_attn",
    )(q, k, vt, near_bias, c_far, bound, lq1, lk1, lq2, lk2, sg_col)


def _ssm_kernel(u_ref, un_ref, bm_ref, cm_ref, lam_ref, lamr_ref, d_ref, y_ref,
                bu_a_ref, bu_b_ref, e_ref, carry_ref):
    ns = SSM_SLAB_GROUPS * SSM_STATE
    tt = SSM_TT
    strip = SSM_STRIP_ROWS
    n_strips = tt // strip
    nc = SSM_CHUNKS
    steps = strip // nc
    bm = bm_ref[0]

    lr = jnp.broadcast_to(lam_ref[0, 0:1, :], (nc, ns))
    li = jnp.broadcast_to(lam_ref[0, 1:2, :], (nc, ns))
    pr = lamr_ref[0, 0:1, :]
    pi = lamr_ref[0, 1:2, :]
    row_id = lax.broadcasted_iota(jnp.int32, (nc, ns), 0)
    z = jnp.zeros((nc, ns), F32)

    def advance(ref, r, sr, si, store):
        rows = pl.ds(nc * r, nc)
        nsr = lr * sr - li * si + ref[rows, 0:ns]
        nsi = lr * si + li * sr + ref[rows, ns:2 * ns]
        if store:
            ref[rows, 0:ns] = nsr
            ref[rows, ns:2 * ns] = nsi
        return nsr, nsi

    def input_strip(dst_ref, src, k):
        dst_ref[pl.ds(k * strip, strip), :] = jnp.dot(
            src(k * strip, strip).astype(BF16), bm, preferred_element_type=F32)

    def carry_in(er, ei):
        cr = carry_ref[0:1, 0:ns]
        ci = carry_ref[0:1, ns:2 * ns]
        sr, si = z, z
        for c in range(nc):
            sr = jnp.where(row_id == c, cr, sr)
            si = jnp.where(row_id == c, ci, si)
            cr, ci = (pr * cr - pi * ci + er[c:c + 1, :], pr * ci + pi * cr + ei[c:c + 1, :])
        carry_ref[0:1, 0:ns] = cr
        carry_ref[0:1, ns:2 * ns] = ci
        return sr, si

    def emit(ref, row0, k):
        rows = pl.ds(k * strip, strip)
        out_rows = pl.ds(row0 + k * strip, strip)
        y = (jnp.dot(ref[rows, :].astype(BF16), cm_ref[0], preferred_element_type=F32)
             + d_ref[0] * u_ref[out_rows, :])
        y_ref[out_rows, :] = y

    @pl.when(pl.program_id(1) == 0)
    def _():
        carry_ref[...] = jnp.zeros_like(carry_ref)
        bu_a_ref[...] = jnp.dot(u_ref[0:tt, :].astype(BF16), bm, preferred_element_type=F32)
        er, ei = z, z
        for r in range(tt // nc):
            er, ei = advance(bu_a_ref, r, er, ei, False)
        e_ref[:, 0:ns] = er
        e_ref[:, ns:2 * ns] = ei

    def phase(cur_ref, row0, cur_er, cur_ei, nxt_ref, nxt_src):
        sr, si = carry_in(cur_er, cur_ei)
        er, ei = z, z
        input_strip(nxt_ref, nxt_src, 0)
        for k in range(n_strips):
            if k + 1 < n_strips:
                input_strip(nxt_ref, nxt_src, k + 1)
            if k > 0:
                emit(cur_ref, row0, k - 1)
            for r in range(k * steps, (k + 1) * steps):
                sr, si = advance(cur_ref, r, sr, si, True)
                er, ei = advance(nxt_ref, r, er, ei, False)
        emit(cur_ref, row0, n_strips - 1)
        return er, ei

    er, ei = phase(bu_a_ref, 0, e_ref[:, 0:ns], e_ref[:, ns:2 * ns], bu_b_ref,
                   lambda r, n: u_ref[pl.ds(tt + r, n), :])
    er, ei = phase(bu_b_ref, tt, er, ei, bu_a_ref, lambda r, n: un_ref[pl.ds(r, n), :])
    e_ref[:, 0:ns] = er
    e_ref[:, ns:2 * ns] = ei


def _ssm_call(u_perm, bmat, cmat, lam2, lamr2, d3):
    L = u_perm.shape[0]
    n_slab = N_SSM_GROUPS // SSM_SLAB_GROUPS
    n_tiles = L // SSM_TT
    assert n_tiles % 2 == 0
    ns = SSM_SLAB_GROUPS * SSM_STATE
    cw = SSM_SLAB_GROUPS * SSM_GROUP
    return pl.pallas_call(
        _ssm_kernel,
        grid=(n_slab, n_tiles // 2),
        in_specs=[
            pl.BlockSpec((2 * SSM_TT, cw), lambda s, p: (p, s)),
            pl.BlockSpec((SSM_TT, cw), lambda s, p: (jnp.minimum(2 * p + 2, n_tiles - 1), s)),
            pl.BlockSpec((1, cw, 2 * ns), lambda s, p: (s, 0, 0)),
            pl.BlockSpec((1, 2 * ns, cw), lambda s, p: (s, 0, 0)),
            pl.BlockSpec((1, 2, ns), lambda s, p: (s, 0, 0)),
            pl.BlockSpec((1, 2, ns), lambda s, p: (s, 0, 0)),
            pl.BlockSpec((1, 1, cw), lambda s, p: (s, 0, 0)),
        ],
        out_specs=pl.BlockSpec((2 * SSM_TT, cw), lambda s, p: (p, s)),
        out_shape=jax.ShapeDtypeStruct((L, SSM_WIDTH), F32),
        scratch_shapes=[
            pltpu.VMEM((SSM_TT, 2 * ns), F32),
            pltpu.VMEM((SSM_TT, 2 * ns), F32),
            pltpu.VMEM((SSM_CHUNKS, 2 * ns), F32),
            pltpu.VMEM((1, 2 * ns), F32),
        ],
        compiler_params=_cparams(("arbitrary", "arbitrary")),
        name="s5_scan",
    )(u_perm, u_perm, bmat, cmat, lam2, lamr2, d3)


def _ssm_params(lam_re, lam_im, log_step, b_re, b_im, c_re, c_im, d):
    g, p, hc = N_SSM_GROUPS, SSM_STATE, SSM_GROUP
    sg = SSM_SLAB_GROUPS
    n_slab = g // sg
    lam = lax.complex(jnp.minimum(lam_re.astype(F32), -1e-4), lam_im.astype(F32))
    step = jnp.exp(log_step.astype(F32))[:, None]
    lam_bar = jnp.exp(lam * step)
    lam_bar_r = jnp.exp(lam * (step * (SSM_TT // SSM_CHUNKS)))
    b_bar = ((lam_bar - 1.0) / lam)[:, :, None] * lax.complex(b_re.astype(F32), b_im.astype(F32))
    eye = jnp.eye(sg, dtype=F32)

    def b_block(part):
        z = part.reshape(n_slab, sg, p, hc).transpose(0, 1, 3, 2)
        return (z[:, :, :, None, :] * eye[None, :, None, :, None]).reshape(n_slab, sg * hc, sg * p)

    def c_block(part):
        z = part.reshape(n_slab, sg, hc, p).transpose(0, 1, 3, 2)
        return (z[:, :, :, None, :] * eye[None, :, None, :, None]).reshape(n_slab, sg * p, sg * hc)

    bmat = jnp.concatenate([b_block(jnp.real(b_bar)), b_block(jnp.imag(b_bar))], axis=-1)
    cmat = jnp.concatenate([c_block(c_re.astype(F32)), c_block(-c_im.astype(F32))], axis=1)

    def rows(zc):
        return jnp.stack([jnp.real(zc).reshape(n_slab, sg * p),
                          jnp.imag(zc).reshape(n_slab, sg * p)], axis=1)

    d3 = d.astype(F32).reshape(n_slab, 1, sg * hc)
    return bmat.astype(BF16), cmat.astype(BF16), rows(lam_bar), rows(lam_bar_r), d3


def _out_kernel(att_ref, y_ref, gw_ref, gb_ref, woa_ref, wob_ref, x_ref, gt_ref, o_ref,
                gw_bf_ref, wo_bf_ref):
    aw = ATT_WIDTH

    @pl.when(pl.program_id(0) == 0)
    def _():
        gw_bf_ref[...] = gw_ref[...].astype(BF16)
        wo_bf_ref[0:aw, :] = woa_ref[...].astype(BF16)
        wo_bf_ref[aw:, :] = wob_ref[...].astype(BF16)

    y = jax.nn.gelu(y_ref[...])
    z = jnp.dot(y.astype(BF16), gw_bf_ref[...], preferred_element_type=F32) + gb_ref[...]
    yg = (y * jax.nn.sigmoid(z)).astype(BF16)
    m = (jnp.dot(att_ref[...], wo_bf_ref[0:aw, :], preferred_element_type=F32)
         + jnp.dot(yg, wo_bf_ref[aw:, :], preferred_element_type=F32))
    o_ref[...] = x_ref[...] + gt_ref[...] * m


def _out_call(layer, att, y, glu_w, glu_b, w_out, x, mod):
    L, d = x.shape
    aw, sw = ATT_WIDTH, SSM_WIDTH
    one = pl.Buffered(1)
    return pl.pallas_call(
        _out_kernel,
        grid=(L // OUT_TM,),
        in_specs=[
            pl.BlockSpec((OUT_TM, aw), lambda i: (i, 0)),
            pl.BlockSpec((OUT_TM, sw), lambda i: (i, 0)),
            pl.BlockSpec((None, sw, sw), lambda i: (layer, 0, 0), pipeline_mode=one),
            pl.BlockSpec((1, sw), lambda i: (0, 0)),
            pl.BlockSpec((None, aw, d), lambda i: (layer, 0, 0), pipeline_mode=one),
            pl.BlockSpec((None, sw, d), lambda i: (layer, 1, 0), pipeline_mode=one),
            pl.BlockSpec((OUT_TM, d), lambda i: (i, 0)),
            _cond_spec(layer, 5, d),
        ],
        out_specs=pl.BlockSpec((OUT_TM, d), lambda i: (i, 0)),
        out_shape=jax.ShapeDtypeStruct((L, d), F32),
        scratch_shapes=[pltpu.VMEM((sw, sw), BF16), pltpu.VMEM((aw + sw, d), BF16)],
        compiler_params=_cparams(("arbitrary",)),
        name="out_proj",
    )(att, y, glu_w, glu_b, w_out, w_out, x, mod)


def _t5_causal_buckets(dist):
    max_exact = N_BUCKETS // 2
    d = jnp.maximum(dist, 1).astype(F32)
    large = max_exact + (jnp.log(d / max_exact) / math.log(MAX_DISTANCE / max_exact)
                         * (N_BUCKETS - max_exact)).astype(jnp.int32)
    large = jnp.minimum(large, N_BUCKETS - 1)
    return jnp.where(dist < max_exact, dist, large)


def _near_bias(rel_bias):
    t = ATT_T
    assert t >= MAX_DISTANCE
    kk = jnp.arange(t, dtype=jnp.int32)[:, None]
    qq = jnp.arange(t, dtype=jnp.int32)[None, :]
    dist = jnp.stack([qq - kk, qq - kk + t], axis=0)
    bucket = _t5_causal_buckets(jnp.maximum(dist, 0))
    rb = rel_bias.astype(F32) * LOG2_E
    val = jnp.zeros((N_ATT_HEADS,) + dist.shape, F32)
    for b in range(N_BUCKETS):
        val = jnp.where((bucket == b)[None], rb[b][:, None, None, None], val)
    return jnp.where((dist >= 0)[None], val, NEG_INF)


def _chunk_interleave(a):
    L, w = a.shape
    return a.reshape(L // SSM_TT, SSM_CHUNKS, SSM_TT // SSM_CHUNKS, w).transpose(0, 2, 1, 3).reshape(L, w)


def _chunk_deinterleave(a):
    L, w = a.shape
    return a.reshape(L // SSM_TT, SSM_TT // SSM_CHUNKS, SSM_CHUNKS, w).transpose(0, 2, 1, 3).reshape(L, w)


def kernel(x, c, rel_bias, ada_w, ada_b, norm_g, ffn1_w_gate, ffn1_w_up, ffn1_w_down, ffn2_w_gate, ffn2_w_up, ffn2_w_down, w_in, w_out, q_norm_g, k_norm_g, lambda_q1, lambda_k1, lambda_q2, lambda_k2, subln_g, ssm_lambda_re, ssm_lambda_im, ssm_log_step, ssm_b_re, ssm_b_im, ssm_c_re, ssm_c_im, ssm_d, ssm_glu_w, ssm_glu_b):
    b, L, d = x.shape
    assert b == 1 and c.shape == (1, d)
    x2 = x.reshape(L, d)

    mod = _ada_call(c.reshape(d, 1), ada_w, ada_b.reshape(DEPTH, 1, N_COND * d))
    near_bias = _near_bias(rel_bias)
    c_far = (rel_bias.astype(F32)[N_BUCKETS - 1] * LOG2_E).reshape(N_ATT_HEADS, 1, 1)
    hd = DIFF_HEAD_DIM
    group_of = jnp.arange(PROJ_NORM_LANES) // hd
    gmat = jnp.where(group_of[:, None] == group_of[None, :], 1.0 / hd, 0.0).astype(BF16)
    n_rep = ATT_WIDTH // hd

    ng3 = norm_g.reshape(DEPTH * 3, 1, d)

    for i in range(DEPTH):
        x2 = _ffn_call(i, 0, x2, ng3, mod, ffn1_w_gate, ffn1_w_up, ffn1_w_down)

        qg = jnp.tile(q_norm_g[i].astype(F32), n_rep)[None] * (hd ** -0.5 * LOG2_E)
        kg = jnp.tile(k_norm_g[i].astype(F32), n_rep)[None]
        q, k, vt, u = _proj_call(i, x2, ng3, mod, w_in, gmat, qg, kg)
        bound = (DIFF_HEAD_DIM * jnp.max(jnp.abs(qg[0, :hd] * kg[0, :hd]))
                 + jnp.max(rel_bias.astype(F32), axis=0) * LOG2_E).reshape(N_ATT_HEADS, 1, 1)
        att = _attn_call(i, q, k, vt, near_bias, c_far, bound, lambda_q1[i][None], lambda_k1[i][None],
                         lambda_q2[i][None], lambda_k2[i][None], subln_g[i].reshape(V_HEAD_DIM, 1))
        bmat, cmat, lam2, lamr2, d3 = _ssm_params(
            ssm_lambda_re[i], ssm_lambda_im[i], ssm_log_step[i], ssm_b_re[i], ssm_b_im[i],
            ssm_c_re[i], ssm_c_im[i], ssm_d[i])
        y = _chunk_deinterleave(_ssm_call(_chunk_interleave(u), bmat, cmat, lam2, lamr2, d3))
        x2 = _out_call(i, att, y, ssm_glu_w, ssm_glu_b[i][None], w_out, x2, mod)

        x2 = _ffn_call(i, 2, x2, ng3, mod, ffn2_w_gate, ffn2_w_up, ffn2_w_down)
    return x2.reshape(b, L, d)
```
